```python
import jax, jax.numpy as jnp
from jax import lax
import numpy as np


D_MODEL = 1024
BATCH = 32
SEQ = 256
DEPTH = 4
DEC_BATCH = 4
DEC_SEQ = 2048
PAST_LEN = 256

GRID_W = 64
N_EVEN = (DEPTH + 1) // 2
N_ODD = DEPTH // 2
RMS_EPS = 1e-6

H_A = 4
DK_A = 128
DV_A = 128
GATE_RANK = 16
GATE_NORM = 16.0
GLA_CHUNK = 64
W_A = H_A * DV_A
H_B = 4
DB = 128
CHUNK_B = 128
W_B = H_B * DB
EVEN_IN = 4 * W_A + GATE_RANK + 2 * W_B
H_C = 4
DC = 128
CONV_W = 3
W_C = H_C * DC
H_D = 4
DD = 128
W_D = H_D * DD
ODD_IN = 3 * W_C + W_D
N_KEYS = 128
N_EXPERTS = N_KEYS * N_KEYS
PEER_HEADS = 8
PEER_TOPK = 16
D_KEY = 256
PEER_BLOCK = 128

kernel_name = 'hybrid_diffusion_gla_gmlp_conv_fnet_peer_step'


def rmsnorm(x, w):
    xf = x.astype(jnp.float32)
    y = xf * lax.rsqrt(jnp.mean(xf * xf, axis=-1, keepdims=True) + RMS_EPS)
    return (y * w.astype(jnp.float32)).astype(x.dtype)


def modulate_norm(x, w, shift, scale):
    return rmsnorm(x, w) * (1.0 + scale) + shift


def grid_pos_embed(rows, dtype):
    t = jnp.arange(rows * GRID_W)
    r = (t // GRID_W).astype(jnp.float32)
    col = (t % GRID_W).astype(jnp.float32)
    nf = D_MODEL // 4
    freqs = 1.0 / (10000.0 ** (jnp.arange(nf, dtype=jnp.float32) / nf))
    def emb(p):
        a = p[:, None] * freqs[None, :]
        return jnp.concatenate([jnp.sin(a), jnp.cos(a)], axis=-1)
    return jnp.concatenate([emb(r), emb(col)], axis=-1).astype(dtype)


def gla_scan(q, k, v, log_a, s0):
    B, L, H, DK = q.shape
    DV = v.shape[-1]
    n = L // GLA_CHUNK
    f32 = jnp.float32
    def chunks(t):
        return t.astype(f32).reshape(B, n, GLA_CHUNK, H, t.shape[-1])
    q, k, v, log_a = chunks(q), chunks(k), chunks(v), chunks(log_a)
    b = jnp.cumsum(log_a, axis=2)
    b_last = b[:, :, -1:]
    q_dec = q * jnp.exp(b)
    k_intra = k * jnp.exp(-b)
    k_state = k * jnp.exp(b_last - b)
    mask = jnp.tril(jnp.ones((GLA_CHUNK, GLA_CHUNK), dtype=bool))
    att = jnp.einsum('bnihd,bnjhd->bnhij', q_dec, k_intra)
    att = jnp.where(mask, att, 0.0)
    o_intra = jnp.einsum('bnhij,bnjhv->bnihv', att, v)
    kv = jnp.einsum('bnjhd,bnjhv->bnhdv', k_state, v)
    decay = jnp.exp(b_last[:, :, 0])
    def step(s, inp):
        q_c, kv_c, dec_c = inp
        o = jnp.einsum('bihd,bhdv->bihv', q_c, s)
        s = dec_c[..., None] * s + kv_c
        return s, o
    s_fin, o_inter = lax.scan(step, s0.astype(f32),
                              (jnp.moveaxis(q_dec, 1, 0), jnp.moveaxis(kv, 1, 0), jnp.moveaxis(decay, 1, 0)))
    o = o_intra + jnp.moveaxis(o_inter, 0, 1)
    return o.reshape(B, L, H, DV), s_fin


def even_mixer(h, w_in, w_gate_f, b_gate_f, w_gate_b, b_gate_b, gla_norm_w,
               sgu_norm_w, sgu_w_s, sgu_b_s, w_out, s0_f, s0_b):
    B, L, _ = h.shape
    z = h @ w_in
    q, k, v, g, lr, ub, vb = jnp.split(
        z, [W_A, 2 * W_A, 3 * W_A, 4 * W_A, 4 * W_A + GATE_RANK, 4 * W_A + GATE_RANK + W_B], axis=-1)
    q = q.reshape(B, L, H_A, DK_A) * (DK_A ** -0.5)
    k = k.reshape(B, L, H_A, DK_A)
    v = v.reshape(B, L, H_A, DV_A)
    lr32 = lr.astype(jnp.float32)
    la_f = (jax.nn.log_sigmoid(lr32 @ w_gate_f.astype(jnp.float32) + b_gate_f.astype(jnp.float32))
            / GATE_NORM).reshape(B, L, H_A, DK_A)
    la_b = (jax.nn.log_sigmoid(lr32 @ w_gate_b.astype(jnp.float32) + b_gate_b.astype(jnp.float32))
            / GATE_NORM).reshape(B, L, H_A, DK_A)
    flip = lambda t: jnp.flip(t, axis=1)
    o_f, s_f = gla_scan(q, k, v, la_f, s0_f)
    o_b, s_b = gla_scan(flip(q), flip(k), flip(v), flip(la_b), s0_b)
    o = rmsnorm(o_f + flip(o_b), gla_norm_w).astype(h.dtype)
    o = o * jax.nn.silu(g.reshape(B, L, H_A, DV_A))
    ub = jax.nn.gelu(ub).reshape(B, L, H_B, DB)
    vb = rmsnorm(jax.nn.gelu(vb).reshape(B, L, H_B, DB), sgu_norm_w)
    vb = vb.reshape(B, L // CHUNK_B, CHUNK_B, H_B, DB)
    sg = jnp.einsum('hij,bnjhd->bnihd', sgu_w_s, vb) + sgu_b_s.T[None, None, :, :, None]
    yb = ub * sg.reshape(B, L, H_B, DB)
    y = jnp.concatenate([o.reshape(B, L, W_A), yb.reshape(B, L, W_B)], axis=-1) @ w_out
    return y, s_f, s_b


def odd_mixer(h, w_in, conv_w, w_out):
    B, L, _ = h.shape
    z = h @ w_in
    bg, cg, xin, f = jnp.split(z, [W_C, 2 * W_C, 3 * W_C], axis=-1)
    t = cg * xin
    tp = jnp.pad(t, ((0, 0), (1, 1), (0, 0)))
    conv = tp[:, :-2] * conv_w[0] + tp[:, 1:-1] * conv_w[1] + tp[:, 2:] * conv_w[2]
    yc = bg * conv
    fd = f.reshape(B, L, H_D, DD).astype(jnp.float32)
    yd = jnp.real(jnp.fft.fft2(fd, axes=(1, 3), norm='ortho')).astype(h.dtype).reshape(B, L, W_D)
    return jnp.concatenate([yc, yd], axis=-1) @ w_out


def peer(h, w_q, sub_keys, u_tab, v_tab):
    B, L, D = h.shape
    x = h.reshape(-1, D)
    nb = x.shape[0] // PEER_BLOCK
    def block(xb):
        q = (xb @ w_q).reshape(PEER_BLOCK, PEER_HEADS, 2, D_KEY // 2).astype(jnp.float32)
        s = jnp.einsum('thpk,pnk->thpn', q, sub_keys.astype(jnp.float32))
        s1, i1 = lax.top_k(s[:, :, 0], PEER_TOPK)
        s2, i2 = lax.top_k(s[:, :, 1], PEER_TOPK)
        cand = (s1[..., :, None] + s2[..., None, :]).reshape(PEER_BLOCK, PEER_HEADS, PEER_TOPK * PEER_TOPK)
        cidx = (i1[..., :, None] * N_KEYS + i2[..., None, :]).reshape(PEER_BLOCK, PEER_HEADS, PEER_TOPK * PEER_TOPK)
        sc, pos = lax.top_k(cand, PEER_TOPK)
        eidx = jnp.take_along_axis(cidx, pos, axis=-1)
        gate = jax.nn.softmax(sc, axis=-1).astype(xb.dtype)
        u = jnp.take(u_tab, eidx, axis=0)
        hid = jax.nn.gelu(jnp.einsum('thkd,td->thk', u, xb))
        vsel = jnp.take(v_tab, eidx, axis=0)
        return jnp.einsum('thk,thkd->td', gate * hid, vsel)
    out = lax.map(block, x.reshape(nb, PEER_BLOCK, D))
    return out.reshape(B, L, D)


def setup_inputs(seed: int = 0) -> dict:
    key = jax.random.key(seed)
    ks = jax.random.split(key, 32)
    nrm = lambda k, shape, s: jax.random.normal(k, shape, jnp.float32) * s
    D = D_MODEL
    return {
        'x_prompt': nrm(ks[0], (BATCH, SEQ, D), 1.0),
        'x_sample': nrm(ks[1], (DEC_BATCH, DEC_SEQ, D), 1.0),
        'state_gla': nrm(ks[2], (DEC_BATCH, N_EVEN, 2, H_A, DK_A, DV_A), 1.0),
        'c': nrm(ks[3], (DEC_BATCH, D), 1.0),
        'c_ctx': nrm(ks[4], (D,), 1.0),
        'w_mod': nrm(ks[5], (DEPTH, D, 6 * D), 0.5 * D ** -0.5),
        'b_mod': nrm(ks[6], (DEPTH, 6 * D), 0.01),
        'norm_w': 1.0 + nrm(ks[7], (DEPTH, 2, D), 0.01),
        'final_norm_w': 1.0 + nrm(ks[8], (D,), 0.01),
        'even_w_in': nrm(ks[9], (N_EVEN, D, EVEN_IN), D ** -0.5),
        'even_w_gate_f': nrm(ks[10], (N_EVEN, GATE_RANK, H_A * DK_A), GATE_RANK ** -0.5),
        'even_b_gate_f': nrm(ks[11], (N_EVEN, H_A * DK_A), 0.1),
        'even_w_gate_b': nrm(ks[12], (N_EVEN, GATE_RANK, H_A * DK_A), GATE_RANK ** -0.5),
        'even_b_gate_b': nrm(ks[13], (N_EVEN, H_A * DK_A), 0.1),
        'gla_norm_w': 1.0 + nrm(ks[14], (N_EVEN, DV_A), 0.01),
        'sgu_norm_w': 1.0 + nrm(ks[15], (N_EVEN, DB), 0.01),
        'sgu_w_s': nrm(ks[16], (N_EVEN, H_B, CHUNK_B, CHUNK_B), CHUNK_B ** -0.5),
        'sgu_b_s': 1.0 + nrm(ks[17], (N_EVEN, H_B, CHUNK_B), 0.01),
        'even_w_out': nrm(ks[18], (N_EVEN, W_A + W_B, D), (W_A + W_B) ** -0.5),
        'odd_w_in': nrm(ks[19], (N_ODD, D, ODD_IN), D ** -0.5),
        'odd_conv_w': nrm(ks[20], (N_ODD, CONV_W, W_C), CONV_W ** -0.5),
        'odd_w_out': nrm(ks[21], (N_ODD, W_C + W_D, D), (W_C + W_D) ** -0.5),
        'peer_w_q': nrm(ks[22], (DEPTH, D, PEER_HEADS * D_KEY), D ** -0.5),
        'peer_sub_keys': nrm(ks[23], (DEPTH, 2, N_KEYS, D_KEY // 2), (D_KEY // 2) ** -0.5),
        'peer_u': nrm(ks[24], (DEPTH, N_EXPERTS, D), D ** -0.5),
        'peer_v': nrm(ks[25], (DEPTH, N_EXPERTS, D), PEER_HEADS ** -0.5),
    }


def reference(x_prompt, x_sample, state_gla, c, c_ctx, w_mod, b_mod, norm_w, final_norm_w,
              even_w_in, even_w_gate_f, even_b_gate_f, even_w_gate_b, even_b_gate_b, gla_norm_w,
              sgu_norm_w, sgu_w_s, sgu_b_s, even_w_out, odd_w_in, odd_conv_w, odd_w_out,
              peer_w_q, peer_sub_keys, peer_u, peer_v):
    rows = x_sample.shape[1] // GRID_W
    b_p = x_prompt.shape[0]
    xp = x_prompt
    xs = x_sample + grid_pos_embed(rows, x_sample.dtype)[None]
    silu_ctx = jax.nn.silu(c_ctx)
    silu_c = jax.nn.silu(c)
    new_states = []
    for l in range(DEPTH):
        mod_p = (silu_ctx @ w_mod[l] + b_mod[l])[None, None, :]
        mod_s = (silu_c @ w_mod[l] + b_mod[l])[:, None, :]
        sh1_p, sc1_p, g1_p, sh2_p, sc2_p, g2_p = jnp.split(mod_p, 6, axis=-1)
        sh1_s, sc1_s, g1_s, sh2_s, sc2_s, g2_s = jnp.split(mod_s, 6, axis=-1)
        hp = modulate_norm(xp, norm_w[l, 0], sh1_p, sc1_p)
        hs = modulate_norm(xs, norm_w[l, 0], sh1_s, sc1_s)
        if l % 2 == 0:
            e = l // 2
            wts = (even_w_in[e], even_w_gate_f[e], even_b_gate_f[e], even_w_gate_b[e], even_b_gate_b[e],
                   gla_norm_w[e], sgu_norm_w[e], sgu_w_s[e], sgu_b_s[e], even_w_out[e])
            zeros = jnp.zeros((b_p, H_A, DK_A, DV_A), jnp.float32)
            yp, sf, sb = even_mixer(hp, *wts, zeros, zeros)
            ys, _, _ = even_mixer(hs, *wts, state_gla[:, e, 0], state_gla[:, e, 1])
            new_states.append(jnp.stack([sf, sb], axis=1).astype(x_prompt.dtype))
        else:
            o = l // 2
            yp = odd_mixer(hp, odd_w_in[o], odd_conv_w[o], odd_w_out[o])
            ys = odd_mixer(hs, odd_w_in[o], odd_conv_w[o], odd_w_out[o])
        xp = xp + g1_p * yp
        xs = xs + g1_s * ys
        hp = modulate_norm(xp, norm_w[l, 1], sh2_p, sc2_p)
        hs = modulate_norm(xs, norm_w[l, 1], sh2_s, sc2_s)
        xp = xp + g2_p * peer(hp, peer_w_q[l], peer_sub_keys[l], peer_u[l], peer_v[l])
        xs = xs + g2_s * peer(hs, peer_w_q[l], peer_sub_keys[l], peer_u[l], peer_v[l])
    y_prompt = rmsnorm(xp, final_norm_w)
    y_sample = rmsnorm(xs, final_norm_w)
    state_gla_new = jnp.stack(new_states, axis=1)
    return (y_prompt, y_sample, state_gla_new)
```

```python
import functools
import math

import jax
import jax.numpy as jnp
import numpy as np
from jax import lax
from jax.experimental import pallas as pl
from jax.experimental.pallas import tpu as pltpu

F32 = jnp.float32
BF16 = jnp.bfloat16

D_MODEL = 1024
DEPTH = 4
GRID_W = 64
RMS_EPS = 1e-6
H_A = 4
DK_A = 128
DV_A = 128
GATE_RANK = 16
GATE_NORM = 16.0
GLA_CHUNK = 64
W_A = H_A * DV_A
H_B = 4
DB = 128
CHUNK_B = 128
W_B = H_B * DB
H_C = 4
DC = 128
W_C = H_C * DC
H_D = 4
DD = 128
W_D = H_D * DD
N_KEYS = 128
N_EXPERTS = N_KEYS * N_KEYS
PEER_HEADS = 8
PEER_TOPK = 16
D_KEY = 256

LANE = 128
MOD_ROWS = 8
EVEN_IN_PAD = 4 * W_A + 2 * W_B + LANE
VMEM_LIMIT = 48 * 1024 * 1024

NN = (((1,), (0,)), ((), ()))
NT = (((1,), (1,)), ((), ()))
TN = (((0,), (0,)), ((), ()))


def _cparams(n_axes):
    return pltpu.CompilerParams(dimension_semantics=("arbitrary",) * n_axes,
                                vmem_limit_bytes=VMEM_LIMIT)


def _dot(a, b, dims=NN):
    return lax.dot_general(a, b, dims, preferred_element_type=F32)


def _split2(x):
    hi = x.astype(BF16)
    lo = (x - hi.astype(F32)).astype(BF16)
    return hi, lo


def _split3(x):
    hi = x.astype(BF16)
    r = x - hi.astype(F32)
    mid = r.astype(BF16)
    lo = (r - mid.astype(F32)).astype(BF16)
    return hi, mid, lo


def _dot3(a, b, dims=NN):
    ah, al = _split2(a)
    bh, bl = _split2(b)
    return _dot(ah, bh, dims) + _dot(al, bh, dims) + _dot(ah, bl, dims)


def _silu(x):
    return x * (1.0 / (1.0 + jnp.exp(-x)))


def _gelu(x):
    c = math.sqrt(2.0 / math.pi)
    return x * (0.5 * (1.0 + jnp.tanh(c * (x + 0.044715 * (x * x * x)))))


def _log_sigmoid(x):
    return jnp.minimum(x, 0.0) - jnp.log(1.0 + jnp.exp(-jnp.abs(x)))


def _rms(x):
    return x * lax.rsqrt(jnp.mean(x * x, axis=-1, keepdims=True) + RMS_EPS)


def _modnorm(x, nw, shift, scale):
    return (_rms(x) * nw) * (1.0 + scale) + shift


def _mod_kernel(cc_ref, w_ref, b_ref, o_ref):
    a = _silu(cc_ref[...])
    o_ref[0] = _dot3(a, w_ref[0]) + b_ref[0]


def _mod_call(cc, w_mod, b_mod):
    depth, d, n = w_mod.shape
    tn = 1024
    return pl.pallas_call(
        _mod_kernel,
        out_shape=jax.ShapeDtypeStruct((depth, MOD_ROWS, n), F32),
        grid=(depth, n // tn),
        in_specs=[
            pl.BlockSpec((MOD_ROWS, d), lambda l, j: (0, 0)),
            pl.BlockSpec((1, d, tn), lambda l, j: (l, 0, j)),
            pl.BlockSpec((1, 1, tn), lambda l, j: (l, 0, j)),
        ],
        out_specs=pl.BlockSpec((1, MOD_ROWS, tn), lambda l, j: (l, 0, j)),
        compiler_params=_cparams(2),
        name="mod_rows",
    )(cc, w_mod, b_mod.reshape(depth, 1, n))


class _Rows:
    def __init__(self, n_prompt_rows, dec_seq):
        self.n_prompt_rows = n_prompt_rows
        self.dec_seq = dec_seq

    def mod_row(self, i, tm):
        npt = self.n_prompt_rows // tm
        tps = self.dec_seq // tm
        return jnp.where(i < npt, 0, 1 + (i - npt) // tps)

    def mod_spec(self, layer, section, tm):
        return pl.BlockSpec((1, 1, 1, D_MODEL),
                            lambda i, *_: (layer, self.mod_row(i, tm), 0, section))


def _normproj_kernel(x_ref, nw_ref, sh_ref, sc_ref, w_ref, o_ref):
    h = _modnorm(x_ref[...], nw_ref[...], sh_ref[0, 0], sc_ref[0, 0])
    o_ref[...] = _dot(h.astype(BF16), w_ref[...])


def _normproj_call(x, nw, mod4, layer, rows, w_bf, tm=256):
    t, d = x.shape
    n = w_bf.shape[1]
    return pl.pallas_call(
        _normproj_kernel,
        out_shape=jax.ShapeDtypeStruct((t, n), F32),
        grid=(t // tm,),
        in_specs=[
            pl.BlockSpec((tm, d), lambda i: (i, 0)),
            pl.BlockSpec((1, d), lambda i: (0, 0)),
            rows.mod_spec(layer, 0, tm),
            rows.mod_spec(layer, 1, tm),
            pl.BlockSpec((d, n), lambda i: (0, 0)),
        ],
        out_specs=pl.BlockSpec((tm, n), lambda i: (i, 0)),
        compiler_params=_cparams(1),
        name="normproj",
    )(x, nw.reshape(1, d), mod4, mod4, w_bf)


def _gla_kernel(*refs, has_s0, want_state):
    (q_ref, k_ref, v_ref, g_ref, lr_ref, wgf_ref, bgf_ref, wgb_ref, bgb_ref, nw_ref) = refs[:10]
    pos = 10
    if has_s0:
        s0f_ref, s0b_ref = refs[pos:pos + 2]
        pos += 2
    o_ref = refs[pos]
    pos += 1
    if want_state:
        sf_ref, sb_ref = refs[pos:pos + 2]
        pos += 2
    laf_scr, lab_scr, of_scr, ob_scr, s_scr = refs[pos:]

    seq = q_ref.shape[0]
    n_chunks = seq // GLA_CHUNK
    c = GLA_CHUNK
    scale = DK_A ** -0.5

    lr = lr_ref[...]
    laf_scr[...] = _log_sigmoid(_dot3(lr, wgf_ref[...]) + bgf_ref[...]) * (1.0 / GATE_NORM)
    lab_scr[...] = _log_sigmoid(_dot3(lr, wgb_ref[...]) + bgb_ref[...]) * (1.0 / GATE_NORM)
    if has_s0:
        s_scr[0] = s0f_ref[0]
        s_scr[1] = s0b_ref[0]
    else:
        s_scr[...] = jnp.zeros(s_scr.shape, F32)

    ri = lax.broadcasted_iota(jnp.int32, (c, c), 0)
    ci = lax.broadcasted_iota(jnp.int32, (c, c), 1)
    lower = ci <= ri
    upper = ci >= ri
    tril = jnp.where(lower, 1.0, 0.0).astype(BF16)
    triu = jnp.where(upper, 1.0, 0.0).astype(BF16)
    ones = jnp.ones((c, DV_A), BF16)

    def chunk(r0, la_scr, d, tri, mask, edge, out_scr):
        rs = pl.ds(r0, c)
        la_h, la_m, la_l = _split3(la_scr[rs, :])
        b = _dot(tri, la_h) + _dot(tri, la_m) + _dot(tri, la_l)
        b_edge = b[edge:edge + 1, :]
        q = q_ref[rs, :] * scale
        k = k_ref[rs, :]
        v = v_ref[rs, :].astype(BF16)
        qd = (q * jnp.exp(b)).astype(BF16)
        ki = (k * jnp.exp(-b)).astype(BF16)
        ks = (k * jnp.exp(b_edge - b)).astype(BF16)
        att = jnp.where(mask, _dot(qd, ki, NT), 0.0)
        s = s_scr[d]
        out_scr[rs, :] = _dot(att.astype(BF16), v) + _dot(qd, s.astype(BF16))
        kv = _dot(ks, v, TN)
        dec = _dot(la_h, ones, TN) + _dot(la_m, ones, TN) + _dot(la_l, ones, TN)
        s_scr[d] = jnp.exp(dec) * s + kv

    def body(i, carry):
        chunk(pl.multiple_of(i * c, c), laf_scr, 0, tril, lower, c - 1, of_scr)
        chunk(pl.multiple_of((n_chunks - 1 - i) * c, c), lab_scr, 1, triu, upper, 0, ob_scr)
        return carry

    lax.fori_loop(0, n_chunks, body, 0)

    o = _rms(of_scr[...] + ob_scr[...]) * nw_ref[...]
    o_ref[...] = (o * _silu(g_ref[...])).astype(o_ref.dtype)
    if want_state:
        sf_ref[0] = s_scr[0]
        sb_ref[0] = s_scr[1]


def _gla_call(z, seq, n_seq, row_block0, wgf, bgf, wgb, bgb, nw, s0=None, want_state=False):
    has_s0 = s0 is not None

    def col(cb):
        return pl.BlockSpec((seq, LANE), lambda s, h: (row_block0 + s, cb + h))

    in_specs = [col(0), col(H_A), col(2 * H_A), col(3 * H_A),
                pl.BlockSpec((seq, LANE), lambda s, h: (row_block0 + s, (4 * W_A + 2 * W_B) // LANE)),
                pl.BlockSpec((LANE, DK_A), lambda s, h: (0, h)),
                pl.BlockSpec((1, DK_A), lambda s, h: (0, h)),
                pl.BlockSpec((LANE, DK_A), lambda s, h: (0, h)),
                pl.BlockSpec((1, DK_A), lambda s, h: (0, h)),
                pl.BlockSpec((1, DV_A), lambda s, h: (0, 0))]
    args = [z, z, z, z, z, wgf, bgf, wgb, bgb, nw]
    if has_s0:
        in_specs += [pl.BlockSpec((1, DK_A, DV_A), lambda s, h: ((s * 2 + 0) * H_A + h, 0, 0)),
                     pl.BlockSpec((1, DK_A, DV_A), lambda s, h: ((s * 2 + 1) * H_A + h, 0, 0))]
        args += [s0, s0]
    out_shape = [jax.ShapeDtypeStruct((n_seq * seq, W_A), BF16)]
    out_specs = [pl.BlockSpec((seq, LANE), lambda s, h: (s, h))]
    if want_state:
        out_shape += [jax.ShapeDtypeStruct((n_seq * H_A, DK_A, DV_A), F32)] * 2
        out_specs += [pl.BlockSpec((1, DK_A, DV_A), lambda s, h: (s * H_A + h, 0, 0))] * 2
    return pl.pallas_call(
        functools.partial(_gla_kernel, has_s0=has_s0, want_state=want_state),
        out_shape=out_shape,
        grid=(n_seq, H_A),
        in_specs=in_specs,
        out_specs=out_specs,
        scratch_shapes=[pltpu.VMEM((seq, DK_A), F32), pltpu.VMEM((seq, DK_A), F32),
                        pltpu.VMEM((seq, DV_A), F32), pltpu.VMEM((seq, DV_A), F32),
                        pltpu.VMEM((2, DK_A, DV_A), F32)],
        compiler_params=_cparams(2),
        name="gla_seq%d" % seq,
    )(*args)


def _sgu_kernel(u_ref, v_ref, nw_ref, ws_ref, bs_ref, o_ref):
    tm = u_ref.shape[0]
    u = _gelu(u_ref[...])
    v = (_rms(_gelu(v_ref[...])) * nw_ref[...]).astype(BF16)
    ws = ws_ref[0]
    bs = bs_ref[0]
    for cidx in range(tm // CHUNK_B):
        rs = slice(cidx * CHUNK_B, (cidx + 1) * CHUNK_B)
        sg = _dot(ws, v[rs, :]) + bs
        o_ref[rs, :] = (u[rs, :] * sg).astype(o_ref.dtype)


def _sgu_call(z, nw, ws_bf, bs_full, tm=512):
    t = z.shape[0]
    ub0 = (4 * W_A) // LANE
    vb0 = (4 * W_A + W_B) // LANE
    return pl.pallas_call(
        _sgu_kernel,
        out_shape=jax.ShapeDtypeStruct((t, W_B), BF16),
        grid=(t // tm, H_B),
        in_specs=[
            pl.BlockSpec((tm, LANE), lambda i, h: (i, ub0 + h)),
            pl.BlockSpec((tm, LANE), lambda i, h: (i, vb0 + h)),
            pl.BlockSpec((1, DB), lambda i, h: (0, 0)),
            pl.BlockSpec((1, CHUNK_B, CHUNK_B), lambda i, h: (h, 0, 0)),
            pl.BlockSpec((1, CHUNK_B, DB), lambda i, h: (h, 0, 0)),
        ],
        out_specs=pl.BlockSpec((tm, LANE), lambda i, h: (i, h)),
        compiler_params=_cparams(2),
        name="sgu",
    )(z, z, nw, ws_bf, bs_full)


def _outproj_kernel(a1_ref, a2_ref, w_ref, x_ref, g_ref, o_ref):
    n1 = a1_ref.shape[1]
    y = _dot(a1_ref[...], w_ref[:n1, :]) + _dot(a2_ref[...], w_ref[n1:, :])
    o_ref[...] = x_ref[...] + g_ref[0, 0] * y


def _outproj_call(a1, a2, w_bf, x, mod4, layer, rows, tm=512):
    t, d = x.shape
    n1, n2 = a1.shape[1], a2.shape[1]
    return pl.pallas_call(
        _outproj_kernel,
        out_shape=jax.ShapeDtypeStruct((t, d), F32),
        grid=(t // tm,),
        in_specs=[
            pl.BlockSpec((tm, n1), lambda i: (i, 0)),
            pl.BlockSpec((tm, n2), lambda i: (i, 0)),
            pl.BlockSpec((n1 + n2, d), lambda i: (0, 0)),
            pl.BlockSpec((tm, d), lambda i: (i, 0)),
            rows.mod_spec(layer, 2, tm),
        ],
        out_specs=pl.BlockSpec((tm, d), lambda i: (i, 0)),
        compiler_params=_cparams(1),
        name="outproj",
    )(a1, a2, w_bf, x, mod4)


def _conv_kernel(bg_ref, cg_ref, xi_ref, w_ref, o_ref):
    seq = bg_ref.shape[0]
    t = cg_ref[...] * xi_ref[...]
    row = lax.broadcasted_iota(jnp.int32, t.shape, 0)
    t_prev = jnp.where(row == 0, 0.0, pltpu.roll(t, 1, 0))
    t_next = jnp.where(row == seq - 1, 0.0, pltpu.roll(t, seq - 1, 0))
    conv = t_prev * w_ref[0:1, :] + t * w_ref[1:2, :] + t_next * w_ref[2:3, :]
    o_ref[...] = (bg_ref[...] * conv).astype(o_ref.dtype)


def _conv_call(z, seq, n_seq, row_block0, conv_w):
    def col(cb):
        return pl.BlockSpec((seq, LANE), lambda s, j: (row_block0 + s, cb + j))
    return pl.pallas_call(
        _conv_kernel,
        out_shape=jax.ShapeDtypeStruct((n_seq * seq, W_C), BF16),
        grid=(n_seq, W_C // LANE),
        in_specs=[col(0), col(H_C), col(2 * H_C),
                  pl.BlockSpec((3, LANE), lambda s, j: (0, j))],
        out_specs=pl.BlockSpec((seq, LANE), lambda s, j: (s, j)),
        compiler_params=_cparams(2),
        name="conv_seq%d" % seq,
    )(z, z, z, conv_w)


def _fft_kernel(f_ref, cd_ref, sd_ref, cl_ref, sl_ref, o_ref, p_scr, q_scr, *, scale):
    @pl.when(pl.program_id(1) == 0)
    def _():
        cd = cd_ref[...]
        sd = sd_ref[...]
        for h in range(H_D):
            cs = slice(h * DD, (h + 1) * DD)
            fh = f_ref[:, cs].astype(BF16)
            p_scr[:, cs] = _dot(fh, cd).astype(BF16)
            q_scr[:, cs] = _dot(fh, sd).astype(BF16)

    y = _dot(cl_ref[...], p_scr[...]) - _dot(sl_ref[...], q_scr[...])
    o_ref[...] = (y * scale).astype(o_ref.dtype)


def _dft_mats(n):
    idx = jnp.arange(n, dtype=jnp.int32)
    ang = ((idx[:, None] * idx[None, :]) % n).astype(F32) * (2.0 * math.pi / n)
    return jnp.cos(ang).astype(BF16), jnp.sin(ang).astype(BF16)


def _fft_call(z, seq, n_seq, row_block0, cd, sd, cl, sl):
    tl = min(seq, 512)
    fcol = (3 * W_C) // W_D
    return pl.pallas_call(
        functools.partial(_fft_kernel, scale=1.0 / math.sqrt(seq * DD)),
        out_shape=jax.ShapeDtypeStruct((n_seq * seq, W_D), BF16),
        grid=(n_seq, seq // tl),
        in_specs=[
            pl.BlockSpec((seq, W_D), lambda s, i: (row_block0 + s, fcol)),
            pl.BlockSpec((DD, DD), lambda s, i: (0, 0)),
            pl.BlockSpec((DD, DD), lambda s, i: (0, 0)),
            pl.BlockSpec((tl, seq), lambda s, i: (i, 0)),
            pl.BlockSpec((tl, seq), lambda s, i: (i, 0)),
        ],
        out_specs=pl.BlockSpec((tl, W_D), lambda s, i: (s * (seq // tl) + i, 0)),
        scratch_shapes=[pltpu.VMEM((seq, W_D), BF16), pltpu.VMEM((seq, W_D), BF16)],
        compiler_params=_cparams(2),
        name="fft_seq%d" % seq,
    )(z, cd, sd, cl, sl)


def _topk16(s):
    nrows = s.shape[0]
    iota = lax.broadcasted_iota(jnp.int32, s.shape, 0)
    i16 = lax.broadcasted_iota(jnp.int32, (PEER_TOPK, s.shape[1]), 0)

    def body(k, carry):
        s, rank, top = carry
        m = jnp.max(s, axis=0, keepdims=True)
        idx = jnp.min(jnp.where(s == m, iota, nrows), axis=0, keepdims=True)
        sel = iota == idx
        rank = jnp.where(sel, k, rank)
        s = jnp.where(sel, -jnp.inf, s)
        top = jnp.where(i16 == k, m, top)
        return s, rank, top

    init = (s, jnp.full(s.shape, PEER_TOPK, jnp.int32), jnp.zeros((PEER_TOPK, s.shape[1]), F32))
    _, rank, top = lax.fori_loop(0, PEER_TOPK, body, init)
    return rank, top


_CAND_GROUPS = [(0, 16), (1, 8)] + [(k1, 8) for k1 in range(2, 8)]
_CAND_SINGLE0 = sum(n for _, n in _CAND_GROUPS)


def _psel_kernel(x_ref, nw_ref, sh_ref, sc_ref, wq_ref, sk_ref,
                 h_ref, r2_ref, e2_ref, cut_ref, e1_ref, q_scr):
    tq = x_ref.shape[0]
    hb = _modnorm(x_ref[...], nw_ref[...], sh_ref[0, 0], sc_ref[0, 0]).astype(BF16)
    h_ref[...] = hb
    q_scr[...] = _dot(wq_ref[...], hb, NT)
    half = D_KEY // 2
    neg = jnp.full((8, tq), -jnp.inf, F32)
    row8 = lax.broadcasted_iota(jnp.int32, (8, tq), 0)

    def head(hh, carry):
        q1 = q_scr[pl.ds(pl.multiple_of(hh * D_KEY, D_KEY), half), :]
        q2 = q_scr[pl.ds(pl.multiple_of(hh * D_KEY + half, half), half), :]
        s1 = _dot3(sk_ref[0], q1)
        s2 = _dot3(sk_ref[1], q2)
        r1, t1 = _topk16(s1)
        r2, t2 = _topk16(s2)
        pieces = []
        for k1, nrow in _CAND_GROUPS:
            blk = t1[k1:k1 + 1, :] + t2[0:nrow, :]
            nvalid = PEER_TOPK // (k1 + 1)
            if nvalid < nrow:
                blk = jnp.where(row8 < nvalid, blk, neg)
            pieces.append(blk)
        pieces.append(t1[8:16, :] + t2[0:1, :])
        cand = jnp.concatenate(pieces, axis=0)
        crank, _ = _topk16(cand)
        sel = crank < PEER_TOPK
        cmax = t1[0:1, :] + t2[0:1, :]
        z = jnp.sum(jnp.where(sel, jnp.exp(cand - cmax), 0.0), axis=0, keepdims=True)
        self32 = jnp.where(sel, 1.0, 0.0)
        cut = jnp.zeros((N_KEYS, tq), F32)
        r0 = 0
        for k1, nrow in _CAND_GROUPS:
            cnt = jnp.sum(self32[r0:r0 + nrow, :], axis=0, keepdims=True)
            cut = jnp.where(r1 == k1, cnt, cut)
            r0 += nrow
        for j in range(8):
            cut = jnp.where(r1 == 8 + j, self32[r0 + j:r0 + j + 1, :], cut)
        r2_ref[hh] = r2.astype(F32)
        cut_ref[hh] = cut
        e1_ref[hh] = jnp.exp(s1 - t1[0:1, :]) * (1.0 / z)
        e2_ref[hh] = jnp.exp(s2 - t2[0:1, :])
        return carry

    lax.fori_loop(0, PEER_HEADS, head, 0)


def _psel_call(x, nw, mod4, layer, rows, wqt_bf, sub_keys, tq=256):
    t, d = x.shape
    nq = wqt_bf.shape[0]
    sel_shape = jax.ShapeDtypeStruct((PEER_HEADS, N_KEYS, t), F32)
    sel_spec = pl.BlockSpec((PEER_HEADS, N_KEYS, tq), lambda i: (0, 0, i))
    return pl.pallas_call(
        _psel_kernel,
        out_shape=[jax.ShapeDtypeStruct((t, d), BF16), sel_shape, sel_shape, sel_shape, sel_shape],
        grid=(t // tq,),
        in_specs=[
            pl.BlockSpec((tq, d), lambda i: (i, 0)),
            pl.BlockSpec((1, d), lambda i: (0, 0)),
            rows.mod_spec(layer, 3, tq),
            rows.mod_spec(layer, 4, tq),
            pl.BlockSpec((nq, d), lambda i: (0, 0)),
            pl.BlockSpec((2, N_KEYS, D_KEY // 2), lambda i: (0, 0, 0)),
        ],
        out_specs=[pl.BlockSpec((tq, d), lambda i: (i, 0)), sel_spec, sel_spec, sel_spec, sel_spec],
        scratch_shapes=[pltpu.VMEM((nq, tq), F32)],
        compiler_params=_cparams(1),
        name="peer_select",
    )(x, nw.reshape(1, d), mod4, mod4, wqt_bf, sub_keys)


def _pdense_kernel(h_ref, u_ref, v_ref, r2_ref, e2_ref, cut_ref, e1_ref, x_ref, g_ref, o_ref,
                   hid_scr, w_scr, acc_scr):
    j = pl.program_id(1)
    te = u_ref.shape[0]
    tq = h_ref.shape[0]
    na = te // N_KEYS

    @pl.when(j == 0)
    def _():
        acc_scr[...] = jnp.zeros(acc_scr.shape, F32)

    hid_scr[...] = _dot(u_ref[...], h_ref[...], NT)

    def abody(a, carry):
        ga = j * na + a
        rs = pl.ds(pl.multiple_of(a * N_KEYS, N_KEYS), N_KEYS)
        cut_rows = [cut_ref[hh, pl.ds(ga, 1), :] for hh in range(PEER_HEADS)]
        e1_rows = [e1_ref[hh, pl.ds(ga, 1), :] for hh in range(PEER_HEADS)]
        for lc in range(tq // LANE):
            ls = slice(lc * LANE, (lc + 1) * LANE)
            gate = jnp.zeros((N_KEYS, LANE), F32)
            for hh in range(PEER_HEADS):
                gate = gate + jnp.where(r2_ref[hh, :, ls] < cut_rows[hh][:, ls],
                                        e2_ref[hh, :, ls] * e1_rows[hh][:, ls], 0.0)
            w_scr[rs, ls] = (_gelu(hid_scr[rs, ls]) * gate).astype(BF16)
        return carry

    lax.fori_loop(0, na, abody, 0)
    acc_scr[...] += _dot(w_scr[...], v_ref[...], TN)

    @pl.when(j == pl.num_programs(1) - 1)
    def _():
        o_ref[...] = x_ref[...] + g_ref[0, 0] * acc_scr[...]


def _pdense_call(h_bf, u_bf, v_bf, r2, e2, cut, e1, x, mod4, layer, rows, tq=512, te=1024):
    t, d = x.shape
    ne = u_bf.shape[0]
    sel_spec = pl.BlockSpec((PEER_HEADS, N_KEYS, tq), lambda i, j: (0, 0, i))
    return pl.pallas_call(
        _pdense_kernel,
        out_shape=jax.ShapeDtypeStruct((t, d), F32),
        grid=(t // tq, ne // te),
        in_specs=[
            pl.BlockSpec((tq, d), lambda i, j: (i, 0)),
            pl.BlockSpec((te, d), lambda i, j: (j, 0)),
            pl.BlockSpec((te, d), lambda i, j: (j, 0)),
            sel_spec, sel_spec, sel_spec, sel_spec,
            pl.BlockSpec((tq, d), lambda i, j: (i, 0)),
            rows.mod_spec(layer, 5, tq),
        ],
        out_specs=pl.BlockSpec((tq, d), lambda i, j: (i, 0)),
        scratch_shapes=[pltpu.VMEM((te, tq), F32), pltpu.VMEM((te, tq), BF16), pltpu.VMEM((tq, d), F32)],
        compiler_params=_cparams(2),
        name="peer_dense",
    )(h_bf, u_bf, v_bf, r2, e2, cut, e1, x, mod4)


def _final_kernel(x_ref, w_ref, o_ref):
    o_ref[...] = _rms(x_ref[...]) * w_ref[...]


def _final_call(x, w, row_block0, n_rows, tm=512):
    d = x.shape[1]
    return pl.pallas_call(
        _final_kernel,
        out_shape=jax.ShapeDtypeStruct((n_rows, d), F32),
        grid=(n_rows // tm,),
        in_specs=[pl.BlockSpec((tm, d), lambda i: (row_block0 + i, 0)),
                  pl.BlockSpec((1, d), lambda i: (0, 0))],
        out_specs=pl.BlockSpec((tm, d), lambda i: (i, 0)),
        compiler_params=_cparams(1),
        name="final_norm",
    )(x, w.reshape(1, d))


def _grid_pos_embed(rows, dtype):
    t = jnp.arange(rows * GRID_W)
    r = (t // GRID_W).astype(F32)
    col = (t % GRID_W).astype(F32)
    nf = D_MODEL // 4
    freqs = 1.0 / (10000.0 ** (jnp.arange(nf, dtype=F32) / nf))

    def emb(p):
        a = p[:, None] * freqs[None, :]
        return jnp.concatenate([jnp.sin(a), jnp.cos(a)], axis=-1)
    return jnp.concatenate([emb(r), emb(col)], axis=-1).astype(dtype)


def _even_w_in_layout(w):
    d = w.shape[0]
    a_end = 4 * W_A
    lr = w[:, a_end:a_end + GATE_RANK]
    uv = w[:, a_end + GATE_RANK:]
    pad = jnp.zeros((d, LANE - GATE_RANK), w.dtype)
    return jnp.concatenate([w[:, :a_end], uv, lr, pad], axis=1)


def _pad_gate_w(w):
    return jnp.concatenate([w, jnp.zeros((LANE - GATE_RANK, w.shape[1]), w.dtype)], axis=0)


def kernel(x_prompt, x_sample, state_gla, c, c_ctx, w_mod, b_mod, norm_w, final_norm_w, even_w_in, even_w_gate_f, even_b_gate_f, even_w_gate_b, even_b_gate_b, gla_norm_w, sgu_norm_w, sgu_w_s, sgu_b_s, even_w_out, odd_w_in, odd_conv_w, odd_w_out, peer_w_q, peer_sub_keys, peer_u, peer_v):
    bp, lp, d = x_prompt.shape
    bs, ls, _ = x_sample.shape
    depth = w_mod.shape[0]
    n_even = even_w_in.shape[0]
    tp = bp * lp
    ts = bs * ls
    rows = _Rows(tp, ls)

    xs = x_sample + _grid_pos_embed(ls // GRID_W, x_sample.dtype)[None]
    x = jnp.concatenate([x_prompt.reshape(tp, d), xs.reshape(ts, d)], axis=0)

    cc = jnp.concatenate([c_ctx[None], c, jnp.zeros((MOD_ROWS - 1 - bs, d), F32)], axis=0)
    mod4 = _mod_call(cc, w_mod, b_mod).reshape(depth, MOD_ROWS, 1, 6 * d)

    cd, sd = _dft_mats(DD)
    clp, slp = _dft_mats(lp)
    cls, sls = _dft_mats(ls)
    s0_all = state_gla.reshape(bs, n_even, 2 * H_A, DK_A, DV_A)

    new_states = []
    for l in range(depth):
        if l % 2 == 0:
            e = l // 2
            z = _normproj_call(x, norm_w[l, 0], mod4, l, rows, _even_w_in_layout(even_w_in[e]).astype(BF16))
            gate_args = (_pad_gate_w(even_w_gate_f[e]), even_b_gate_f[e][None],
                         _pad_gate_w(even_w_gate_b[e]), even_b_gate_b[e][None], gla_norm_w[e][None])
            o_p, sf, sb = _gla_call(z, lp, bp, 0, *gate_args, want_state=True)
            (o_s,) = _gla_call(z, ls, bs, tp // ls, *gate_args,
                               s0=s0_all[:, e].reshape(bs * 2 * H_A, DK_A, DV_A))
            a1 = jnp.concatenate([o_p, o_s], axis=0)
            bs_full = jnp.broadcast_to(sgu_b_s[e][:, :, None], (H_B, CHUNK_B, DB))
            a2 = _sgu_call(z, sgu_norm_w[e][None], sgu_w_s[e].astype(BF16), bs_full)
            x = _outproj_call(a1, a2, even_w_out[e].astype(BF16), x, mod4, l, rows)
            sfb = jnp.stack([sf.reshape(bp, H_A, DK_A, DV_A), sb.reshape(bp, H_A, DK_A, DV_A)], axis=1)
            new_states.append(sfb.astype(x_prompt.dtype))
        else:
            o = l // 2
            z = _normproj_call(x, norm_w[l, 0], mod4, l, rows, odd_w_in[o].astype(BF16))
            yc = jnp.concatenate([_conv_call(z, lp, bp, 0, odd_conv_w[o]),
                                  _conv_call(z, ls, bs, tp // ls, odd_conv_w[o])], axis=0)
            yd = jnp.concatenate([_fft_call(z, lp, bp, 0, cd, sd, clp, slp),
                                  _fft_call(z, ls, bs, tp // ls, cd, sd, cls, sls)], axis=0)
            x = _outproj_call(yc, yd, odd_w_out[o].astype(BF16), x, mod4, l, rows)
        h_bf, r2, e2, cut, e1 = _psel_call(x, norm_w[l, 1], mod4, l, rows,
                                           peer_w_q[l].T.astype(BF16), peer_sub_keys[l])
        x = _pdense_call(h_bf, peer_u[l].astype(BF16), peer_v[l].astype(BF16),
                         r2, e2, cut, e1, x, mod4, l, rows)

    y_prompt = _final_call(x, final_norm_w, 0, tp).reshape(bp, lp, d)
    y_sample = _final_call(x, final_norm_w, tp // 512, ts).reshape(bs, ls, d)
    state_new = jnp.stack(new_states, axis=1)
    return (y_prompt, y_sample, state_new)
```

```python
import functools
import math

import jax
import jax.numpy as jnp
import numpy as np
from jax import lax
from jax.experimental import pallas as pl
from jax.experimental.pallas import tpu as pltpu

F32 = jnp.float32
BF16 = jnp.bfloat16

D_MODEL = 1024
DEPTH = 4
GRID_W = 64
RMS_EPS = 1e-6
H_A = 4
DK_A = 128
DV_A = 128
GATE_RANK = 16
GATE_NORM = 16.0
GLA_CHUNK = 64
W_A = H_A * DV_A
H_B = 4
DB = 128
CHUNK_B = 128
W_B = H_B * DB
H_C = 4
DC = 128
W_C = H_C * DC
H_D = 4
DD = 128
W_D = H_D * DD
N_KEYS = 128
N_EXPERTS = N_KEYS * N_KEYS
PEER_HEADS = 8
PEER_TOPK = 16
D_KEY = 256

LANE = 128
MOD_ROWS = 8
EVEN_IN_PAD = 4 * W_A + 2 * W_B + LANE
VMEM_LIMIT = 48 * 1024 * 1024

NN = (((1,), (0,)), ((), ()))
NT = (((1,), (1,)), ((), ()))
TN = (((0,), (0,)), ((), ()))


def _cparams(n_axes, flags=None):
    return pltpu.CompilerParams(dimension_semantics=("arbitrary",) * n_axes,
                                vmem_limit_bytes=VMEM_LIMIT, flags=flags)


def _dot(a, b, dims=NN):
    return lax.dot_general(a, b, dims, preferred_element_type=F32)


def _split2(x):
    hi = x.astype(BF16)
    lo = (x - hi.astype(F32)).astype(BF16)
    return hi, lo


def _split3(x):
    hi = x.astype(BF16)
    r = x - hi.astype(F32)
    mid = r.astype(BF16)
    lo = (r - mid.astype(F32)).astype(BF16)
    return hi, mid, lo


def _dot3(a, b, dims=NN):
    ah, al = _split2(a)
    bh, bl = _split2(b)
    return _dot(ah, bh, dims) + _dot(al, bh, dims) + _dot(ah, bl, dims)


def _silu(x):
    return x * (1.0 / (1.0 + jnp.exp(-x)))


def _gelu(x):
    c = math.sqrt(2.0 / math.pi)
    return x * (0.5 * (1.0 + jnp.tanh(c * (x + 0.044715 * (x * x * x)))))


def _log_sigmoid(x):
    return jnp.minimum(x, 0.0) - jnp.log(1.0 + jnp.exp(-jnp.abs(x)))


def _rms(x):
    return x * lax.rsqrt(jnp.mean(x * x, axis=-1, keepdims=True) + RMS_EPS)


def _modnorm(x, nw, shift, scale):
    return (_rms(x) * nw) * (1.0 + scale) + shift


def _mod_kernel(cc_ref, w_ref, b_ref, o_ref):
    a = _silu(cc_ref[...])
    o_ref[0] = _dot3(a, w_ref[0]) + b_ref[0]


def _mod_call(cc, w_mod, b_mod):
    depth, d, n = w_mod.shape
    tn = 1024
    return pl.pallas_call(
        _mod_kernel,
        out_shape=jax.ShapeDtypeStruct((depth, MOD_ROWS, n), F32),
        grid=(depth, n // tn),
        in_specs=[
            pl.BlockSpec((MOD_ROWS, d), lambda l, j: (0, 0)),
            pl.BlockSpec((1, d, tn), lambda l, j: (l, 0, j)),
            pl.BlockSpec((1, 1, tn), lambda l, j: (l, 0, j)),
        ],
        out_specs=pl.BlockSpec((1, MOD_ROWS, tn), lambda l, j: (l, 0, j)),
        compiler_params=_cparams(2),
        name="mod_rows",
    )(cc, w_mod, b_mod.reshape(depth, 1, n))


class _Rows:
    def __init__(self, n_prompt_rows, dec_seq):
        self.n_prompt_rows = n_prompt_rows
        self.dec_seq = dec_seq

    def mod_row(self, i, tm):
        npt = self.n_prompt_rows // tm
        tps = self.dec_seq // tm
        return jnp.where(i < npt, 0, 1 + (i - npt) // tps)

    def mod_spec(self, layer, section, tm, tile_of=lambda i, *_: i):
        return pl.BlockSpec((1, 1, 1, D_MODEL),
                            lambda *g: (layer, self.mod_row(tile_of(*g), tm), 0, section))


def _normproj_kernel(x_ref, nw_ref, sh_ref, sc_ref, w_ref, o_ref):
    h = _modnorm(x_ref[...], nw_ref[...], sh_ref[0, 0], sc_ref[0, 0])
    o_ref[...] = _dot(h.astype(BF16), w_ref[...])


def _normproj_call(x, nw, mod4, layer, rows, w_bf, tm=256):
    t, d = x.shape
    n = w_bf.shape[1]
    return pl.pallas_call(
        _normproj_kernel,
        out_shape=jax.ShapeDtypeStruct((t, n), F32),
        grid=(t // tm,),
        in_specs=[
            pl.BlockSpec((tm, d), lambda i: (i, 0)),
            pl.BlockSpec((1, d), lambda i: (0, 0)),
            rows.mod_spec(layer, 0, tm),
            rows.mod_spec(layer, 1, tm),
            pl.BlockSpec((d, n), lambda i: (0, 0)),
        ],
        out_specs=pl.BlockSpec((tm, n), lambda i: (i, 0)),
        compiler_params=_cparams(1),
        name="normproj",
    )(x, nw.reshape(1, d), mod4, mod4, w_bf)


def _gla_kernel(*refs, has_s0, want_state):
    (q_ref, k_ref, v_ref, g_ref, lr_ref, wgf_ref, bgf_ref, wgb_ref, bgb_ref, nw_ref) = refs[:10]
    pos = 10
    if has_s0:
        s0f_ref, s0b_ref = refs[pos:pos + 2]
        pos += 2
    o_ref = refs[pos]
    pos += 1
    if want_state:
        sf_ref, sb_ref = refs[pos:pos + 2]
        pos += 2
    laf_scr, lab_scr, of_scr, ob_scr, s_scr = refs[pos:]

    seq = q_ref.shape[0]
    n_chunks = seq // GLA_CHUNK
    c = GLA_CHUNK
    scale = DK_A ** -0.5

    lr = lr_ref[...]
    laf_scr[...] = _log_sigmoid(_dot3(lr, wgf_ref[...]) + bgf_ref[...]) * (1.0 / GATE_NORM)
    lab_scr[...] = _log_sigmoid(_dot3(lr, wgb_ref[...]) + bgb_ref[...]) * (1.0 / GATE_NORM)
    if has_s0:
        s_scr[0] = s0f_ref[0]
        s_scr[1] = s0b_ref[0]
    else:
        s_scr[...] = jnp.zeros(s_scr.shape, F32)

    ri = lax.broadcasted_iota(jnp.int32, (c, c), 0)
    ci = lax.broadcasted_iota(jnp.int32, (c, c), 1)
    lower = ci <= ri
    upper = ci >= ri
    tril = jnp.where(lower, 1.0, 0.0).astype(BF16)
    triu = jnp.where(upper, 1.0, 0.0).astype(BF16)
    ones = jnp.ones((c, DV_A), BF16)

    def chunk(r0, la_scr, d, tri, mask, edge, out_scr):
        rs = pl.ds(r0, c)
        la_h, la_m, la_l = _split3(la_scr[rs, :])
        b = _dot(tri, la_h) + _dot(tri, la_m) + _dot(tri, la_l)
        b_edge = b[edge:edge + 1, :]
        q = q_ref[rs, :] * scale
        k = k_ref[rs, :]
        v = v_ref[rs, :].astype(BF16)
        qd = (q * jnp.exp(b)).astype(BF16)
        ki = (k * jnp.exp(-b)).astype(BF16)
        ks = (k * jnp.exp(b_edge - b)).astype(BF16)
        att = jnp.where(mask, _dot(qd, ki, NT), 0.0)
        s = s_scr[d]
        out_scr[rs, :] = _dot(att.astype(BF16), v) + _dot(qd, s.astype(BF16))
        kv = _dot(ks, v, TN)
        dec = _dot(la_h, ones, TN) + _dot(la_m, ones, TN) + _dot(la_l, ones, TN)
        s_scr[d] = jnp.exp(dec) * s + kv

    def body(i, carry):
        chunk(pl.multiple_of(i * c, c), laf_scr, 0, tril, lower, c - 1, of_scr)
        chunk(pl.multiple_of((n_chunks - 1 - i) * c, c), lab_scr, 1, triu, upper, 0, ob_scr)
        return carry

    lax.fori_loop(0, n_chunks, body, 0)

    o = _rms(of_scr[...] + ob_scr[...]) * nw_ref[...]
    o_ref[...] = (o * _silu(g_ref[...])).astype(o_ref.dtype)
    if want_state:
        sf_ref[0] = s_scr[0]
        sb_ref[0] = s_scr[1]


def _gla_call(z, seq, n_seq, row_block0, wgf, bgf, wgb, bgb, nw, s0=None, want_state=False):
    has_s0 = s0 is not None

    def col(cb):
        return pl.BlockSpec((seq, LANE), lambda s, h: (row_block0 + s, cb + h))

    in_specs = [col(0), col(H_A), col(2 * H_A), col(3 * H_A),
                pl.BlockSpec((seq, LANE), lambda s, h: (row_block0 + s, (4 * W_A + 2 * W_B) // LANE)),
                pl.BlockSpec((LANE, DK_A), lambda s, h: (0, h)),
                pl.BlockSpec((1, DK_A), lambda s, h: (0, h)),
                pl.BlockSpec((LANE, DK_A), lambda s, h: (0, h)),
                pl.BlockSpec((1, DK_A), lambda s, h: (0, h)),
                pl.BlockSpec((1, DV_A), lambda s, h: (0, 0))]
    args = [z, z, z, z, z, wgf, bgf, wgb, bgb, nw]
    if has_s0:
        in_specs += [pl.BlockSpec((1, DK_A, DV_A), lambda s, h: ((s * 2 + 0) * H_A + h, 0, 0)),
                     pl.BlockSpec((1, DK_A, DV_A), lambda s, h: ((s * 2 + 1) * H_A + h, 0, 0))]
        args += [s0, s0]
    out_shape = [jax.ShapeDtypeStruct((n_seq * seq, W_A), BF16)]
    out_specs = [pl.BlockSpec((seq, LANE), lambda s, h: (s, h))]
    if want_state:
        out_shape += [jax.ShapeDtypeStruct((n_seq * H_A, DK_A, DV_A), F32)] * 2
        out_specs += [pl.BlockSpec((1, DK_A, DV_A), lambda s, h: (s * H_A + h, 0, 0))] * 2
    return pl.pallas_call(
        functools.partial(_gla_kernel, has_s0=has_s0, want_state=want_state),
        out_shape=out_shape,
        grid=(n_seq, H_A),
        in_specs=in_specs,
        out_specs=out_specs,
        scratch_shapes=[pltpu.VMEM((seq, DK_A), F32), pltpu.VMEM((seq, DK_A), F32),
                        pltpu.VMEM((seq, DV_A), F32), pltpu.VMEM((seq, DV_A), F32),
                        pltpu.VMEM((2, DK_A, DV_A), F32)],
        compiler_params=_cparams(2),
        name="gla_seq%d" % seq,
    )(*args)


def _sgu_kernel(u_ref, v_ref, nw_ref, ws_ref, bs_ref, o_ref):
    tm = u_ref.shape[0]
    u = _gelu(u_ref[...])
    v = (_rms(_gelu(v_ref[...])) * nw_ref[...]).astype(BF16)
    ws = ws_ref[0]
    bs = bs_ref[0]
    for cidx in range(tm // CHUNK_B):
        rs = slice(cidx * CHUNK_B, (cidx + 1) * CHUNK_B)
        sg = _dot(ws, v[rs, :]) + bs
        o_ref[rs, :] = (u[rs, :] * sg).astype(o_ref.dtype)


def _sgu_call(z, nw, ws_bf, bs_full, tm=512):
    t = z.shape[0]
    ub0 = (4 * W_A) // LANE
    vb0 = (4 * W_A + W_B) // LANE
    return pl.pallas_call(
        _sgu_kernel,
        out_shape=jax.ShapeDtypeStruct((t, W_B), BF16),
        grid=(t // tm, H_B),
        in_specs=[
            pl.BlockSpec((tm, LANE), lambda i, h: (i, ub0 + h)),
            pl.BlockSpec((tm, LANE), lambda i, h: (i, vb0 + h)),
            pl.BlockSpec((1, DB), lambda i, h: (0, 0)),
            pl.BlockSpec((1, CHUNK_B, CHUNK_B), lambda i, h: (h, 0, 0)),
            pl.BlockSpec((1, CHUNK_B, DB), lambda i, h: (h, 0, 0)),
        ],
        out_specs=pl.BlockSpec((tm, LANE), lambda i, h: (i, h)),
        compiler_params=_cparams(2),
        name="sgu",
    )(z, z, nw, ws_bf, bs_full)


def _outproj_kernel(a1_ref, a2_ref, w_ref, x_ref, g_ref, o_ref):
    n1 = a1_ref.shape[1]
    y = _dot(a1_ref[...], w_ref[:n1, :]) + _dot(a2_ref[...], w_ref[n1:, :])
    o_ref[...] = x_ref[...] + g_ref[0, 0] * y


def _outproj_call(a1, a2, w_bf, x, mod4, layer, rows, tm=512):
    t, d = x.shape
    n1, n2 = a1.shape[1], a2.shape[1]
    return pl.pallas_call(
        _outproj_kernel,
        out_shape=jax.ShapeDtypeStruct((t, d), F32),
        grid=(t // tm,),
        in_specs=[
            pl.BlockSpec((tm, n1), lambda i: (i, 0)),
            pl.BlockSpec((tm, n2), lambda i: (i, 0)),
            pl.BlockSpec((n1 + n2, d), lambda i: (0, 0)),
            pl.BlockSpec((tm, d), lambda i: (i, 0)),
            rows.mod_spec(layer, 2, tm),
        ],
        out_specs=pl.BlockSpec((tm, d), lambda i: (i, 0)),
        compiler_params=_cparams(1),
        name="outproj",
    )(a1, a2, w_bf, x, mod4)


def _conv_kernel(bg_ref, cg_ref, xi_ref, w_ref, o_ref):
    seq = bg_ref.shape[0]
    t = cg_ref[...] * xi_ref[...]
    row = lax.broadcasted_iota(jnp.int32, t.shape, 0)
    t_prev = jnp.where(row == 0, 0.0, pltpu.roll(t, 1, 0))
    t_next = jnp.where(row == seq - 1, 0.0, pltpu.roll(t, seq - 1, 0))
    conv = t_prev * w_ref[0:1, :] + t * w_ref[1:2, :] + t_next * w_ref[2:3, :]
    o_ref[...] = (bg_ref[...] * conv).astype(o_ref.dtype)


def _conv_call(z, seq, n_seq, row_block0, conv_w):
    def col(cb):
        return pl.BlockSpec((seq, LANE), lambda s, j: (row_block0 + s, cb + j))
    return pl.pallas_call(
        _conv_kernel,
        out_shape=jax.ShapeDtypeStruct((n_seq * seq, W_C), BF16),
        grid=(n_seq, W_C // LANE),
        in_specs=[col(0), col(H_C), col(2 * H_C),
                  pl.BlockSpec((3, LANE), lambda s, j: (0, j))],
        out_specs=pl.BlockSpec((seq, LANE), lambda s, j: (s, j)),
        compiler_params=_cparams(2),
        name="conv_seq%d" % seq,
    )(z, z, z, conv_w)


def _fft_kernel(f_ref, cd_ref, sd_ref, cl_ref, sl_ref, o_ref, p_scr, q_scr, *, scale):
    @pl.when(pl.program_id(1) == 0)
    def _():
        cd = cd_ref[...]
        sd = sd_ref[...]
        for h in range(H_D):
            cs = slice(h * DD, (h + 1) * DD)
            fh = f_ref[:, cs].astype(BF16)
            p_scr[:, cs] = _dot(fh, cd).astype(BF16)
            q_scr[:, cs] = _dot(fh, sd).astype(BF16)

    y = _dot(cl_ref[...], p_scr[...]) - _dot(sl_ref[...], q_scr[...])
    o_ref[...] = (y * scale).astype(o_ref.dtype)


def _dft_mats(n):
    idx = jnp.arange(n, dtype=jnp.int32)
    ang = ((idx[:, None] * idx[None, :]) % n).astype(F32) * (2.0 * math.pi / n)
    return jnp.cos(ang).astype(BF16), jnp.sin(ang).astype(BF16)


def _fft_call(z, seq, n_seq, row_block0, cd, sd, cl, sl):
    tl = min(seq, 512)
    fcol = (3 * W_C) // W_D
    return pl.pallas_call(
        functools.partial(_fft_kernel, scale=1.0 / math.sqrt(seq * DD)),
        out_shape=jax.ShapeDtypeStruct((n_seq * seq, W_D), BF16),
        grid=(n_seq, seq // tl),
        in_specs=[
            pl.BlockSpec((seq, W_D), lambda s, i: (row_block0 + s, fcol)),
            pl.BlockSpec((DD, DD), lambda s, i: (0, 0)),
            pl.BlockSpec((DD, DD), lambda s, i: (0, 0)),
            pl.BlockSpec((tl, seq), lambda s, i: (i, 0)),
            pl.BlockSpec((tl, seq), lambda s, i: (i, 0)),
        ],
        out_specs=pl.BlockSpec((tl, W_D), lambda s, i: (s * (seq // tl) + i, 0)),
        scratch_shapes=[pltpu.VMEM((seq, W_D), BF16), pltpu.VMEM((seq, W_D), BF16)],
        compiler_params=_cparams(2),
        name="fft_seq%d" % seq,
    )(z, cd, sd, cl, sl)


def _topk16(s):
    nrows = s.shape[0]
    iota = lax.broadcasted_iota(jnp.int32, s.shape, 0)
    i16 = lax.broadcasted_iota(jnp.int32, (PEER_TOPK, s.shape[1]), 0)

    def body(k, carry):
        s, rank, top = carry
        m = jnp.max(s, axis=0, keepdims=True)
        idx = jnp.min(jnp.where(s == m, iota, nrows), axis=0, keepdims=True)
        sel = iota == idx
        rank = jnp.where(sel, k, rank)
        s = jnp.where(sel, -jnp.inf, s)
        top = jnp.where(i16 == k, m, top)
        return s, rank, top

    init = (s, jnp.full(s.shape, PEER_TOPK, jnp.int32), jnp.zeros((PEER_TOPK, s.shape[1]), F32))
    _, rank, top = lax.fori_loop(0, PEER_TOPK, body, init)
    return rank, top


_CAND_GROUPS = [(0, 16), (1, 8)] + [(k1, 8) for k1 in range(2, 8)]
_CAND_SINGLE0 = sum(n for _, n in _CAND_GROUPS)


def _psel_chunk(s1, s2):
    n = s1.shape[1]
    neg = jnp.full((8, n), -jnp.inf, F32)
    row8 = lax.broadcasted_iota(jnp.int32, (8, n), 0)
    r1, t1 = _topk16(s1)
    r2, t2 = _topk16(s2)
    pieces = []
    for k1, nrow in _CAND_GROUPS:
        blk = t1[k1:k1 + 1, :] + t2[0:nrow, :]
        nvalid = PEER_TOPK // (k1 + 1)
        if nvalid < nrow:
            blk = jnp.where(row8 < nvalid, blk, neg)
        pieces.append(blk)
    pieces.append(t1[8:16, :] + t2[0:1, :])
    cand = jnp.concatenate(pieces, axis=0)
    crank, _ = _topk16(cand)
    sel = crank < PEER_TOPK
    cmax = t1[0:1, :] + t2[0:1, :]
    z = jnp.sum(jnp.where(sel, jnp.exp(cand - cmax), 0.0), axis=0, keepdims=True)
    self32 = jnp.where(sel, 1.0, 0.0)
    cut = jnp.zeros((N_KEYS, n), F32)
    r0 = 0
    for k1, nrow in _CAND_GROUPS:
        cnt = jnp.sum(self32[r0:r0 + nrow, :], axis=0, keepdims=True)
        cut = jnp.where(r1 == k1, cnt, cut)
        r0 += nrow
    for j in range(8):
        cut = jnp.where(r1 == 8 + j, self32[r0 + j:r0 + j + 1, :], cut)
    e1 = jnp.exp(s1 - t1[0:1, :]) * (1.0 / z)
    e2 = jnp.exp(s2 - t2[0:1, :])
    return r2, cut, e1, e2


def _psel_kernel(x_ref, nw_ref, sh_ref, sc_ref, wq_ref, sk_ref,
                 h_ref, r2_ref, e2_ref, cut_ref, e1_ref, q_scr):
    tq = x_ref.shape[0]
    hb = _modnorm(x_ref[...], nw_ref[...], sh_ref[0, 0], sc_ref[0, 0]).astype(BF16)
    h_ref[...] = hb
    q_scr[...] = _dot(wq_ref[...], hb, NT)
    half = D_KEY // 2

    def head(hh, carry):
        q1 = q_scr[pl.ds(pl.multiple_of(hh * D_KEY, D_KEY), half), :]
        q2 = q_scr[pl.ds(pl.multiple_of(hh * D_KEY + half, half), half), :]
        s1 = _dot3(sk_ref[0], q1)
        s2 = _dot3(sk_ref[1], q2)
        for lc in range(tq // LANE):
            ls = slice(lc * LANE, (lc + 1) * LANE)
            r2, cut, e1, e2 = _psel_chunk(s1[:, ls], s2[:, ls])
            r2_ref[hh, :, ls] = r2.astype(F32).astype(BF16)
            e2_ref[hh, :, ls] = e2.astype(BF16)
            cut_ref[hh, :, ls] = cut
            e1_ref[hh, :, ls] = e1
        return carry

    lax.fori_loop(0, PEER_HEADS, head, 0)


def _psel_call(x, nw, mod4, layer, rows, wqt_bf, sub_keys, tq=256):
    t, d = x.shape
    nq = wqt_bf.shape[0]
    sel_shape = jax.ShapeDtypeStruct((PEER_HEADS, N_KEYS, t), F32)
    sel_bf = jax.ShapeDtypeStruct((PEER_HEADS, N_KEYS, t), BF16)
    sel_spec = pl.BlockSpec((PEER_HEADS, N_KEYS, tq), lambda i: (0, 0, i))
    return pl.pallas_call(
        _psel_kernel,
        out_shape=[jax.ShapeDtypeStruct((t, d), BF16), sel_bf, sel_bf, sel_shape, sel_shape],
        grid=(t // tq,),
        in_specs=[
            pl.BlockSpec((tq, d), lambda i: (i, 0)),
            pl.BlockSpec((1, d), lambda i: (0, 0)),
            rows.mod_spec(layer, 3, tq),
            rows.mod_spec(layer, 4, tq),
            pl.BlockSpec((nq, d), lambda i: (0, 0)),
            pl.BlockSpec((2, N_KEYS, D_KEY // 2), lambda i: (0, 0, 0)),
        ],
        out_specs=[pl.BlockSpec((tq, d), lambda i: (i, 0)), sel_spec, sel_spec, sel_spec, sel_spec],
        scratch_shapes=[pltpu.VMEM((nq, tq), F32)],
        compiler_params=_cparams(1),
        name="peer_select",
    )(x, nw.reshape(1, d), mod4, mod4, wqt_bf, sub_keys)


def _gelu_sigmoid_form(x):
    k = 2.0 * math.sqrt(2.0 / math.pi) * math.log2(math.e)
    u = x * (-k - (k * 0.044715) * (x * x))
    return x * (1.0 / (1.0 + jnp.exp2(u)))


def _pdense_kernel(h_ref, u_ref, v_ref, r2_ref, e2_ref, cut_ref, e1_ref, x_ref, g_ref, o_ref,
                   hid0, hid1, w0, w1, acc_scr, *, nj, n_real):
    s = pl.program_id(0)
    s3 = jnp.clip(s - 2, 0, n_real - 1)
    j3 = s3 % nj
    te, tq = hid0.shape
    bzero = jnp.zeros((), BF16)

    @pl.when(s == 0)
    def _():
        for r in (hid0, hid1, w0, w1):
            r[...] = jnp.zeros(r.shape, r.dtype)

    @pl.when(j3 == 0)
    def _():
        acc_scr[...] = jnp.zeros(acc_scr.shape, F32)

    def step(hid_new, hid_old, w_new, w_old):
        for a in range(te // N_KEYS):
            rs = slice(a * N_KEYS, (a + 1) * N_KEYS)
            if a % 2 == 0:
                rs2 = slice(a * N_KEYS, (a + 2) * N_KEYS)
                hid_new[rs2, :] = _dot(u_ref[rs2, :], h_ref[...], NT)
            for lc in range(tq // (2 * LANE)):
                ls = slice(lc * 2 * LANE, (lc + 1) * 2 * LANE)
                gate = None
                for hh in range(PEER_HEADS):
                    cut = jnp.broadcast_to(cut_ref[hh, a:a + 1, ls], (N_KEYS, 2 * LANE)).astype(BF16)
                    e1 = jnp.broadcast_to(e1_ref[hh, a:a + 1, ls], (N_KEYS, 2 * LANE)).astype(BF16)
                    term = jnp.where(r2_ref[hh, :, ls] < cut, e2_ref[hh, :, ls] * e1, bzero)
                    gate = term if gate is None else gate + term
                w_new[rs, ls] = _gelu_sigmoid_form(hid_old[rs, ls]).astype(BF16) * gate
            if a % 2 == 1:
                acc_scr[...] += _dot(w_old[rs2, :], v_ref[rs2, :], TN)

    @pl.when(s % 2 == 0)
    def _():
        step(hid0, hid1, w0, w1)

    @pl.when(s % 2 == 1)
    def _():
        step(hid1, hid0, w1, w0)

    @pl.when(jnp.logical_and(j3 == nj - 1, s >= 2))
    def _():
        o_ref[...] = x_ref[...] + g_ref[0, 0] * acc_scr[...]


def _pdense_call(h_bf, u_bf, v_bf, r2, e2, cut, e1, x, mod4, layer, rows, tq=512, te=1024):
    t, d = x.shape
    ne = u_bf.shape[0]
    ni, nj = t // tq, ne // te
    n_real = ni * nj
    na = te // N_KEYS

    def tile(shift):
        def f(s):
            ss = jnp.clip(s - shift, 0, n_real - 1)
            return ss // nj, ss % nj
        return f

    t1, t2, t3 = tile(0), tile(1), tile(2)
    return pl.pallas_call(
        functools.partial(_pdense_kernel, nj=nj, n_real=n_real),
        out_shape=jax.ShapeDtypeStruct((t, d), F32),
        grid=(n_real + 2,),
        in_specs=[
            pl.BlockSpec((tq, d), lambda s: (t1(s)[0], 0)),
            pl.BlockSpec((te, d), lambda s: (t1(s)[1], 0)),
            pl.BlockSpec((te, d), lambda s: (t3(s)[1], 0)),
            pl.BlockSpec((PEER_HEADS, N_KEYS, tq), lambda s: (0, 0, t2(s)[0])),
            pl.BlockSpec((PEER_HEADS, N_KEYS, tq), lambda s: (0, 0, t2(s)[0])),
            pl.BlockSpec((PEER_HEADS, na, tq), lambda s: (0, t2(s)[1], t2(s)[0])),
            pl.BlockSpec((PEER_HEADS, na, tq), lambda s: (0, t2(s)[1], t2(s)[0])),
            pl.BlockSpec((tq, d), lambda s: (t3(s)[0], 0)),
            rows.mod_spec(layer, 5, tq, tile_of=lambda s: t3(s)[0]),
        ],
        out_specs=pl.BlockSpec((tq, d), lambda s: (t3(s)[0], 0)),
        scratch_shapes=[pltpu.VMEM((te, tq), F32), pltpu.VMEM((te, tq), F32),
                        pltpu.VMEM((te, tq), BF16), pltpu.VMEM((te, tq), BF16),
                        pltpu.VMEM((tq, d), F32)],
        compiler_params=_cparams(1),
        name="peer_dense",
    )(h_bf, u_bf, v_bf, r2, e2, cut, e1, x, mod4)


def _final_kernel(x_ref, w_ref, o_ref):
    o_ref[...] = _rms(x_ref[...]) * w_ref[...]


def _final_call(x, w, row_block0, n_rows, tm=512):
    d = x.shape[1]
    return pl.pallas_call(
        _final_kernel,
        out_shape=jax.ShapeDtypeStruct((n_rows, d), F32),
        grid=(n_rows // tm,),
        in_specs=[pl.BlockSpec((tm, d), lambda i: (row_block0 + i, 0)),
                  pl.BlockSpec((1, d), lambda i: (0, 0))],
        out_specs=pl.BlockSpec((tm, d), lambda i: (i, 0)),
        compiler_params=_cparams(1),
        name="final_norm",
    )(x, w.reshape(1, d))


def _grid_pos_embed(rows, dtype):
    t = jnp.arange(rows * GRID_W)
    r = (t // GRID_W).astype(F32)
    col = (t % GRID_W).astype(F32)
    nf = D_MODEL // 4
    freqs = 1.0 / (10000.0 ** (jnp.arange(nf, dtype=F32) / nf))

    def emb(p):
        a = p[:, None] * freqs[None, :]
        return jnp.concatenate([jnp.sin(a), jnp.cos(a)], axis=-1)
    return jnp.concatenate([emb(r), emb(col)], axis=-1).astype(dtype)


def _even_w_in_layout(w):
    d = w.shape[0]
    a_end = 4 * W_A
    lr = w[:, a_end:a_end + GATE_RANK]
    uv = w[:, a_end + GATE_RANK:]
    pad = jnp.zeros((d, LANE - GATE_RANK), w.dtype)
    return jnp.concatenate([w[:, :a_end], uv, lr, pad], axis=1)


def _pad_gate_w(w):
    return jnp.concatenate([w, jnp.zeros((LANE - GATE_RANK, w.shape[1]), w.dtype)], axis=0)


def kernel(x_prompt, x_sample, state_gla, c, c_ctx, w_mod, b_mod, norm_w, final_norm_w, even_w_in, even_w_gate_f, even_b_gate_f, even_w_gate_b, even_b_gate_b, gla_norm_w, sgu_norm_w, sgu_w_s, sgu_b_s, even_w_out, odd_w_in, odd_conv_w, odd_w_out, peer_w_q, peer_sub_keys, peer_u, peer_v):
    bp, lp, d = x_prompt.shape
    bs, ls, _ = x_sample.shape
    depth = w_mod.shape[0]
    n_even = even_w_in.shape[0]
    tp = bp * lp
    ts = bs * ls
    rows = _Rows(tp, ls)

    xs = x_sample + _grid_pos_embed(ls // GRID_W, x_sample.dtype)[None]
    x = jnp.concatenate([x_prompt.reshape(tp, d), xs.reshape(ts, d)], axis=0)

    cc = jnp.concatenate([c_ctx[None], c, jnp.zeros((MOD_ROWS - 1 - bs, d), F32)], axis=0)
    mod4 = _mod_call(cc, w_mod, b_mod).reshape(depth, MOD_ROWS, 1, 6 * d)

    cd, sd = _dft_mats(DD)
    clp, slp = _dft_mats(lp)
    cls, sls = _dft_mats(ls)
    s0_all = state_gla.reshape(bs, n_even, 2 * H_A, DK_A, DV_A)

    new_states = []
    for l in range(depth):
        if l % 2 == 0:
            e = l // 2
            z = _normproj_call(x, norm_w[l, 0], mod4, l, rows, _even_w_in_layout(even_w_in[e]).astype(BF16))
            gate_args = (_pad_gate_w(even_w_gate_f[e]), even_b_gate_f[e][None],
                         _pad_gate_w(even_w_gate_b[e]), even_b_gate_b[e][None], gla_norm_w[e][None])
            o_p, sf, sb = _gla_call(z, lp, bp, 0, *gate_args, want_state=True)
            (o_s,) = _gla_call(z, ls, bs, tp // ls, *gate_args,
                               s0=s0_all[:, e].reshape(bs * 2 * H_A, DK_A, DV_A))
            a1 = jnp.concatenate([o_p, o_s], axis=0)
            bs_full = jnp.broadcast_to(sgu_b_s[e][:, :, None], (H_B, CHUNK_B, DB))
            a2 = _sgu_call(z, sgu_norm_w[e][None], sgu_w_s[e].astype(BF16), bs_full)
            x = _outproj_call(a1, a2, even_w_out[e].astype(BF16), x, mod4, l, rows)
            sfb = jnp.stack([sf.reshape(bp, H_A, DK_A, DV_A), sb.reshape(bp, H_A, DK_A, DV_A)], axis=1)
            new_states.append(sfb.astype(x_prompt.dtype))
        else:
            o = l // 2
            z = _normproj_call(x, norm_w[l, 0], mod4, l, rows, odd_w_in[o].astype(BF16))
            yc = jnp.concatenate([_conv_call(z, lp, bp, 0, odd_conv_w[o]),
                                  _conv_call(z, ls, bs, tp // ls, odd_conv_w[o])], axis=0)
            yd = jnp.concatenate([_fft_call(z, lp, bp, 0, cd, sd, clp, slp),
                                  _fft_call(z, ls, bs, tp // ls, cd, sd, cls, sls)], axis=0)
            x = _outproj_call(yc, yd, odd_w_out[o].astype(BF16), x, mod4, l, rows)
        h_bf, r2, e2, cut, e1 = _psel_call(x, norm_w[l, 1], mod4, l, rows,
                                           peer_w_q[l].T.astype(BF16), peer_sub_keys[l])
        x = _pdense_call(h_bf, peer_u[l].astype(BF16), peer_v[l].astype(BF16),
                         r2, e2, cut, e1, x, mod4, l, rows)

    y_prompt = _final_call(x, final_norm_w, 0, tp).reshape(bp, lp, d)
    y_sample = _final_call(x, final_norm_w, tp // 512, ts).reshape(bs, ls, d)
    state_new = jnp.stack(new_states, axis=1)
    return (y_prompt, y_sample, state_new)
```

```python
import functools
import math

import jax
import jax.numpy as jnp
import numpy as np
from jax import lax
from jax.experimental import pallas as pl
from jax.experimental.pallas import tpu as pltpu

F32 = jnp.float32
BF16 = jnp.bfloat16

D_MODEL = 1024
DEPTH = 4
GRID_W = 64
RMS_EPS = 1e-6
H_A = 4
DK_A = 128
DV_A = 128
GATE_RANK = 16
GATE_NORM = 16.0
GLA_CHUNK = 64
W_A = H_A * DV_A
H_B = 4
DB = 128
CHUNK_B = 128
W_B = H_B * DB
H_C = 4
DC = 128
W_C = H_C * DC
H_D = 4
DD = 128
W_D = H_D * DD
N_KEYS = 128
N_EXPERTS = N_KEYS * N_KEYS
PEER_HEADS = 8
PEER_TOPK = 16
D_KEY = 256

LANE = 128
MOD_ROWS = 8
EVEN_IN_PAD = 4 * W_A + 2 * W_B + LANE
VMEM_LIMIT = 48 * 1024 * 1024

NN = (((1,), (0,)), ((), ()))
NT = (((1,), (1,)), ((), ()))
TN = (((0,), (0,)), ((), ()))


def _cparams(n_axes, flags=None):
    return pltpu.CompilerParams(dimension_semantics=("arbitrary",) * n_axes,
                                vmem_limit_bytes=VMEM_LIMIT, flags=flags)


def _dot(a, b, dims=NN):
    return lax.dot_general(a, b, dims, preferred_element_type=F32)


def _split2(x):
    hi = x.astype(BF16)
    lo = (x - hi.astype(F32)).astype(BF16)
    return hi, lo


def _split3(x):
    hi = x.astype(BF16)
    r = x - hi.astype(F32)
    mid = r.astype(BF16)
    lo = (r - mid.astype(F32)).astype(BF16)
    return hi, mid, lo


def _dot3(a, b, dims=NN):
    ah, al = _split2(a)
    bh, bl = _split2(b)
    return _dot(ah, bh, dims) + _dot(al, bh, dims) + _dot(ah, bl, dims)


def _silu(x):
    return x * (1.0 / (1.0 + jnp.exp(-x)))


def _gelu(x):
    c = math.sqrt(2.0 / math.pi)
    return x * (0.5 * (1.0 + jnp.tanh(c * (x + 0.044715 * (x * x * x)))))


def _log_sigmoid(x):
    return jnp.minimum(x, 0.0) - jnp.log(1.0 + jnp.exp(-jnp.abs(x)))


def _rms(x):
    return x * lax.rsqrt(jnp.mean(x * x, axis=-1, keepdims=True) + RMS_EPS)


def _modnorm(x, nw, shift, scale):
    return (_rms(x) * nw) * (1.0 + scale) + shift


def _mod_kernel(cc_ref, w_ref, b_ref, o_ref):
    a = _silu(cc_ref[...])
    o_ref[0] = _dot3(a, w_ref[0]) + b_ref[0]


def _mod_call(cc, w_mod, b_mod):
    depth, d, n = w_mod.shape
    tn = 1024
    return pl.pallas_call(
        _mod_kernel,
        out_shape=jax.ShapeDtypeStruct((depth, MOD_ROWS, n), F32),
        grid=(depth, n // tn),
        in_specs=[
            pl.BlockSpec((MOD_ROWS, d), lambda l, j: (0, 0)),
            pl.BlockSpec((1, d, tn), lambda l, j: (l, 0, j)),
            pl.BlockSpec((1, 1, tn), lambda l, j: (l, 0, j)),
        ],
        out_specs=pl.BlockSpec((1, MOD_ROWS, tn), lambda l, j: (l, 0, j)),
        compiler_params=_cparams(2),
        name="mod_rows",
    )(cc, w_mod, b_mod.reshape(depth, 1, n))


class _Rows:
    def __init__(self, n_prompt_rows, dec_seq):
        self.n_prompt_rows = n_prompt_rows
        self.dec_seq = dec_seq

    def mod_row(self, i, tm):
        npt = self.n_prompt_rows // tm
        tps = self.dec_seq // tm
        return jnp.where(i < npt, 0, 1 + (i - npt) // tps)

    def mod_spec(self, layer, section, tm, tile_of=lambda i, *_: i):
        return pl.BlockSpec((1, 1, 1, D_MODEL),
                            lambda *g: (layer, self.mod_row(tile_of(*g), tm), 0, section))


def _normproj_kernel(x_ref, nw_ref, sh_ref, sc_ref, w_ref, o_ref):
    h = _modnorm(x_ref[...], nw_ref[...], sh_ref[0, 0], sc_ref[0, 0])
    o_ref[...] = _dot(h.astype(BF16), w_ref[...])


def _normproj_call(x, nw, mod4, layer, rows, w_bf, tm=256):
    t, d = x.shape
    n = w_bf.shape[1]
    return pl.pallas_call(
        _normproj_kernel,
        out_shape=jax.ShapeDtypeStruct((t, n), F32),
        grid=(t // tm,),
        in_specs=[
            pl.BlockSpec((tm, d), lambda i: (i, 0)),
            pl.BlockSpec((1, d), lambda i: (0, 0)),
            rows.mod_spec(layer, 0, tm),
            rows.mod_spec(layer, 1, tm),
            pl.BlockSpec((d, n), lambda i: (0, 0)),
        ],
        out_specs=pl.BlockSpec((tm, n), lambda i: (i, 0)),
        compiler_params=_cparams(1),
        name="normproj",
    )(x, nw.reshape(1, d), mod4, mod4, w_bf)


def _gla_kernel(*refs, has_s0, want_state):
    (q_ref, k_ref, v_ref, g_ref, lr_ref, wgf_ref, bgf_ref, wgb_ref, bgb_ref, nw_ref) = refs[:10]
    pos = 10
    if has_s0:
        s0f_ref, s0b_ref = refs[pos:pos + 2]
        pos += 2
    o_ref = refs[pos]
    pos += 1
    if want_state:
        sf_ref, sb_ref = refs[pos:pos + 2]
        pos += 2
    laf_scr, lab_scr, of_scr, ob_scr, s_scr = refs[pos:]

    seq = q_ref.shape[0]
    n_chunks = seq // GLA_CHUNK
    c = GLA_CHUNK
    scale = DK_A ** -0.5

    lr = lr_ref[...]
    laf_scr[...] = _log_sigmoid(_dot3(lr, wgf_ref[...]) + bgf_ref[...]) * (1.0 / GATE_NORM)
    lab_scr[...] = _log_sigmoid(_dot3(lr, wgb_ref[...]) + bgb_ref[...]) * (1.0 / GATE_NORM)
    if has_s0:
        s_scr[0] = s0f_ref[0]
        s_scr[1] = s0b_ref[0]
    else:
        s_scr[...] = jnp.zeros(s_scr.shape, F32)

    ri = lax.broadcasted_iota(jnp.int32, (c, c), 0)
    ci = lax.broadcasted_iota(jnp.int32, (c, c), 1)
    lower = ci <= ri
    upper = ci >= ri
    tril = jnp.where(lower, 1.0, 0.0).astype(BF16)
    triu = jnp.where(upper, 1.0, 0.0).astype(BF16)
    ones = jnp.ones((c, DV_A), BF16)

    def chunk(r0, la_scr, d, tri, mask, edge, out_scr):
        rs = pl.ds(r0, c)
        la_h, la_m, la_l = _split3(la_scr[rs, :])
        b = _dot(tri, la_h) + _dot(tri, la_m) + _dot(tri, la_l)
        b_edge = b[edge:edge + 1, :]
        q = q_ref[rs, :] * scale
        k = k_ref[rs, :]
        v = v_ref[rs, :].astype(BF16)
        qd = (q * jnp.exp(b)).astype(BF16)
        ki = (k * jnp.exp(-b)).astype(BF16)
        ks = (k * jnp.exp(b_edge - b)).astype(BF16)
        att = jnp.where(mask, _dot(qd, ki, NT), 0.0)
        s = s_scr[d]
        out_scr[rs, :] = _dot(att.astype(BF16), v) + _dot(qd, s.astype(BF16))
        kv = _dot(ks, v, TN)
        dec = _dot(la_h, ones, TN) + _dot(la_m, ones, TN) + _dot(la_l, ones, TN)
        s_scr[d] = jnp.exp(dec) * s + kv

    def body(i, carry):
        chunk(pl.multiple_of(i * c, c), laf_scr, 0, tril, lower, c - 1, of_scr)
        chunk(pl.multiple_of((n_chunks - 1 - i) * c, c), lab_scr, 1, triu, upper, 0, ob_scr)
        return carry

    lax.fori_loop(0, n_chunks, body, 0, unroll=4)

    o = _rms(of_scr[...] + ob_scr[...]) * nw_ref[...]
    o_ref[...] = (o * _silu(g_ref[...])).astype(o_ref.dtype)
    if want_state:
        sf_ref[0] = s_scr[0]
        sb_ref[0] = s_scr[1]


def _gla_call(z, seq, n_seq, row_block0, wgf, bgf, wgb, bgb, nw, s0=None, want_state=False):
    has_s0 = s0 is not None

    def col(cb):
        return pl.BlockSpec((seq, LANE), lambda s, h: (row_block0 + s, cb + h))

    in_specs = [col(0), col(H_A), col(2 * H_A), col(3 * H_A),
                pl.BlockSpec((seq, LANE), lambda s, h: (row_block0 + s, (4 * W_A + 2 * W_B) // LANE)),
                pl.BlockSpec((LANE, DK_A), lambda s, h: (0, h)),
                pl.BlockSpec((1, DK_A), lambda s, h: (0, h)),
                pl.BlockSpec((LANE, DK_A), lambda s, h: (0, h)),
                pl.BlockSpec((1, DK_A), lambda s, h: (0, h)),
                pl.BlockSpec((1, DV_A), lambda s, h: (0, 0))]
    args = [z, z, z, z, z, wgf, bgf, wgb, bgb, nw]
    if has_s0:
        in_specs += [pl.BlockSpec((1, DK_A, DV_A), lambda s, h: ((s * 2 + 0) * H_A + h, 0, 0)),
                     pl.BlockSpec((1, DK_A, DV_A), lambda s, h: ((s * 2 + 1) * H_A + h, 0, 0))]
        args += [s0, s0]
    out_shape = [jax.ShapeDtypeStruct((n_seq * seq, W_A), BF16)]
    out_specs = [pl.BlockSpec((seq, LANE), lambda s, h: (s, h))]
    if want_state:
        out_shape += [jax.ShapeDtypeStruct((n_seq * H_A, DK_A, DV_A), F32)] * 2
        out_specs += [pl.BlockSpec((1, DK_A, DV_A), lambda s, h: (s * H_A + h, 0, 0))] * 2
    return pl.pallas_call(
        functools.partial(_gla_kernel, has_s0=has_s0, want_state=want_state),
        out_shape=out_shape,
        grid=(n_seq, H_A),
        in_specs=in_specs,
        out_specs=out_specs,
        scratch_shapes=[pltpu.VMEM((seq, DK_A), F32), pltpu.VMEM((seq, DK_A), F32),
                        pltpu.VMEM((seq, DV_A), F32), pltpu.VMEM((seq, DV_A), F32),
                        pltpu.VMEM((2, DK_A, DV_A), F32)],
        compiler_params=_cparams(2),
        name="gla_seq%d" % seq,
    )(*args)


def _sgu_kernel(u_ref, v_ref, nw_ref, ws_ref, bs_ref, o_ref):
    tm = u_ref.shape[0]
    u = _gelu(u_ref[...])
    v = (_rms(_gelu(v_ref[...])) * nw_ref[...]).astype(BF16)
    ws = ws_ref[0]
    bs = bs_ref[0]
    for cidx in range(tm // CHUNK_B):
        rs = slice(cidx * CHUNK_B, (cidx + 1) * CHUNK_B)
        sg = _dot(ws, v[rs, :]) + bs
        o_ref[rs, :] = (u[rs, :] * sg).astype(o_ref.dtype)


def _sgu_call(z, nw, ws_bf, bs_full, tm=512):
    t = z.shape[0]
    ub0 = (4 * W_A) // LANE
    vb0 = (4 * W_A + W_B) // LANE
    return pl.pallas_call(
        _sgu_kernel,
        out_shape=jax.ShapeDtypeStruct((t, W_B), BF16),
        grid=(t // tm, H_B),
        in_specs=[
            pl.BlockSpec((tm, LANE), lambda i, h: (i, ub0 + h)),
            pl.BlockSpec((tm, LANE), lambda i, h: (i, vb0 + h)),
            pl.BlockSpec((1, DB), lambda i, h: (0, 0)),
            pl.BlockSpec((1, CHUNK_B, CHUNK_B), lambda i, h: (h, 0, 0)),
            pl.BlockSpec((1, CHUNK_B, DB), lambda i, h: (h, 0, 0)),
        ],
        out_specs=pl.BlockSpec((tm, LANE), lambda i, h: (i, h)),
        compiler_params=_cparams(2),
        name="sgu",
    )(z, z, nw, ws_bf, bs_full)


def _outproj_kernel(a1_ref, a2_ref, w_ref, x_ref, g_ref, o_ref):
    n1 = a1_ref.shape[1]
    y = _dot(a1_ref[...], w_ref[:n1, :]) + _dot(a2_ref[...], w_ref[n1:, :])
    o_ref[...] = x_ref[...] + g_ref[0, 0] * y


def _outproj_call(a1, a2, w_bf, x, mod4, layer, rows, tm=512):
    t, d = x.shape
    n1, n2 = a1.shape[1], a2.shape[1]
    return pl.pallas_call(
        _outproj_kernel,
        out_shape=jax.ShapeDtypeStruct((t, d), F32),
        grid=(t // tm,),
        in_specs=[
            pl.BlockSpec((tm, n1), lambda i: (i, 0)),
            pl.BlockSpec((tm, n2), lambda i: (i, 0)),
            pl.BlockSpec((n1 + n2, d), lambda i: (0, 0)),
            pl.BlockSpec((tm, d), lambda i: (i, 0)),
            rows.mod_spec(layer, 2, tm),
        ],
        out_specs=pl.BlockSpec((tm, d), lambda i: (i, 0)),
        compiler_params=_cparams(1),
        name="outproj",
    )(a1, a2, w_bf, x, mod4)


def _conv_kernel(bg_ref, cg_ref, xi_ref, w_ref, o_ref):
    seq = bg_ref.shape[0]
    t = cg_ref[...] * xi_ref[...]
    row = lax.broadcasted_iota(jnp.int32, t.shape, 0)
    t_prev = jnp.where(row == 0, 0.0, pltpu.roll(t, 1, 0))
    t_next = jnp.where(row == seq - 1, 0.0, pltpu.roll(t, seq - 1, 0))
    conv = t_prev * w_ref[0:1, :] + t * w_ref[1:2, :] + t_next * w_ref[2:3, :]
    o_ref[...] = (bg_ref[...] * conv).astype(o_ref.dtype)


def _conv_call(z, seq, n_seq, row_block0, conv_w):
    def col(cb):
        return pl.BlockSpec((seq, LANE), lambda s, j: (row_block0 + s, cb + j))
    return pl.pallas_call(
        _conv_kernel,
        out_shape=jax.ShapeDtypeStruct((n_seq * seq, W_C), BF16),
        grid=(n_seq, W_C // LANE),
        in_specs=[col(0), col(H_C), col(2 * H_C),
                  pl.BlockSpec((3, LANE), lambda s, j: (0, j))],
        out_specs=pl.BlockSpec((seq, LANE), lambda s, j: (s, j)),
        compiler_params=_cparams(2),
        name="conv_seq%d" % seq,
    )(z, z, z, conv_w)


def _fft_kernel(f_ref, cd_ref, sd_ref, cl_ref, sl_ref, o_ref, p_scr, q_scr, *, scale):
    @pl.when(pl.program_id(1) == 0)
    def _():
        cd = cd_ref[...]
        sd = sd_ref[...]
        for h in range(H_D):
            cs = slice(h * DD, (h + 1) * DD)
            fh = f_ref[:, cs].astype(BF16)
            p_scr[:, cs] = _dot(fh, cd).astype(BF16)
            q_scr[:, cs] = _dot(fh, sd).astype(BF16)

    y = _dot(cl_ref[...], p_scr[...]) - _dot(sl_ref[...], q_scr[...])
    o_ref[...] = (y * scale).astype(o_ref.dtype)


def _dft_mats(n):
    idx = jnp.arange(n, dtype=jnp.int32)
    ang = ((idx[:, None] * idx[None, :]) % n).astype(F32) * (2.0 * math.pi / n)
    return jnp.cos(ang).astype(BF16), jnp.sin(ang).astype(BF16)


def _fft_call(z, seq, n_seq, row_block0, cd, sd, cl, sl):
    tl = min(seq, 512)
    fcol = (3 * W_C) // W_D
    return pl.pallas_call(
        functools.partial(_fft_kernel, scale=1.0 / math.sqrt(seq * DD)),
        out_shape=jax.ShapeDtypeStruct((n_seq * seq, W_D), BF16),
        grid=(n_seq, seq // tl),
        in_specs=[
            pl.BlockSpec((seq, W_D), lambda s, i: (row_block0 + s, fcol)),
            pl.BlockSpec((DD, DD), lambda s, i: (0, 0)),
            pl.BlockSpec((DD, DD), lambda s, i: (0, 0)),
            pl.BlockSpec((tl, seq), lambda s, i: (i, 0)),
            pl.BlockSpec((tl, seq), lambda s, i: (i, 0)),
        ],
        out_specs=pl.BlockSpec((tl, W_D), lambda s, i: (s * (seq // tl) + i, 0)),
        scratch_shapes=[pltpu.VMEM((seq, W_D), BF16), pltpu.VMEM((seq, W_D), BF16)],
        compiler_params=_cparams(2),
        name="fft_seq%d" % seq,
    )(z, cd, sd, cl, sl)


def _topk16(s):
    nrows = s.shape[0]
    iota = lax.broadcasted_iota(jnp.int32, s.shape, 0)
    i16 = lax.broadcasted_iota(jnp.int32, (PEER_TOPK, s.shape[1]), 0)

    def body(k, carry):
        s, rank, top = carry
        m = jnp.max(s, axis=0, keepdims=True)
        idx = jnp.min(jnp.where(s == m, iota, nrows), axis=0, keepdims=True)
        sel = iota == idx
        rank = jnp.where(sel, k, rank)
        s = jnp.where(sel, -jnp.inf, s)
        top = jnp.where(i16 == k, m, top)
        return s, rank, top

    init = (s, jnp.full(s.shape, PEER_TOPK, jnp.int32), jnp.zeros((PEER_TOPK, s.shape[1]), F32))
    _, rank, top = lax.fori_loop(0, PEER_TOPK, body, init)
    return rank, top


_CAND_GROUPS = [(0, 16), (1, 8)] + [(k1, 8) for k1 in range(2, 8)]
_CAND_SINGLE0 = sum(n for _, n in _CAND_GROUPS)


_CODE0 = -(2 ** 31)


def _sort_key(x):
    i = pltpu.bitcast(jnp.where(x == 0.0, 0.0, x), jnp.int32)
    return i ^ ((i >> 31) & 0x7FFFFFFF)


def _key_to_f32(k):
    return pltpu.bitcast(k ^ ((k >> 31) & 0x7FFFFFFF), F32)


def _topk16_untied(scores):
    n = scores[0].shape[1]
    i16 = lax.broadcasted_iota(jnp.int32, (PEER_TOPK, n), 0)
    code0 = jnp.int32(_CODE0)

    def body(k, carry):
        keys, tops = carry
        new_keys, new_tops = [], []
        for key, top in zip(keys, tops):
            m = jnp.max(key, axis=0, keepdims=True)
            new_keys.append(jnp.where(key == m, code0 + k, key))
            new_tops.append(jnp.where(i16 == k, m, top))
        return tuple(new_keys), tuple(new_tops)

    init = (tuple(_sort_key(s) for s in scores),
            tuple(jnp.zeros((PEER_TOPK, n), jnp.int32) for _ in scores))
    keys, tops = lax.fori_loop(0, PEER_TOPK, body, init, unroll=True)
    out = []
    for key, top in zip(keys, tops):
        taken = key < code0 + PEER_TOPK
        rank = jnp.where(taken, key - code0, PEER_TOPK)
        count = jnp.sum(jnp.where(taken, 1, 0), axis=0, keepdims=True)
        out.append((rank, _key_to_f32(top), count))
    return out


def _cand_scores(t1, t2):
    n = t1.shape[1]
    neg = jnp.full((8, n), -jnp.inf, F32)
    row8 = lax.broadcasted_iota(jnp.int32, (8, n), 0)
    pieces = []
    for k1, nrow in _CAND_GROUPS:
        blk = t1[k1:k1 + 1, :] + t2[0:nrow, :]
        nvalid = PEER_TOPK // (k1 + 1)
        if nvalid < nrow:
            blk = jnp.where(row8 < nvalid, blk, neg)
        pieces.append(blk)
    pieces.append(t1[8:16, :] + t2[0:1, :])
    return jnp.concatenate(pieces, axis=0)


def _psel_finish(s1, s2, r1, t1, t2, cand, sel):
    n = s1.shape[1]
    cmax = t1[0:1, :] + t2[0:1, :]
    z = jnp.sum(jnp.where(sel, jnp.exp(cand - cmax), 0.0), axis=0, keepdims=True)
    self32 = jnp.where(sel, 1.0, 0.0)
    cut = jnp.zeros((N_KEYS, n), F32)
    r0 = 0
    for k1, nrow in _CAND_GROUPS:
        cnt = jnp.sum(self32[r0:r0 + nrow, :], axis=0, keepdims=True)
        cut = jnp.where(r1 == k1, cnt, cut)
        r0 += nrow
    for j in range(8):
        cut = jnp.where(r1 == 8 + j, self32[r0 + j:r0 + j + 1, :], cut)
    e1 = jnp.exp(s1 - t1[0:1, :]) * (1.0 / z)
    e2 = jnp.exp(s2 - t2[0:1, :])
    return cut, e1, e2


def _psel_kernel(x_ref, nw_ref, sh_ref, sc_ref, wq_ref, sk_ref,
                 h_ref, r2_ref, e2_ref, cut_ref, e1_ref, q_scr):
    tq = x_ref.shape[0]
    n_chunks = tq // LANE
    hb = _modnorm(x_ref[...], nw_ref[...], sh_ref[0, 0], sc_ref[0, 0]).astype(BF16)
    h_ref[...] = hb
    q_scr[...] = _dot(wq_ref[...], hb, NT)
    half = D_KEY // 2

    def scores(hh):
        q1 = q_scr[pl.ds(pl.multiple_of(hh * D_KEY, D_KEY), half), :]
        q2 = q_scr[pl.ds(pl.multiple_of(hh * D_KEY + half, half), half), :]
        return _dot3(sk_ref[0], q1), _dot3(sk_ref[1], q2)

    def write(hh, ls, r2, cut, e1, e2):
        r2_ref[hh, :, ls] = r2.astype(F32).astype(BF16)
        e2_ref[hh, :, ls] = e2.astype(BF16)
        cut_ref[hh, :, ls] = cut
        e1_ref[hh, :, ls] = e1

    def exact_chunk(hh, ls, s1c, s2c):
        r1, t1 = _topk16(s1c)
        r2, t2 = _topk16(s2c)
        cand = _cand_scores(t1, t2)
        crank, _ = _topk16(cand)
        cut, e1, e2 = _psel_finish(s1c, s2c, r1, t1, t2, cand, crank < PEER_TOPK)
        write(hh, ls, r2, cut, e1, e2)

    def head(hh, carry):
        s1, s2 = scores(hh)
        halves = []
        for lc in range(n_chunks):
            ls = slice(lc * LANE, (lc + 1) * LANE)
            halves.append(_topk16_untied([s1[:, ls], s2[:, ls]]))
        cands = [_cand_scores(h1[1], h2[1]) for h1, h2 in halves]
        picked = _topk16_untied(cands)
        for lc in range(n_chunks):
            ls = slice(lc * LANE, (lc + 1) * LANE)
            (r1, t1, c1), (r2, t2, c2) = halves[lc]
            crank, _, cc = picked[lc]
            cut, e1, e2 = _psel_finish(s1[:, ls], s2[:, ls], r1, t1, t2, cands[lc], crank < PEER_TOPK)
            write(hh, ls, r2, cut, e1, e2)
            extra = (c1 - PEER_TOPK) + (c2 - PEER_TOPK) + (cc - PEER_TOPK)

            @pl.when(jnp.max(extra) > 0)
            def _():
                exact_chunk(hh, ls, s1[:, ls], s2[:, ls])
        return carry

    lax.fori_loop(0, PEER_HEADS, head, 0)


def _psel_call(x, nw, mod4, layer, rows, wqt_bf, sub_keys, tq=256):
    t, d = x.shape
    nq = wqt_bf.shape[0]
    sel_shape = jax.ShapeDtypeStruct((PEER_HEADS, N_KEYS, t), F32)
    sel_bf = jax.ShapeDtypeStruct((PEER_HEADS, N_KEYS, t), BF16)
    sel_spec = pl.BlockSpec((PEER_HEADS, N_KEYS, tq), lambda i: (0, 0, i))
    return pl.pallas_call(
        _psel_kernel,
        out_shape=[jax.ShapeDtypeStruct((t, d), BF16), sel_bf, sel_bf, sel_shape, sel_shape],
        grid=(t // tq,),
        in_specs=[
            pl.BlockSpec((tq, d), lambda i: (i, 0)),
            pl.BlockSpec((1, d), lambda i: (0, 0)),
            rows.mod_spec(layer, 3, tq),
            rows.mod_spec(layer, 4, tq),
            pl.BlockSpec((nq, d), lambda i: (0, 0)),
            pl.BlockSpec((2, N_KEYS, D_KEY // 2), lambda i: (0, 0, 0)),
        ],
        out_specs=[pl.BlockSpec((tq, d), lambda i: (i, 0)), sel_spec, sel_spec, sel_spec, sel_spec],
        scratch_shapes=[pltpu.VMEM((nq, tq), F32)],
        compiler_params=_cparams(1),
        name="peer_select",
    )(x, nw.reshape(1, d), mod4, mod4, wqt_bf, sub_keys)


def _gelu_sigmoid_form(x):
    k = 2.0 * math.sqrt(2.0 / math.pi) * math.log2(math.e)
    u = x * (-k - (k * 0.044715) * (x * x))
    return x * (1.0 / (1.0 + jnp.exp2(u)))


def _pdense_kernel(h_ref, u_ref, v_ref, r2_ref, e2_ref, cut_ref, e1_ref, x_ref, g_ref, o_ref,
                   hid0, hid1, w0, w1, acc_scr, *, nj, n_real):
    s = pl.program_id(0)
    s3 = jnp.clip(s - 2, 0, n_real - 1)
    j3 = s3 % nj
    te, tq = hid0.shape
    bzero = jnp.zeros((), BF16)

    @pl.when(s == 0)
    def _():
        for r in (hid0, hid1, w0, w1):
            r[...] = jnp.zeros(r.shape, r.dtype)

    @pl.when(j3 == 0)
    def _():
        acc_scr[...] = jnp.zeros(acc_scr.shape, F32)

    def step(hid_new, hid_old, w_new, w_old):
        for a in range(te // N_KEYS):
            rs = slice(a * N_KEYS, (a + 1) * N_KEYS)
            if a % 2 == 0:
                rs2 = slice(a * N_KEYS, (a + 2) * N_KEYS)
                hid_new[rs2, :] = _dot(u_ref[rs2, :], h_ref[...], NT)
            for lc in range(tq // (2 * LANE)):
                ls = slice(lc * 2 * LANE, (lc + 1) * 2 * LANE)
                gate = None
                for hh in range(PEER_HEADS):
                    cut = jnp.broadcast_to(cut_ref[hh, a:a + 1, ls], (N_KEYS, 2 * LANE)).astype(BF16)
                    e1 = jnp.broadcast_to(e1_ref[hh, a:a + 1, ls], (N_KEYS, 2 * LANE)).astype(BF16)
                    term = jnp.where(r2_ref[hh, :, ls] < cut, e2_ref[hh, :, ls] * e1, bzero)
                    gate = term if gate is None else gate + term
                w_new[rs, ls] = _gelu_sigmoid_form(hid_old[rs, ls]).astype(BF16) * gate
            if a % 2 == 1:
                acc_scr[...] += _dot(w_old[rs2, :], v_ref[rs2, :], TN)

    @pl.when(s % 2 == 0)
    def _():
        step(hid0, hid1, w0, w1)

    @pl.when(s % 2 == 1)
    def _():
        step(hid1, hid0, w1, w0)

    @pl.when(jnp.logical_and(j3 == nj - 1, s >= 2))
    def _():
        o_ref[...] = x_ref[...] + g_ref[0, 0] * acc_scr[...]


def _pdense_call(h_bf, u_bf, v_bf, r2, e2, cut, e1, x, mod4, layer, rows, tq=512, te=1024):
    t, d = x.shape
    ne = u_bf.shape[0]
    ni, nj = t // tq, ne // te
    n_real = ni * nj
    na = te // N_KEYS

    def tile(shift):
        def f(s):
            ss = jnp.clip(s - shift, 0, n_real - 1)
            return ss // nj, ss % nj
        return f

    t1, t2, t3 = tile(0), tile(1), tile(2)
    return pl.pallas_call(
        functools.partial(_pdense_kernel, nj=nj, n_real=n_real),
        out_shape=jax.ShapeDtypeStruct((t, d), F32),
        grid=(n_real + 2,),
        in_specs=[
            pl.BlockSpec((tq, d), lambda s: (t1(s)[0], 0)),
            pl.BlockSpec((te, d), lambda s: (t1(s)[1], 0)),
            pl.BlockSpec((te, d), lambda s: (t3(s)[1], 0)),
            pl.BlockSpec((PEER_HEADS, N_KEYS, tq), lambda s: (0, 0, t2(s)[0])),
            pl.BlockSpec((PEER_HEADS, N_KEYS, tq), lambda s: (0, 0, t2(s)[0])),
            pl.BlockSpec((PEER_HEADS, na, tq), lambda s: (0, t2(s)[1], t2(s)[0])),
            pl.BlockSpec((PEER_HEADS, na, tq), lambda s: (0, t2(s)[1], t2(s)[0])),
            pl.BlockSpec((tq, d), lambda s: (t3(s)[0], 0)),
            rows.mod_spec(layer, 5, tq, tile_of=lambda s: t3(s)[0]),
        ],
        out_specs=pl.BlockSpec((tq, d), lambda s: (t3(s)[0], 0)),
        scratch_shapes=[pltpu.VMEM((te, tq), F32), pltpu.VMEM((te, tq), F32),
                        pltpu.VMEM((te, tq), BF16), pltpu.VMEM((te, tq), BF16),
                        pltpu.VMEM((tq, d), F32)],
        compiler_params=_cparams(1),
        name="peer_dense",
    )(h_bf, u_bf, v_bf, r2, e2, cut, e1, x, mod4)


def _final_kernel(x_ref, w_ref, o_ref):
    o_ref[...] = _rms(x_ref[...]) * w_ref[...]


def _final_call(x, w, row_block0, n_rows, tm=512):
    d = x.shape[1]
    return pl.pallas_call(
        _final_kernel,
        out_shape=jax.ShapeDtypeStruct((n_rows, d), F32),
        grid=(n_rows // tm,),
        in_specs=[pl.BlockSpec((tm, d), lambda i: (row_block0 + i, 0)),
                  pl.BlockSpec((1, d), lambda i: (0, 0))],
        out_specs=pl.BlockSpec((tm, d), lambda i: (i, 0)),
        compiler_params=_cparams(1),
        name="final_norm",
    )(x, w.reshape(1, d))


def _grid_pos_embed(rows, dtype):
    t = jnp.arange(rows * GRID_W)
    r = (t // GRID_W).astype(F32)
    col = (t % GRID_W).astype(F32)
    nf = D_MODEL // 4
    freqs = 1.0 / (10000.0 ** (jnp.arange(nf, dtype=F32) / nf))

    def emb(p):
        a = p[:, None] * freqs[None, :]
        return jnp.concatenate([jnp.sin(a), jnp.cos(a)], axis=-1)
    return jnp.concatenate([emb(r), emb(col)], axis=-1).astype(dtype)


def _even_w_in_layout(w):
    d = w.shape[0]
    a_end = 4 * W_A
    lr = w[:, a_end:a_end + GATE_RANK]
    uv = w[:, a_end + GATE_RANK:]
    pad = jnp.zeros((d, LANE - GATE_RANK), w.dtype)
    return jnp.concatenate([w[:, :a_end], uv, lr, pad], axis=1)


def _pad_gate_w(w):
    return jnp.concatenate([w, jnp.zeros((LANE - GATE_RANK, w.shape[1]), w.dtype)], axis=0)


def kernel(x_prompt, x_sample, state_gla, c, c_ctx, w_mod, b_mod, norm_w, final_norm_w, even_w_in, even_w_gate_f, even_b_gate_f, even_w_gate_b, even_b_gate_b, gla_norm_w, sgu_norm_w, sgu_w_s, sgu_b_s, even_w_out, odd_w_in, odd_conv_w, odd_w_out, peer_w_q, peer_sub_keys, peer_u, peer_v):
    bp, lp, d = x_prompt.shape
    bs, ls, _ = x_sample.shape
    depth = w_mod.shape[0]
    n_even = even_w_in.shape[0]
    tp = bp * lp
    ts = bs * ls
    rows = _Rows(tp, ls)

    xs = x_sample + _grid_pos_embed(ls // GRID_W, x_sample.dtype)[None]
    x = jnp.concatenate([x_prompt.reshape(tp, d), xs.reshape(ts, d)], axis=0)

    cc = jnp.concatenate([c_ctx[None], c, jnp.zeros((MOD_ROWS - 1 - bs, d), F32)], axis=0)
    mod4 = _mod_call(cc, w_mod, b_mod).reshape(depth, MOD_ROWS, 1, 6 * d)

    cd, sd = _dft_mats(DD)
    clp, slp = _dft_mats(lp)
    cls, sls = _dft_mats(ls)
    s0_all = state_gla.reshape(bs, n_even, 2 * H_A, DK_A, DV_A)

    new_states = []
    for l in range(depth):
        if l % 2 == 0:
            e = l // 2
            z = _normproj_call(x, norm_w[l, 0], mod4, l, rows, _even_w_in_layout(even_w_in[e]).astype(BF16))
            gate_args = (_pad_gate_w(even_w_gate_f[e]), even_b_gate_f[e][None],
                         _pad_gate_w(even_w_gate_b[e]), even_b_gate_b[e][None], gla_norm_w[e][None])
            o_p, sf, sb = _gla_call(z, lp, bp, 0, *gate_args, want_state=True)
            (o_s,) = _gla_call(z, ls, bs, tp // ls, *gate_args,
                               s0=s0_all[:, e].reshape(bs * 2 * H_A, DK_A, DV_A))
            a1 = jnp.concatenate([o_p, o_s], axis=0)
            bs_full = jnp.broadcast_to(sgu_b_s[e][:, :, None], (H_B, CHUNK_B, DB))
            a2 = _sgu_call(z, sgu_norm_w[e][None], sgu_w_s[e].astype(BF16), bs_full)
            x = _outproj_call(a1, a2, even_w_out[e].astype(BF16), x, mod4, l, rows)
            sfb = jnp.stack([sf.reshape(bp, H_A, DK_A, DV_A), sb.reshape(bp, H_A, DK_A, DV_A)], axis=1)
            new_states.append(sfb.astype(x_prompt.dtype))
        else:
            o = l // 2
            z = _normproj_call(x, norm_w[l, 0], mod4, l, rows, odd_w_in[o].astype(BF16))
            yc = jnp.concatenate([_conv_call(z, lp, bp, 0, odd_conv_w[o]),
                                  _conv_call(z, ls, bs, tp // ls, odd_conv_w[o])], axis=0)
            yd = jnp.concatenate([_fft_call(z, lp, bp, 0, cd, sd, clp, slp),
                                  _fft_call(z, ls, bs, tp // ls, cd, sd, cls, sls)], axis=0)
            x = _outproj_call(yc, yd, odd_w_out[o].astype(BF16), x, mod4, l, rows)
        h_bf, r2, e2, cut, e1 = _psel_call(x, norm_w[l, 1], mod4, l, rows,
                                           peer_w_q[l].T.astype(BF16), peer_sub_keys[l])
        x = _pdense_call(h_bf, peer_u[l].astype(BF16), peer_v[l].astype(BF16),
                         r2, e2, cut, e1, x, mod4, l, rows)

    y_prompt = _final_call(x, final_norm_w, 0, tp).reshape(bp, lp, d)
    y_sample = _final_call(x, final_norm_w, tp // 512, ts).reshape(bs, ls, d)
    state_new = jnp.stack(new_states, axis=1)
    return (y_prompt, y_sample, state_new)
```

```python
import functools
import math

import jax
import jax.numpy as jnp
import numpy as np
from jax import lax
from jax.experimental import pallas as pl
from jax.experimental.pallas import tpu as pltpu

F32 = jnp.float32
BF16 = jnp.bfloat16

D_MODEL = 1024
DEPTH = 4
GRID_W = 64
RMS_EPS = 1e-6
H_A = 4
DK_A = 128
DV_A = 128
GATE_RANK = 16
GATE_NORM = 16.0
GLA_CHUNK = 64
W_A = H_A * DV_A
H_B = 4
DB = 128
CHUNK_B = 128
W_B = H_B * DB
H_C = 4
DC = 128
W_C = H_C * DC
H_D = 4
DD = 128
W_D = H_D * DD
N_KEYS = 128
N_EXPERTS = N_KEYS * N_KEYS
PEER_HEADS = 8
PEER_TOPK = 16
D_KEY = 256

LANE = 128
MOD_ROWS = 8
EVEN_IN_PAD = 4 * W_A + 2 * W_B + LANE
VMEM_LIMIT = 48 * 1024 * 1024

NN = (((1,), (0,)), ((), ()))
NT = (((1,), (1,)), ((), ()))
TN = (((0,), (0,)), ((), ()))


def _cparams(n_axes, flags=None):
    return pltpu.CompilerParams(dimension_semantics=("arbitrary",) * n_axes,
                                vmem_limit_bytes=VMEM_LIMIT, flags=flags)


def _dot(a, b, dims=NN):
    return lax.dot_general(a, b, dims, preferred_element_type=F32)


def _split2(x):
    hi = x.astype(BF16)
    lo = (x - hi.astype(F32)).astype(BF16)
    return hi, lo


def _split3(x):
    hi = x.astype(BF16)
    r = x - hi.astype(F32)
    mid = r.astype(BF16)
    lo = (r - mid.astype(F32)).astype(BF16)
    return hi, mid, lo


def _dot3(a, b, dims=NN):
    ah, al = _split2(a)
    bh, bl = _split2(b)
    return _dot(ah, bh, dims) + _dot(al, bh, dims) + _dot(ah, bl, dims)


def _silu(x):
    return x * (1.0 / (1.0 + jnp.exp(-x)))


def _gelu(x):
    c = math.sqrt(2.0 / math.pi)
    return x * (0.5 * (1.0 + jnp.tanh(c * (x + 0.044715 * (x * x * x)))))


def _log_sigmoid(x):
    return jnp.minimum(x, 0.0) - jnp.log(1.0 + jnp.exp(-jnp.abs(x)))


def _rms(x):
    return x * lax.rsqrt(jnp.mean(x * x, axis=-1, keepdims=True) + RMS_EPS)


def _modnorm(x, nw, shift, scale):
    return (_rms(x) * nw) * (1.0 + scale) + shift


def _mod_kernel(cc_ref, w_ref, b_ref, o_ref):
    a = _silu(cc_ref[...])
    o_ref[0] = _dot3(a, w_ref[0]) + b_ref[0]


def _mod_call(cc, w_mod, b_mod):
    depth, d, n = w_mod.shape
    tn = 1024
    return pl.pallas_call(
        _mod_kernel,
        out_shape=jax.ShapeDtypeStruct((depth, MOD_ROWS, n), F32),
        grid=(depth, n // tn),
        in_specs=[
            pl.BlockSpec((MOD_ROWS, d), lambda l, j: (0, 0)),
            pl.BlockSpec((1, d, tn), lambda l, j: (l, 0, j)),
            pl.BlockSpec((1, 1, tn), lambda l, j: (l, 0, j)),
        ],
        out_specs=pl.BlockSpec((1, MOD_ROWS, tn), lambda l, j: (l, 0, j)),
        compiler_params=_cparams(2),
        name="mod_rows",
    )(cc, w_mod, b_mod.reshape(depth, 1, n))


class _Rows:
    def __init__(self, n_prompt_rows, dec_seq):
        self.n_prompt_rows = n_prompt_rows
        self.dec_seq = dec_seq

    def mod_row(self, i, tm):
        npt = self.n_prompt_rows // tm
        tps = self.dec_seq // tm
        return jnp.where(i < npt, 0, 1 + (i - npt) // tps)

    def mod_spec(self, layer, section, tm, tile_of=lambda i, *_: i):
        return pl.BlockSpec((1, 1, 1, D_MODEL),
                            lambda *g: (layer, self.mod_row(tile_of(*g), tm), 0, section))


def _normproj_kernel(x_ref, nw_ref, sh_ref, sc_ref, w_ref, o_ref):
    h = _modnorm(x_ref[...], nw_ref[...], sh_ref[0, 0], sc_ref[0, 0])
    o_ref[...] = _dot(h.astype(BF16), w_ref[...])


def _normproj_call(x, nw, mod4, layer, rows, w_bf, tm=256):
    t, d = x.shape
    n = w_bf.shape[1]
    return pl.pallas_call(
        _normproj_kernel,
        out_shape=jax.ShapeDtypeStruct((t, n), F32),
        grid=(t // tm,),
        in_specs=[
            pl.BlockSpec((tm, d), lambda i: (i, 0)),
            pl.BlockSpec((1, d), lambda i: (0, 0)),
            rows.mod_spec(layer, 0, tm),
            rows.mod_spec(layer, 1, tm),
            pl.BlockSpec((d, n), lambda i: (0, 0)),
        ],
        out_specs=pl.BlockSpec((tm, n), lambda i: (i, 0)),
        compiler_params=_cparams(1),
        name="normproj",
    )(x, nw.reshape(1, d), mod4, mod4, w_bf)


def _gla_kernel(*refs, has_s0, want_state):
    (q_ref, k_ref, v_ref, g_ref, lr_ref, wgf_ref, bgf_ref, wgb_ref, bgb_ref, nw_ref) = refs[:10]
    pos = 10
    if has_s0:
        s0f_ref, s0b_ref = refs[pos:pos + 2]
        pos += 2
    o_ref = refs[pos]
    pos += 1
    if want_state:
        sf_ref, sb_ref = refs[pos:pos + 2]
        pos += 2
    laf_scr, lab_scr, of_scr, ob_scr, s_scr = refs[pos:]

    seq = q_ref.shape[0]
    n_chunks = seq // GLA_CHUNK
    c = GLA_CHUNK
    scale = DK_A ** -0.5

    lr = lr_ref[...]
    laf_scr[...] = _log_sigmoid(_dot3(lr, wgf_ref[...]) + bgf_ref[...]) * (1.0 / GATE_NORM)
    lab_scr[...] = _log_sigmoid(_dot3(lr, wgb_ref[...]) + bgb_ref[...]) * (1.0 / GATE_NORM)
    if has_s0:
        s_scr[0] = s0f_ref[0]
        s_scr[1] = s0b_ref[0]
    else:
        s_scr[...] = jnp.zeros(s_scr.shape, F32)

    ri = lax.broadcasted_iota(jnp.int32, (c, c), 0)
    ci = lax.broadcasted_iota(jnp.int32, (c, c), 1)
    lower = ci <= ri
    upper = ci >= ri
    tril = jnp.where(lower, 1.0, 0.0).astype(BF16)
    triu = jnp.where(upper, 1.0, 0.0).astype(BF16)
    ones = jnp.ones((c, DV_A), BF16)

    def chunk(r0, la_scr, d, tri, mask, edge, out_scr):
        rs = pl.ds(r0, c)
        la_h, la_m, la_l = _split3(la_scr[rs, :])
        b = _dot(tri, la_h) + _dot(tri, la_m) + _dot(tri, la_l)
        b_edge = b[edge:edge + 1, :]
        q = q_ref[rs, :] * scale
        k = k_ref[rs, :]
        v = v_ref[rs, :].astype(BF16)
        qd = (q * jnp.exp(b)).astype(BF16)
        ki = (k * jnp.exp(-b)).astype(BF16)
        ks = (k * jnp.exp(b_edge - b)).astype(BF16)
        att = jnp.where(mask, _dot(qd, ki, NT), 0.0)
        s = s_scr[d]
        out_scr[rs, :] = _dot(att.astype(BF16), v) + _dot(qd, s.astype(BF16))
        kv = _dot(ks, v, TN)
        dec = _dot(la_h, ones, TN) + _dot(la_m, ones, TN) + _dot(la_l, ones, TN)
        s_scr[d] = jnp.exp(dec) * s + kv

    def body(i, carry):
        chunk(pl.multiple_of(i * c, c), laf_scr, 0, tril, lower, c - 1, of_scr)
        chunk(pl.multiple_of((n_chunks - 1 - i) * c, c), lab_scr, 1, triu, upper, 0, ob_scr)
        return carry

    lax.fori_loop(0, n_chunks, body, 0, unroll=4)

    o = _rms(of_scr[...] + ob_scr[...]) * nw_ref[...]
    o_ref[...] = (o * _silu(g_ref[...])).astype(o_ref.dtype)
    if want_state:
        sf_ref[0] = s_scr[0]
        sb_ref[0] = s_scr[1]


def _gla_call(z, seq, n_seq, row_block0, wgf, bgf, wgb, bgb, nw, s0=None, want_state=False):
    has_s0 = s0 is not None

    def col(cb):
        return pl.BlockSpec((seq, LANE), lambda s, h: (row_block0 + s, cb + h))

    in_specs = [col(0), col(H_A), col(2 * H_A), col(3 * H_A),
                pl.BlockSpec((seq, LANE), lambda s, h: (row_block0 + s, (4 * W_A + 2 * W_B) // LANE)),
                pl.BlockSpec((LANE, DK_A), lambda s, h: (0, h)),
                pl.BlockSpec((1, DK_A), lambda s, h: (0, h)),
                pl.BlockSpec((LANE, DK_A), lambda s, h: (0, h)),
                pl.BlockSpec((1, DK_A), lambda s, h: (0, h)),
                pl.BlockSpec((1, DV_A), lambda s, h: (0, 0))]
    args = [z, z, z, z, z, wgf, bgf, wgb, bgb, nw]
    if has_s0:
        in_specs += [pl.BlockSpec((1, DK_A, DV_A), lambda s, h: ((s * 2 + 0) * H_A + h, 0, 0)),
                     pl.BlockSpec((1, DK_A, DV_A), lambda s, h: ((s * 2 + 1) * H_A + h, 0, 0))]
        args += [s0, s0]
    out_shape = [jax.ShapeDtypeStruct((n_seq * seq, W_A), BF16)]
    out_specs = [pl.BlockSpec((seq, LANE), lambda s, h: (s, h))]
    if want_state:
        out_shape += [jax.ShapeDtypeStruct((n_seq * H_A, DK_A, DV_A), F32)] * 2
        out_specs += [pl.BlockSpec((1, DK_A, DV_A), lambda s, h: (s * H_A + h, 0, 0))] * 2
    return pl.pallas_call(
        functools.partial(_gla_kernel, has_s0=has_s0, want_state=want_state),
        out_shape=out_shape,
        grid=(n_seq, H_A),
        in_specs=in_specs,
        out_specs=out_specs,
        scratch_shapes=[pltpu.VMEM((seq, DK_A), F32), pltpu.VMEM((seq, DK_A), F32),
                        pltpu.VMEM((seq, DV_A), F32), pltpu.VMEM((seq, DV_A), F32),
                        pltpu.VMEM((2, DK_A, DV_A), F32)],
        compiler_params=_cparams(2),
        name="gla_seq%d" % seq,
    )(*args)


def _sgu_kernel(u_ref, v_ref, nw_ref, ws_ref, bs_ref, o_ref):
    tm = u_ref.shape[0]
    u = _gelu(u_ref[...])
    v = (_rms(_gelu(v_ref[...])) * nw_ref[...]).astype(BF16)
    ws = ws_ref[0]
    bs = bs_ref[0]
    for cidx in range(tm // CHUNK_B):
        rs = slice(cidx * CHUNK_B, (cidx + 1) * CHUNK_B)
        sg = _dot(ws, v[rs, :]) + bs
        o_ref[rs, :] = (u[rs, :] * sg).astype(o_ref.dtype)


def _sgu_call(z, nw, ws_bf, bs_full, tm=512):
    t = z.shape[0]
    ub0 = (4 * W_A) // LANE
    vb0 = (4 * W_A + W_B) // LANE
    return pl.pallas_call(
        _sgu_kernel,
        out_shape=jax.ShapeDtypeStruct((t, W_B), BF16),
        grid=(t // tm, H_B),
        in_specs=[
            pl.BlockSpec((tm, LANE), lambda i, h: (i, ub0 + h)),
            pl.BlockSpec((tm, LANE), lambda i, h: (i, vb0 + h)),
            pl.BlockSpec((1, DB), lambda i, h: (0, 0)),
            pl.BlockSpec((1, CHUNK_B, CHUNK_B), lambda i, h: (h, 0, 0)),
            pl.BlockSpec((1, CHUNK_B, DB), lambda i, h: (h, 0, 0)),
        ],
        out_specs=pl.BlockSpec((tm, LANE), lambda i, h: (i, h)),
        compiler_params=_cparams(2),
        name="sgu",
    )(z, z, nw, ws_bf, bs_full)


def _outproj_kernel(a1_ref, a2_ref, w_ref, x_ref, g_ref, o_ref):
    n1 = a1_ref.shape[1]
    y = _dot(a1_ref[...], w_ref[:n1, :]) + _dot(a2_ref[...], w_ref[n1:, :])
    o_ref[...] = x_ref[...] + g_ref[0, 0] * y


def _outproj_call(a1, a2, w_bf, x, mod4, layer, rows, tm=512):
    t, d = x.shape
    n1, n2 = a1.shape[1], a2.shape[1]
    return pl.pallas_call(
        _outproj_kernel,
        out_shape=jax.ShapeDtypeStruct((t, d), F32),
        grid=(t // tm,),
        in_specs=[
            pl.BlockSpec((tm, n1), lambda i: (i, 0)),
            pl.BlockSpec((tm, n2), lambda i: (i, 0)),
            pl.BlockSpec((n1 + n2, d), lambda i: (0, 0)),
            pl.BlockSpec((tm, d), lambda i: (i, 0)),
            rows.mod_spec(layer, 2, tm),
        ],
        out_specs=pl.BlockSpec((tm, d), lambda i: (i, 0)),
        compiler_params=_cparams(1),
        name="outproj",
    )(a1, a2, w_bf, x, mod4)


def _conv_kernel(bg_ref, cg_ref, xi_ref, w_ref, o_ref):
    seq = bg_ref.shape[0]
    t = cg_ref[...] * xi_ref[...]
    row = lax.broadcasted_iota(jnp.int32, t.shape, 0)
    t_prev = jnp.where(row == 0, 0.0, pltpu.roll(t, 1, 0))
    t_next = jnp.where(row == seq - 1, 0.0, pltpu.roll(t, seq - 1, 0))
    conv = t_prev * w_ref[0:1, :] + t * w_ref[1:2, :] + t_next * w_ref[2:3, :]
    o_ref[...] = (bg_ref[...] * conv).astype(o_ref.dtype)


def _conv_call(z, seq, n_seq, row_block0, conv_w):
    def col(cb):
        return pl.BlockSpec((seq, LANE), lambda s, j: (row_block0 + s, cb + j))
    return pl.pallas_call(
        _conv_kernel,
        out_shape=jax.ShapeDtypeStruct((n_seq * seq, W_C), BF16),
        grid=(n_seq, W_C // LANE),
        in_specs=[col(0), col(H_C), col(2 * H_C),
                  pl.BlockSpec((3, LANE), lambda s, j: (0, j))],
        out_specs=pl.BlockSpec((seq, LANE), lambda s, j: (s, j)),
        compiler_params=_cparams(2),
        name="conv_seq%d" % seq,
    )(z, z, z, conv_w)


def _fft_kernel(f_ref, cd_ref, sd_ref, cl_ref, sl_ref, o_ref, p_scr, q_scr, *, scale):
    @pl.when(pl.program_id(1) == 0)
    def _():
        cd = cd_ref[...]
        sd = sd_ref[...]
        for h in range(H_D):
            cs = slice(h * DD, (h + 1) * DD)
            fh = f_ref[:, cs].astype(BF16)
            p_scr[:, cs] = _dot(fh, cd).astype(BF16)
            q_scr[:, cs] = _dot(fh, sd).astype(BF16)

    y = _dot(cl_ref[...], p_scr[...]) - _dot(sl_ref[...], q_scr[...])
    o_ref[...] = (y * scale).astype(o_ref.dtype)


def _dft_mats(n):
    idx = jnp.arange(n, dtype=jnp.int32)
    ang = ((idx[:, None] * idx[None, :]) % n).astype(F32) * (2.0 * math.pi / n)
    return jnp.cos(ang).astype(BF16), jnp.sin(ang).astype(BF16)


def _fft_call(z, seq, n_seq, row_block0, cd, sd, cl, sl):
    tl = min(seq, 512)
    fcol = (3 * W_C) // W_D
    return pl.pallas_call(
        functools.partial(_fft_kernel, scale=1.0 / math.sqrt(seq * DD)),
        out_shape=jax.ShapeDtypeStruct((n_seq * seq, W_D), BF16),
        grid=(n_seq, seq // tl),
        in_specs=[
            pl.BlockSpec((seq, W_D), lambda s, i: (row_block0 + s, fcol)),
            pl.BlockSpec((DD, DD), lambda s, i: (0, 0)),
            pl.BlockSpec((DD, DD), lambda s, i: (0, 0)),
            pl.BlockSpec((tl, seq), lambda s, i: (i, 0)),
            pl.BlockSpec((tl, seq), lambda s, i: (i, 0)),
        ],
        out_specs=pl.BlockSpec((tl, W_D), lambda s, i: (s * (seq // tl) + i, 0)),
        scratch_shapes=[pltpu.VMEM((seq, W_D), BF16), pltpu.VMEM((seq, W_D), BF16)],
        compiler_params=_cparams(2),
        name="fft_seq%d" % seq,
    )(z, cd, sd, cl, sl)


def _topk16(s):
    nrows = s.shape[0]
    iota = lax.broadcasted_iota(jnp.int32, s.shape, 0)
    i16 = lax.broadcasted_iota(jnp.int32, (PEER_TOPK, s.shape[1]), 0)

    def body(k, carry):
        s, rank, top = carry
        m = jnp.max(s, axis=0, keepdims=True)
        idx = jnp.min(jnp.where(s == m, iota, nrows), axis=0, keepdims=True)
        sel = iota == idx
        rank = jnp.where(sel, k, rank)
        s = jnp.where(sel, -jnp.inf, s)
        top = jnp.where(i16 == k, m, top)
        return s, rank, top

    init = (s, jnp.full(s.shape, PEER_TOPK, jnp.int32), jnp.zeros((PEER_TOPK, s.shape[1]), F32))
    _, rank, top = lax.fori_loop(0, PEER_TOPK, body, init)
    return rank, top


_CAND_GROUPS = [(0, 16), (1, 8)] + [(k1, 8) for k1 in range(2, 8)]
_CAND_SINGLE0 = sum(n for _, n in _CAND_GROUPS)


_SCORE_FLOOR = -(2.0 ** 126)
_CODE_UNIT = 2.0 ** 127


def _rank_code(k):
    return -_CODE_UNIT * (1.0 + (k + 1) / 32.0)


def _topk16_untied(scores):
    n = scores[0].shape[1]
    i16 = lax.broadcasted_iota(jnp.int32, (PEER_TOPK, n), 0)
    keys = [jnp.maximum(s, _SCORE_FLOOR) for s in scores]
    tops = [jnp.zeros((PEER_TOPK, n), F32) for _ in scores]
    for k in range(PEER_TOPK):
        for c in range(len(keys)):
            m = jnp.max(keys[c], axis=0, keepdims=True)
            keys[c] = jnp.where(keys[c] == m, _rank_code(k), keys[c])
            tops[c] = jnp.where(i16 == k, m, tops[c])
    out = []
    for key, top in zip(keys, tops):
        taken = key < _SCORE_FLOOR
        rank = jnp.where(taken, key * -(32.0 / _CODE_UNIT) - 33.0, float(PEER_TOPK))
        count = jnp.sum(jnp.where(taken, 1.0, 0.0), axis=0, keepdims=True)
        clamped = jnp.sum(jnp.where(top <= _SCORE_FLOOR, 1.0, 0.0), axis=0, keepdims=True)
        out.append((rank, top, (count - PEER_TOPK) + clamped))
    return out


def _cand_scores(t1, t2):
    n = t1.shape[1]
    neg = jnp.full((8, n), -jnp.inf, F32)
    row8 = lax.broadcasted_iota(jnp.int32, (8, n), 0)
    pieces = []
    for k1, nrow in _CAND_GROUPS:
        blk = t1[k1:k1 + 1, :] + t2[0:nrow, :]
        nvalid = PEER_TOPK // (k1 + 1)
        if nvalid < nrow:
            blk = jnp.where(row8 < nvalid, blk, neg)
        pieces.append(blk)
    pieces.append(t1[8:16, :] + t2[0:1, :])
    return jnp.concatenate(pieces, axis=0)


def _psel_finish(s1, s2, r1, t1, t2, cand, sel):
    n = s1.shape[1]
    cmax = t1[0:1, :] + t2[0:1, :]
    z = jnp.sum(jnp.where(sel, jnp.exp(cand - cmax), 0.0), axis=0, keepdims=True)
    self32 = jnp.where(sel, 1.0, 0.0)
    cut = jnp.zeros((N_KEYS, n), F32)
    r0 = 0
    for k1, nrow in _CAND_GROUPS:
        cnt = jnp.sum(self32[r0:r0 + nrow, :], axis=0, keepdims=True)
        cut = jnp.where(r1 == k1, cnt, cut)
        r0 += nrow
    for j in range(8):
        cut = jnp.where(r1 == 8 + j, self32[r0 + j:r0 + j + 1, :], cut)
    e1 = jnp.exp(s1 - t1[0:1, :]) * (1.0 / z)
    e2 = jnp.exp(s2 - t2[0:1, :])
    return cut, e1, e2


def _psel_kernel(x_ref, nw_ref, sh_ref, sc_ref, wq_ref, sk_ref,
                 h_ref, r2_ref, e2_ref, cut_ref, e1_ref, q_scr):
    tq = x_ref.shape[0]
    n_chunks = tq // LANE
    hb = _modnorm(x_ref[...], nw_ref[...], sh_ref[0, 0], sc_ref[0, 0]).astype(BF16)
    h_ref[...] = hb
    q_scr[...] = _dot(wq_ref[...], hb, NT)
    half = D_KEY // 2

    def scores(hh):
        q1 = q_scr[pl.ds(pl.multiple_of(hh * D_KEY, D_KEY), half), :]
        q2 = q_scr[pl.ds(pl.multiple_of(hh * D_KEY + half, half), half), :]
        return _dot3(sk_ref[0], q1), _dot3(sk_ref[1], q2)

    def write(hh, ls, r2, cut, e1, e2):
        r2_ref[hh, :, ls] = r2.astype(BF16)
        e2_ref[hh, :, ls] = e2.astype(BF16)
        cut_ref[hh, :, ls] = cut
        e1_ref[hh, :, ls] = e1

    def exact_chunk(hh, ls, s1c, s2c):
        r1, t1 = _topk16(s1c)
        r2, t2 = _topk16(s2c)
        cand = _cand_scores(t1, t2)
        crank, _ = _topk16(cand)
        cut, e1, e2 = _psel_finish(s1c, s2c, r1.astype(F32), t1, t2, cand, crank < PEER_TOPK)
        write(hh, ls, r2.astype(F32), cut, e1, e2)

    def head(hh, carry):
        s1, s2 = scores(hh)
        halves = []
        for lc in range(n_chunks):
            ls = slice(lc * LANE, (lc + 1) * LANE)
            halves.append(_topk16_untied([s1[:, ls], s2[:, ls]]))
        cands = [_cand_scores(h1[1], h2[1]) for h1, h2 in halves]
        picked = _topk16_untied(cands)
        for lc in range(n_chunks):
            ls = slice(lc * LANE, (lc + 1) * LANE)
            (r1, t1, c1), (r2, t2, c2) = halves[lc]
            crank, _, cc = picked[lc]
            cut, e1, e2 = _psel_finish(s1[:, ls], s2[:, ls], r1, t1, t2, cands[lc], crank < PEER_TOPK)
            write(hh, ls, r2, cut, e1, e2)
            extra = c1 + c2 + cc

            @pl.when(jnp.max(extra) > 0)
            def _():
                exact_chunk(hh, ls, s1[:, ls], s2[:, ls])
        return carry

    lax.fori_loop(0, PEER_HEADS, head, 0)


def _psel_call(x, nw, mod4, layer, rows, wqt_bf, sub_keys, tq=256):
    t, d = x.shape
    nq = wqt_bf.shape[0]
    sel_shape = jax.ShapeDtypeStruct((PEER_HEADS, N_KEYS, t), F32)
    sel_bf = jax.ShapeDtypeStruct((PEER_HEADS, N_KEYS, t), BF16)
    sel_spec = pl.BlockSpec((PEER_HEADS, N_KEYS, tq), lambda i: (0, 0, i))
    return pl.pallas_call(
        _psel_kernel,
        out_shape=[jax.ShapeDtypeStruct((t, d), BF16), sel_bf, sel_bf, sel_shape, sel_shape],
        grid=(t // tq,),
        in_specs=[
            pl.BlockSpec((tq, d), lambda i: (i, 0)),
            pl.BlockSpec((1, d), lambda i: (0, 0)),
            rows.mod_spec(layer, 3, tq),
            rows.mod_spec(layer, 4, tq),
            pl.BlockSpec((nq, d), lambda i: (0, 0)),
            pl.BlockSpec((2, N_KEYS, D_KEY // 2), lambda i: (0, 0, 0)),
        ],
        out_specs=[pl.BlockSpec((tq, d), lambda i: (i, 0)), sel_spec, sel_spec, sel_spec, sel_spec],
        scratch_shapes=[pltpu.VMEM((nq, tq), F32)],
        compiler_params=_cparams(1),
        name="peer_select",
    )(x, nw.reshape(1, d), mod4, mod4, wqt_bf, sub_keys)


def _gelu_sigmoid_form(x):
    k = 2.0 * math.sqrt(2.0 / math.pi) * math.log2(math.e)
    u = x * (-k - (k * 0.044715) * (x * x))
    return x * (1.0 / (1.0 + jnp.exp2(u)))


def _pdense_kernel(h_ref, u_ref, v_ref, r2_ref, e2_ref, cut_ref, e1_ref, x_ref, g_ref, o_ref,
                   hid0, hid1, w0, w1, acc_scr, *, nj, n_real):
    s = pl.program_id(0)
    s3 = jnp.clip(s - 2, 0, n_real - 1)
    j3 = s3 % nj
    te, tq = hid0.shape
    bzero = jnp.zeros((), BF16)

    @pl.when(s == 0)
    def _():
        for r in (hid0, hid1, w0, w1):
            r[...] = jnp.zeros(r.shape, r.dtype)

    @pl.when(j3 == 0)
    def _():
        acc_scr[...] = jnp.zeros(acc_scr.shape, F32)

    def step(hid_new, hid_old, w_new, w_old):
        for a in range(te // N_KEYS):
            rs = slice(a * N_KEYS, (a + 1) * N_KEYS)
            if a % 2 == 0:
                rs2 = slice(a * N_KEYS, (a + 2) * N_KEYS)
                hid_new[rs2, :] = _dot(u_ref[rs2, :], h_ref[...], NT)
            for lc in range(tq // (2 * LANE)):
                ls = slice(lc * 2 * LANE, (lc + 1) * 2 * LANE)
                gate = None
                for hh in range(PEER_HEADS):
                    cut = jnp.broadcast_to(cut_ref[hh, a:a + 1, ls], (N_KEYS, 2 * LANE)).astype(BF16)
                    e1 = jnp.broadcast_to(e1_ref[hh, a:a + 1, ls], (N_KEYS, 2 * LANE)).astype(BF16)
                    term = jnp.where(r2_ref[hh, :, ls] < cut, e2_ref[hh, :, ls] * e1, bzero)
                    gate = term if gate is None else gate + term
                w_new[rs, ls] = _gelu_sigmoid_form(hid_old[rs, ls]).astype(BF16) * gate
            if a % 2 == 1:
                acc_scr[...] += _dot(w_old[rs2, :], v_ref[rs2, :], TN)

    @pl.when(s % 2 == 0)
    def _():
        step(hid0, hid1, w0, w1)

    @pl.when(s % 2 == 1)
    def _():
        step(hid1, hid0, w1, w0)

    @pl.when(jnp.logical_and(j3 == nj - 1, s >= 2))
    def _():
        o_ref[...] = x_ref[...] + g_ref[0, 0] * acc_scr[...]


def _pdense_call(h_bf, u_bf, v_bf, r2, e2, cut, e1, x, mod4, layer, rows, tq=512, te=2048):
    t, d = x.shape
    ne = u_bf.shape[0]
    ni, nj = t // tq, ne // te
    n_real = ni * nj
    na = te // N_KEYS

    def tile(shift):
        def f(s):
            ss = jnp.clip(s - shift, 0, n_real - 1)
            return ss // nj, ss % nj
        return f

    t1, t2, t3 = tile(0), tile(1), tile(2)
    return pl.pallas_call(
        functools.partial(_pdense_kernel, nj=nj, n_real=n_real),
        out_shape=jax.ShapeDtypeStruct((t, d), F32),
        grid=(n_real + 2,),
        in_specs=[
            pl.BlockSpec((tq, d), lambda s: (t1(s)[0], 0)),
            pl.BlockSpec((te, d), lambda s: (t1(s)[1], 0)),
            pl.BlockSpec((te, d), lambda s: (t3(s)[1], 0)),
            pl.BlockSpec((PEER_HEADS, N_KEYS, tq), lambda s: (0, 0, t2(s)[0])),
            pl.BlockSpec((PEER_HEADS, N_KEYS, tq), lambda s: (0, 0, t2(s)[0])),
            pl.BlockSpec((PEER_HEADS, na, tq), lambda s: (0, t2(s)[1], t2(s)[0])),
            pl.BlockSpec((PEER_HEADS, na, tq), lambda s: (0, t2(s)[1], t2(s)[0])),
            pl.BlockSpec((tq, d), lambda s: (t3(s)[0], 0)),
            rows.mod_spec(layer, 5, tq, tile_of=lambda s: t3(s)[0]),
        ],
        out_specs=pl.BlockSpec((tq, d), lambda s: (t3(s)[0], 0)),
        scratch_shapes=[pltpu.VMEM((te, tq), F32), pltpu.VMEM((te, tq), F32),
                        pltpu.VMEM((te, tq), BF16), pltpu.VMEM((te, tq), BF16),
                        pltpu.VMEM((tq, d), F32)],
        compiler_params=_cparams(1),
        name="peer_dense",
    )(h_bf, u_bf, v_bf, r2, e2, cut, e1, x, mod4)


def _final_kernel(x_ref, w_ref, o_ref):
    o_ref[...] = _rms(x_ref[...]) * w_ref[...]


def _final_call(x, w, row_block0, n_rows, tm=512):
    d = x.shape[1]
    return pl.pallas_call(
        _final_kernel,
        out_shape=jax.ShapeDtypeStruct((n_rows, d), F32),
        grid=(n_rows // tm,),
        in_specs=[pl.BlockSpec((tm, d), lambda i: (row_block0 + i, 0)),
                  pl.BlockSpec((1, d), lambda i: (0, 0))],
        out_specs=pl.BlockSpec((tm, d), lambda i: (i, 0)),
        compiler_params=_cparams(1),
        name="final_norm",
    )(x, w.reshape(1, d))


def _grid_pos_embed(rows, dtype):
    t = jnp.arange(rows * GRID_W)
    r = (t // GRID_W).astype(F32)
    col = (t % GRID_W).astype(F32)
    nf = D_MODEL // 4
    freqs = 1.0 / (10000.0 ** (jnp.arange(nf, dtype=F32) / nf))

    def emb(p):
        a = p[:, None] * freqs[None, :]
        return jnp.concatenate([jnp.sin(a), jnp.cos(a)], axis=-1)
    return jnp.concatenate([emb(r), emb(col)], axis=-1).astype(dtype)


def _even_w_in_layout(w):
    d = w.shape[0]
    a_end = 4 * W_A
    lr = w[:, a_end:a_end + GATE_RANK]
    uv = w[:, a_end + GATE_RANK:]
    pad = jnp.zeros((d, LANE - GATE_RANK), w.dtype)
    return jnp.concatenate([w[:, :a_end], uv, lr, pad], axis=1)


def _pad_gate_w(w):
    return jnp.concatenate([w, jnp.zeros((LANE - GATE_RANK, w.shape[1]), w.dtype)], axis=0)


def kernel(x_prompt, x_sample, state_gla, c, c_ctx, w_mod, b_mod, norm_w, final_norm_w, even_w_in, even_w_gate_f, even_b_gate_f, even_w_gate_b, even_b_gate_b, gla_norm_w, sgu_norm_w, sgu_w_s, sgu_b_s, even_w_out, odd_w_in, odd_conv_w, odd_w_out, peer_w_q, peer_sub_keys, peer_u, peer_v):
    bp, lp, d = x_prompt.shape
    bs, ls, _ = x_sample.shape
    depth = w_mod.shape[0]
    n_even = even_w_in.shape[0]
    tp = bp * lp
    ts = bs * ls
    rows = _Rows(tp, ls)

    xs = x_sample + _grid_pos_embed(ls // GRID_W, x_sample.dtype)[None]
    x = jnp.concatenate([x_prompt.reshape(tp, d), xs.reshape(ts, d)], axis=0)

    cc = jnp.concatenate([c_ctx[None], c, jnp.zeros((MOD_ROWS - 1 - bs, d), F32)], axis=0)
    mod4 = _mod_call(cc, w_mod, b_mod).reshape(depth, MOD_ROWS, 1, 6 * d)

    cd, sd = _dft_mats(DD)
    clp, slp = _dft_mats(lp)
    cls, sls = _dft_mats(ls)
    s0_all = state_gla.reshape(bs, n_even, 2 * H_A, DK_A, DV_A)

    new_states = []
    for l in range(depth):
        if l % 2 == 0:
            e = l // 2
            z = _normproj_call(x, norm_w[l, 0], mod4, l, rows, _even_w_in_layout(even_w_in[e]).astype(BF16))
            gate_args = (_pad_gate_w(even_w_gate_f[e]), even_b_gate_f[e][None],
                         _pad_gate_w(even_w_gate_b[e]), even_b_gate_b[e][None], gla_norm_w[e][None])
            o_p, sf, sb = _gla_call(z, lp, bp, 0, *gate_args, want_state=True)
            (o_s,) = _gla_call(z, ls, bs, tp // ls, *gate_args,
                               s0=s0_all[:, e].reshape(bs * 2 * H_A, DK_A, DV_A))
            a1 = jnp.concatenate([o_p, o_s], axis=0)
            bs_full = jnp.broadcast_to(sgu_b_s[e][:, :, None], (H_B, CHUNK_B, DB))
            a2 = _sgu_call(z, sgu_norm_w[e][None], sgu_w_s[e].astype(BF16), bs_full)
            x = _outproj_call(a1, a2, even_w_out[e].astype(BF16), x, mod4, l, rows)
            sfb = jnp.stack([sf.reshape(bp, H_A, DK_A, DV_A), sb.reshape(bp, H_A, DK_A, DV_A)], axis=1)
            new_states.append(sfb.astype(x_prompt.dtype))
        else:
            o = l // 2
            z = _normproj_call(x, norm_w[l, 0], mod4, l, rows, odd_w_in[o].astype(BF16))
            yc = jnp.concatenate([_conv_call(z, lp, bp, 0, odd_conv_w[o]),
                                  _conv_call(z, ls, bs, tp // ls, odd_conv_w[o])], axis=0)
            yd = jnp.concatenate([_fft_call(z, lp, bp, 0, cd, sd, clp, slp),
                                  _fft_call(z, ls, bs, tp // ls, cd, sd, cls, sls)], axis=0)
            x = _outproj_call(yc, yd, odd_w_out[o].astype(BF16), x, mod4, l, rows)
        h_bf, r2, e2, cut, e1 = _psel_call(x, norm_w[l, 1], mod4, l, rows,
                                           peer_w_q[l].T.astype(BF16), peer_sub_keys[l])
        x = _pdense_call(h_bf, peer_u[l].astype(BF16), peer_v[l].astype(BF16),
                         r2, e2, cut, e1, x, mod4, l, rows)

    y_prompt = _final_call(x, final_norm_w, 0, tp).reshape(bp, lp, d)
    y_sample = _final_call(x, final_norm_w, tp // 512, ts).reshape(bs, ls, d)
    state_new = jnp.stack(new_states, axis=1)
    return (y_prompt, y_sample, state_new)
```

```python
import functools
import math

import jax
import jax.numpy as jnp
import numpy as np
from jax import lax
from jax.experimental import pallas as pl
from jax.experimental.pallas import tpu as pltpu

F32 = jnp.float32
BF16 = jnp.bfloat16

D_MODEL = 1024
DEPTH = 4
GRID_W = 64
RMS_EPS = 1e-6
H_A = 4
DK_A = 128
DV_A = 128
GATE_RANK = 16
GATE_NORM = 16.0
GLA_CHUNK = 64
W_A = H_A * DV_A
H_B = 4
DB = 128
CHUNK_B = 128
W_B = H_B * DB
H_C = 4
DC = 128
W_C = H_C * DC
H_D = 4
DD = 128
W_D = H_D * DD
N_KEYS = 128
N_EXPERTS = N_KEYS * N_KEYS
PEER_HEADS = 8
PEER_TOPK = 16
D_KEY = 256

LANE = 128
MOD_ROWS = 8
EVEN_IN_PAD = 4 * W_A + 2 * W_B + LANE
VMEM_LIMIT = 48 * 1024 * 1024

NN = (((1,), (0,)), ((), ()))
NT = (((1,), (1,)), ((), ()))
TN = (((0,), (0,)), ((), ()))


def _cparams(n_axes, flags=None):
    return pltpu.CompilerParams(dimension_semantics=("arbitrary",) * n_axes,
                                vmem_limit_bytes=VMEM_LIMIT, flags=flags)


def _dot(a, b, dims=NN):
    return lax.dot_general(a, b, dims, preferred_element_type=F32)


def _split2(x):
    hi = x.astype(BF16)
    lo = (x - hi.astype(F32)).astype(BF16)
    return hi, lo


def _split3(x):
    hi = x.astype(BF16)
    r = x - hi.astype(F32)
    mid = r.astype(BF16)
    lo = (r - mid.astype(F32)).astype(BF16)
    return hi, mid, lo


def _dot3(a, b, dims=NN):
    ah, al = _split2(a)
    bh, bl = _split2(b)
    return _dot(ah, bh, dims) + _dot(al, bh, dims) + _dot(ah, bl, dims)


def _silu(x):
    return x * (1.0 / (1.0 + jnp.exp(-x)))


def _gelu(x):
    c = math.sqrt(2.0 / math.pi)
    return x * (0.5 * (1.0 + jnp.tanh(c * (x + 0.044715 * (x * x * x)))))


def _log_sigmoid(x):
    return jnp.minimum(x, 0.0) - jnp.log(1.0 + jnp.exp(-jnp.abs(x)))


def _rms(x):
    return x * lax.rsqrt(jnp.mean(x * x, axis=-1, keepdims=True) + RMS_EPS)


def _modnorm(x, nw, shift, scale):
    return (_rms(x) * nw) * (1.0 + scale) + shift


def _mod_kernel(cc_ref, w_ref, b_ref, o_ref):
    a = _silu(cc_ref[...])
    o_ref[0] = _dot3(a, w_ref[0]) + b_ref[0]


def _mod_call(cc, w_mod, b_mod):
    depth, d, n = w_mod.shape
    tn = 1024
    return pl.pallas_call(
        _mod_kernel,
        out_shape=jax.ShapeDtypeStruct((depth, MOD_ROWS, n), F32),
        grid=(depth, n // tn),
        in_specs=[
            pl.BlockSpec((MOD_ROWS, d), lambda l, j: (0, 0)),
            pl.BlockSpec((1, d, tn), lambda l, j: (l, 0, j)),
            pl.BlockSpec((1, 1, tn), lambda l, j: (l, 0, j)),
        ],
        out_specs=pl.BlockSpec((1, MOD_ROWS, tn), lambda l, j: (l, 0, j)),
        compiler_params=_cparams(2),
        name="mod_rows",
    )(cc, w_mod, b_mod.reshape(depth, 1, n))


class _Rows:
    def __init__(self, n_prompt_rows, dec_seq):
        self.n_prompt_rows = n_prompt_rows
        self.dec_seq = dec_seq

    def mod_row(self, i, tm):
        npt = self.n_prompt_rows // tm
        tps = self.dec_seq // tm
        return jnp.where(i < npt, 0, 1 + (i - npt) // tps)

    def mod_spec(self, layer, section, tm, tile_of=lambda i, *_: i):
        return pl.BlockSpec((1, 1, 1, D_MODEL),
                            lambda *g: (layer, self.mod_row(tile_of(*g), tm), 0, section))


def _normproj_kernel(x_ref, nw_ref, sh_ref, sc_ref, w_ref, o_ref):
    h = _modnorm(x_ref[...], nw_ref[...], sh_ref[0, 0], sc_ref[0, 0])
    o_ref[...] = _dot(h.astype(BF16), w_ref[...])


def _normproj_call(x, nw, mod4, layer, rows, w_bf, tm=256):
    t, d = x.shape
    n = w_bf.shape[1]
    return pl.pallas_call(
        _normproj_kernel,
        out_shape=jax.ShapeDtypeStruct((t, n), F32),
        grid=(t // tm,),
        in_specs=[
            pl.BlockSpec((tm, d), lambda i: (i, 0)),
            pl.BlockSpec((1, d), lambda i: (0, 0)),
            rows.mod_spec(layer, 0, tm),
            rows.mod_spec(layer, 1, tm),
            pl.BlockSpec((d, n), lambda i: (0, 0)),
        ],
        out_specs=pl.BlockSpec((tm, n), lambda i: (i, 0)),
        compiler_params=_cparams(1),
        name="normproj",
    )(x, nw.reshape(1, d), mod4, mod4, w_bf)


def _gla_kernel(*refs, has_s0, want_state):
    (q_ref, k_ref, v_ref, g_ref, lr_ref, wgf_ref, bgf_ref, wgb_ref, bgb_ref, nw_ref) = refs[:10]
    pos = 10
    if has_s0:
        s0f_ref, s0b_ref = refs[pos:pos + 2]
        pos += 2
    o_ref = refs[pos]
    pos += 1
    if want_state:
        sf_ref, sb_ref = refs[pos:pos + 2]
        pos += 2
    laf_scr, lab_scr, of_scr, ob_scr, s_scr = refs[pos:]

    seq = q_ref.shape[0]
    c = GLA_CHUNK
    per_group = 4
    gw = per_group * c
    n_groups = seq // gw
    scale = DK_A ** -0.5

    lr = lr_ref[...]
    laf_scr[...] = _log_sigmoid(_dot3(lr, wgf_ref[...]) + bgf_ref[...]) * (1.0 / GATE_NORM)
    lab_scr[...] = _log_sigmoid(_dot3(lr, wgb_ref[...]) + bgb_ref[...]) * (1.0 / GATE_NORM)
    if has_s0:
        s_scr[0] = s0f_ref[0].T
        s_scr[1] = s0b_ref[0].T
    else:
        s_scr[...] = jnp.zeros(s_scr.shape, F32)

    ri = lax.broadcasted_iota(jnp.int32, (gw, gw), 0)
    ci = lax.broadcasted_iota(jnp.int32, (gw, gw), 1)
    same = (ri // c) == (ci // c)
    lower = jnp.logical_and(same, ci <= ri)
    upper = jnp.logical_and(same, ci >= ri)
    tril = jnp.where(lower, 1.0, 0.0).astype(BF16)
    triu = jnp.where(upper, 1.0, 0.0).astype(BF16)
    ones = jnp.where(same, 1.0, 0.0).astype(BF16)

    def group(r0, la_scr, d, tri, mask, order, out_scr):
        rs = pl.ds(r0, gw)
        la_h, la_m, la_l = _split3(la_scr[rs, :])
        b = _dot(tri, la_h) + _dot(tri, la_m) + _dot(tri, la_l)
        be = _dot(ones, la_h) + _dot(ones, la_m) + _dot(ones, la_l)
        q = q_ref[rs, :] * scale
        k = k_ref[rs, :]
        v = v_ref[rs, :].astype(BF16)
        qd = (q * jnp.exp(b)).astype(BF16)
        ki = (k * jnp.exp(-b)).astype(BF16)
        ks = (k * jnp.exp(be - b)).astype(BF16)
        att = jnp.where(mask, _dot(qd, ki, NT), 0.0)
        o_intra = _dot(att.astype(BF16), v)
        dec = jnp.exp(be)
        st = s_scr[d]
        inter = [None] * per_group
        for ch in order:
            cs = slice(ch * c, (ch + 1) * c)
            inter[ch] = _dot(qd[cs, :], st.astype(BF16), NT)
            st = st * dec[ch * c:ch * c + 1, :] + _dot(v[cs, :], ks[cs, :], TN)
        s_scr[d] = st
        out_scr[rs, :] = o_intra + jnp.concatenate(inter, axis=0)

    fwd_order = tuple(range(per_group))
    bwd_order = tuple(reversed(fwd_order))

    def body(i, carry):
        group(pl.multiple_of(i * gw, gw), laf_scr, 0, tril, lower, fwd_order, of_scr)
        group(pl.multiple_of((n_groups - 1 - i) * gw, gw), lab_scr, 1, triu, upper, bwd_order, ob_scr)
        return carry

    lax.fori_loop(0, n_groups, body, 0)

    o = _rms(of_scr[...] + ob_scr[...]) * nw_ref[...]
    o_ref[...] = (o * _silu(g_ref[...])).astype(o_ref.dtype)
    if want_state:
        sf_ref[0] = s_scr[0].T
        sb_ref[0] = s_scr[1].T


def _gla_call(z, seq, n_seq, row_block0, wgf, bgf, wgb, bgb, nw, s0=None, want_state=False):
    has_s0 = s0 is not None

    def col(cb):
        return pl.BlockSpec((seq, LANE), lambda s, h: (row_block0 + s, cb + h))

    in_specs = [col(0), col(H_A), col(2 * H_A), col(3 * H_A),
                pl.BlockSpec((seq, LANE), lambda s, h: (row_block0 + s, (4 * W_A + 2 * W_B) // LANE)),
                pl.BlockSpec((LANE, DK_A), lambda s, h: (0, h)),
                pl.BlockSpec((1, DK_A), lambda s, h: (0, h)),
                pl.BlockSpec((LANE, DK_A), lambda s, h: (0, h)),
                pl.BlockSpec((1, DK_A), lambda s, h: (0, h)),
                pl.BlockSpec((1, DV_A), lambda s, h: (0, 0))]
    args = [z, z, z, z, z, wgf, bgf, wgb, bgb, nw]
    if has_s0:
        in_specs += [pl.BlockSpec((1, DK_A, DV_A), lambda s, h: ((s * 2 + 0) * H_A + h, 0, 0)),
                     pl.BlockSpec((1, DK_A, DV_A), lambda s, h: ((s * 2 + 1) * H_A + h, 0, 0))]
        args += [s0, s0]
    out_shape = [jax.ShapeDtypeStruct((n_seq * seq, W_A), BF16)]
    out_specs = [pl.BlockSpec((seq, LANE), lambda s, h: (s, h))]
    if want_state:
        out_shape += [jax.ShapeDtypeStruct((n_seq * H_A, DK_A, DV_A), F32)] * 2
        out_specs += [pl.BlockSpec((1, DK_A, DV_A), lambda s, h: (s * H_A + h, 0, 0))] * 2
    return pl.pallas_call(
        functools.partial(_gla_kernel, has_s0=has_s0, want_state=want_state),
        out_shape=out_shape,
        grid=(n_seq, H_A),
        in_specs=in_specs,
        out_specs=out_specs,
        scratch_shapes=[pltpu.VMEM((seq, DK_A), F32), pltpu.VMEM((seq, DK_A), F32),
                        pltpu.VMEM((seq, DV_A), F32), pltpu.VMEM((seq, DV_A), F32),
                        pltpu.VMEM((2, DK_A, DV_A), F32)],
        compiler_params=_cparams(2),
        name="gla_seq%d" % seq,
    )(*args)


def _sgu_kernel(u_ref, v_ref, nw_ref, ws_ref, bs_ref, o_ref):
    tm = u_ref.shape[0]
    u = _gelu(u_ref[...])
    v = (_rms(_gelu(v_ref[...])) * nw_ref[...]).astype(BF16)
    ws = ws_ref[0]
    bs = bs_ref[0]
    for cidx in range(tm // CHUNK_B):
        rs = slice(cidx * CHUNK_B, (cidx + 1) * CHUNK_B)
        sg = _dot(ws, v[rs, :]) + bs
        o_ref[rs, :] = (u[rs, :] * sg).astype(o_ref.dtype)


def _sgu_call(z, nw, ws_bf, bs_full, tm=512):
    t = z.shape[0]
    ub0 = (4 * W_A) // LANE
    vb0 = (4 * W_A + W_B) // LANE
    return pl.pallas_call(
        _sgu_kernel,
        out_shape=jax.ShapeDtypeStruct((t, W_B), BF16),
        grid=(t // tm, H_B),
        in_specs=[
            pl.BlockSpec((tm, LANE), lambda i, h: (i, ub0 + h)),
            pl.BlockSpec((tm, LANE), lambda i, h: (i, vb0 + h)),
            pl.BlockSpec((1, DB), lambda i, h: (0, 0)),
            pl.BlockSpec((1, CHUNK_B, CHUNK_B), lambda i, h: (h, 0, 0)),
            pl.BlockSpec((1, CHUNK_B, DB), lambda i, h: (h, 0, 0)),
        ],
        out_specs=pl.BlockSpec((tm, LANE), lambda i, h: (i, h)),
        compiler_params=_cparams(2),
        name="sgu",
    )(z, z, nw, ws_bf, bs_full)


def _outproj_kernel(a1p_ref, a1s_ref, a2p_ref, a2s_ref, w_ref, x_ref, g_ref, o_ref, *, n_prompt_tiles):
    n1 = a1p_ref.shape[1]

    def run(a1_ref, a2_ref):
        y = _dot(a1_ref[...], w_ref[:n1, :]) + _dot(a2_ref[...], w_ref[n1:, :])
        o_ref[...] = x_ref[...] + g_ref[0, 0] * y

    @pl.when(pl.program_id(0) < n_prompt_tiles)
    def _():
        run(a1p_ref, a2p_ref)

    @pl.when(pl.program_id(0) >= n_prompt_tiles)
    def _():
        run(a1s_ref, a2s_ref)


def _outproj_call(a1, a2, w_bf, x, mod4, layer, rows, tm=512):
    t, d = x.shape
    npt = rows.n_prompt_rows // tm

    def parts(a):
        if isinstance(a, tuple):
            n = a[0].shape[1]
            return (a[0], a[1],
                    pl.BlockSpec((tm, n), lambda i: (jnp.minimum(i, npt - 1), 0)),
                    pl.BlockSpec((tm, n), lambda i: (jnp.maximum(i - npt, 0), 0)))
        spec = pl.BlockSpec((tm, a.shape[1]), lambda i: (i, 0))
        return a, a, spec, spec

    a1p, a1s, s1p, s1s = parts(a1)
    a2p, a2s, s2p, s2s = parts(a2)
    n1, n2 = a1p.shape[1], a2p.shape[1]
    return pl.pallas_call(
        functools.partial(_outproj_kernel, n_prompt_tiles=npt),
        out_shape=jax.ShapeDtypeStruct((t, d), F32),
        grid=(t // tm,),
        in_specs=[
            s1p, s1s, s2p, s2s,
            pl.BlockSpec((n1 + n2, d), lambda i: (0, 0)),
            pl.BlockSpec((tm, d), lambda i: (i, 0)),
            rows.mod_spec(layer, 2, tm),
        ],
        out_specs=pl.BlockSpec((tm, d), lambda i: (i, 0)),
        compiler_params=_cparams(1),
        name="outproj",
    )(a1p, a1s, a2p, a2s, w_bf, x, mod4)


def _embed_kernel(xp_ref, xs_ref, pe_ref, o_ref, *, n_prompt_tiles):
    @pl.when(pl.program_id(0) < n_prompt_tiles)
    def _():
        o_ref[...] = xp_ref[...]

    @pl.when(pl.program_id(0) >= n_prompt_tiles)
    def _():
        o_ref[...] = xs_ref[...] + pe_ref[...]


def _embed_call(xp, xs, pe, tm=512):
    tp, d = xp.shape
    ts = xs.shape[0]
    npt = tp // tm
    pe_tiles = pe.shape[0] // tm
    return pl.pallas_call(
        functools.partial(_embed_kernel, n_prompt_tiles=npt),
        out_shape=jax.ShapeDtypeStruct((tp + ts, d), F32),
        grid=((tp + ts) // tm,),
        in_specs=[
            pl.BlockSpec((tm, d), lambda i: (jnp.minimum(i, npt - 1), 0)),
            pl.BlockSpec((tm, d), lambda i: (jnp.maximum(i - npt, 0), 0)),
            pl.BlockSpec((tm, d), lambda i: (jnp.maximum(i - npt, 0) % pe_tiles, 0)),
        ],
        out_specs=pl.BlockSpec((tm, d), lambda i: (i, 0)),
        compiler_params=_cparams(1),
        name="embed",
    )(xp, xs, pe)


def _conv_kernel(bg_ref, cg_ref, xi_ref, w_ref, o_ref):
    seq = bg_ref.shape[0]
    t = cg_ref[...] * xi_ref[...]
    row = lax.broadcasted_iota(jnp.int32, t.shape, 0)
    t_prev = jnp.where(row == 0, 0.0, pltpu.roll(t, 1, 0))
    t_next = jnp.where(row == seq - 1, 0.0, pltpu.roll(t, seq - 1, 0))
    conv = t_prev * w_ref[0:1, :] + t * w_ref[1:2, :] + t_next * w_ref[2:3, :]
    o_ref[...] = (bg_ref[...] * conv).astype(o_ref.dtype)


def _conv_call(z, seq, n_seq, row_block0, conv_w):
    def col(cb):
        return pl.BlockSpec((seq, LANE), lambda s, j: (row_block0 + s, cb + j))
    return pl.pallas_call(
        _conv_kernel,
        out_shape=jax.ShapeDtypeStruct((n_seq * seq, W_C), BF16),
        grid=(n_seq, W_C // LANE),
        in_specs=[col(0), col(H_C), col(2 * H_C),
                  pl.BlockSpec((3, LANE), lambda s, j: (0, j))],
        out_specs=pl.BlockSpec((seq, LANE), lambda s, j: (s, j)),
        compiler_params=_cparams(2),
        name="conv_seq%d" % seq,
    )(z, z, z, conv_w)


def _fft_kernel(f_ref, cd_ref, sd_ref, cl_ref, sl_ref, o_ref, p_scr, q_scr, *, scale):
    @pl.when(pl.program_id(1) == 0)
    def _():
        cd = cd_ref[...]
        sd = sd_ref[...]
        for h in range(H_D):
            cs = slice(h * DD, (h + 1) * DD)
            fh = f_ref[:, cs].astype(BF16)
            p_scr[:, cs] = _dot(fh, cd).astype(BF16)
            q_scr[:, cs] = _dot(fh, sd).astype(BF16)

    y = _dot(cl_ref[...], p_scr[...]) - _dot(sl_ref[...], q_scr[...])
    o_ref[...] = (y * scale).astype(o_ref.dtype)


def _dft_mats(n):
    idx = jnp.arange(n, dtype=jnp.int32)
    ang = ((idx[:, None] * idx[None, :]) % n).astype(F32) * (2.0 * math.pi / n)
    return jnp.cos(ang).astype(BF16), jnp.sin(ang).astype(BF16)


def _fft_call(z, seq, n_seq, row_block0, cd, sd, cl, sl):
    tl = min(seq, 512)
    fcol = (3 * W_C) // W_D
    return pl.pallas_call(
        functools.partial(_fft_kernel, scale=1.0 / math.sqrt(seq * DD)),
        out_shape=jax.ShapeDtypeStruct((n_seq * seq, W_D), BF16),
        grid=(n_seq, seq // tl),
        in_specs=[
            pl.BlockSpec((seq, W_D), lambda s, i: (row_block0 + s, fcol)),
            pl.BlockSpec((DD, DD), lambda s, i: (0, 0)),
            pl.BlockSpec((DD, DD), lambda s, i: (0, 0)),
            pl.BlockSpec((tl, seq), lambda s, i: (i, 0)),
            pl.BlockSpec((tl, seq), lambda s, i: (i, 0)),
        ],
        out_specs=pl.BlockSpec((tl, W_D), lambda s, i: (s * (seq // tl) + i, 0)),
        scratch_shapes=[pltpu.VMEM((seq, W_D), BF16), pltpu.VMEM((seq, W_D), BF16)],
        compiler_params=_cparams(2),
        name="fft_seq%d" % seq,
    )(z, cd, sd, cl, sl)


def _topk16(s):
    nrows = s.shape[0]
    iota = lax.broadcasted_iota(jnp.int32, s.shape, 0)
    i16 = lax.broadcasted_iota(jnp.int32, (PEER_TOPK, s.shape[1]), 0)

    def body(k, carry):
        s, rank, top = carry
        m = jnp.max(s, axis=0, keepdims=True)
        idx = jnp.min(jnp.where(s == m, iota, nrows), axis=0, keepdims=True)
        sel = iota == idx
        rank = jnp.where(sel, k, rank)
        s = jnp.where(sel, -jnp.inf, s)
        top = jnp.where(i16 == k, m, top)
        return s, rank, top

    init = (s, jnp.full(s.shape, PEER_TOPK, jnp.int32), jnp.zeros((PEER_TOPK, s.shape[1]), F32))
    _, rank, top = lax.fori_loop(0, PEER_TOPK, body, init)
    return rank, top


_CAND_GROUPS = [(0, 16), (1, 8)] + [(k1, 8) for k1 in range(2, 8)]
_CAND_SINGLE0 = sum(n for _, n in _CAND_GROUPS)


_SCORE_FLOOR = -(2.0 ** 126)
_CODE_UNIT = 2.0 ** 127


def _rank_code(k):
    return -_CODE_UNIT * (1.0 + (k + 1) / 32.0)


def _topk16_untied(scores):
    n = scores[0].shape[1]
    i16 = lax.broadcasted_iota(jnp.int32, (PEER_TOPK, n), 0)
    keys = [jnp.maximum(s, _SCORE_FLOOR) for s in scores]
    tops = [jnp.zeros((PEER_TOPK, n), F32) for _ in scores]
    for k in range(PEER_TOPK):
        for c in range(len(keys)):
            m = jnp.max(keys[c], axis=0, keepdims=True)
            keys[c] = jnp.where(keys[c] == m, _rank_code(k), keys[c])
            tops[c] = jnp.where(i16 == k, m, tops[c])
    out = []
    for key, top in zip(keys, tops):
        taken = key < _SCORE_FLOOR
        rank = jnp.where(taken, key * -(32.0 / _CODE_UNIT) - 33.0, float(PEER_TOPK))
        count = jnp.sum(jnp.where(taken, 1.0, 0.0), axis=0, keepdims=True)
        clamped = jnp.sum(jnp.where(top <= _SCORE_FLOOR, 1.0, 0.0), axis=0, keepdims=True)
        out.append((rank, top, (count - PEER_TOPK) + clamped))
    return out


def _cand_scores(t1, t2):
    n = t1.shape[1]
    neg = jnp.full((8, n), -jnp.inf, F32)
    row8 = lax.broadcasted_iota(jnp.int32, (8, n), 0)
    pieces = []
    for k1, nrow in _CAND_GROUPS:
        blk = t1[k1:k1 + 1, :] + t2[0:nrow, :]
        nvalid = PEER_TOPK // (k1 + 1)
        if nvalid < nrow:
            blk = jnp.where(row8 < nvalid, blk, neg)
        pieces.append(blk)
    pieces.append(t1[8:16, :] + t2[0:1, :])
    return jnp.concatenate(pieces, axis=0)


def _psel_finish(s1, s2, r1, t1, t2, cand, sel):
    n = s1.shape[1]
    cmax = t1[0:1, :] + t2[0:1, :]
    z = jnp.sum(jnp.where(sel, jnp.exp(cand - cmax), 0.0), axis=0, keepdims=True)
    self32 = jnp.where(sel, 1.0, 0.0)
    cut = jnp.zeros((N_KEYS, n), F32)
    r0 = 0
    for k1, nrow in _CAND_GROUPS:
        cnt = jnp.sum(self32[r0:r0 + nrow, :], axis=0, keepdims=True)
        cut = jnp.where(r1 == k1, cnt, cut)
        r0 += nrow
    for j in range(8):
        cut = jnp.where(r1 == 8 + j, self32[r0 + j:r0 + j + 1, :], cut)
    e1 = jnp.exp(s1 - t1[0:1, :]) * (1.0 / z)
    e2 = jnp.exp(s2 - t2[0:1, :])
    return cut, e1, e2


def _psel_kernel(x_ref, nw_ref, sh_ref, sc_ref, wq_ref, sk_ref,
                 h_ref, r2_ref, e2_ref, cut_ref, e1_ref, q_scr):
    tq = x_ref.shape[0]
    n_chunks = tq // LANE
    hb = _modnorm(x_ref[...], nw_ref[...], sh_ref[0, 0], sc_ref[0, 0]).astype(BF16)
    h_ref[...] = hb
    q_scr[...] = _dot(wq_ref[...], hb, NT)
    half = D_KEY // 2

    def scores(hh):
        q1 = q_scr[pl.ds(pl.multiple_of(hh * D_KEY, D_KEY), half), :]
        q2 = q_scr[pl.ds(pl.multiple_of(hh * D_KEY + half, half), half), :]
        return _dot3(sk_ref[0], q1), _dot3(sk_ref[1], q2)

    def write(hh, ls, r2, cut, e1, e2):
        r2_ref[hh, :, ls] = r2.astype(BF16)
        e2_ref[hh, :, ls] = e2.astype(BF16)
        cut_ref[hh, :, ls] = cut
        e1_ref[hh, :, ls] = e1

    def exact_chunk(hh, ls, s1c, s2c):
        r1, t1 = _topk16(s1c)
        r2, t2 = _topk16(s2c)
        cand = _cand_scores(t1, t2)
        crank, _ = _topk16(cand)
        cut, e1, e2 = _psel_finish(s1c, s2c, r1.astype(F32), t1, t2, cand, crank < PEER_TOPK)
        write(hh, ls, r2.astype(F32), cut, e1, e2)

    def head(hh, carry):
        s1, s2 = scores(hh)
        halves = []
        for lc in range(n_chunks):
            ls = slice(lc * LANE, (lc + 1) * LANE)
            halves.append(_topk16_untied([s1[:, ls], s2[:, ls]]))
        cands = [_cand_scores(h1[1], h2[1]) for h1, h2 in halves]
        picked = _topk16_untied(cands)
        for lc in range(n_chunks):
            ls = slice(lc * LANE, (lc + 1) * LANE)
            (r1, t1, c1), (r2, t2, c2) = halves[lc]
            crank, _, cc = picked[lc]
            cut, e1, e2 = _psel_finish(s1[:, ls], s2[:, ls], r1, t1, t2, cands[lc], crank < PEER_TOPK)
            write(hh, ls, r2, cut, e1, e2)
            extra = c1 + c2 + cc

            @pl.when(jnp.max(extra) > 0)
            def _():
                exact_chunk(hh, ls, s1[:, ls], s2[:, ls])
        return carry

    lax.fori_loop(0, PEER_HEADS, head, 0)


def _psel_call(x, nw, mod4, layer, rows, wqt_bf, sub_keys, tq=256):
    t, d = x.shape
    nq = wqt_bf.shape[0]
    sel_shape = jax.ShapeDtypeStruct((PEER_HEADS, N_KEYS, t), F32)
    sel_bf = jax.ShapeDtypeStruct((PEER_HEADS, N_KEYS, t), BF16)
    sel_spec = pl.BlockSpec((PEER_HEADS, N_KEYS, tq), lambda i: (0, 0, i))
    return pl.pallas_call(
        _psel_kernel,
        out_shape=[jax.ShapeDtypeStruct((t, d), BF16), sel_bf, sel_bf, sel_shape, sel_shape],
        grid=(t // tq,),
        in_specs=[
            pl.BlockSpec((tq, d), lambda i: (i, 0)),
            pl.BlockSpec((1, d), lambda i: (0, 0)),
            rows.mod_spec(layer, 3, tq),
            rows.mod_spec(layer, 4, tq),
            pl.BlockSpec((nq, d), lambda i: (0, 0)),
            pl.BlockSpec((2, N_KEYS, D_KEY // 2), lambda i: (0, 0, 0)),
        ],
        out_specs=[pl.BlockSpec((tq, d), lambda i: (i, 0)), sel_spec, sel_spec, sel_spec, sel_spec],
        scratch_shapes=[pltpu.VMEM((nq, tq), F32)],
        compiler_params=_cparams(1),
        name="peer_select",
    )(x, nw.reshape(1, d), mod4, mod4, wqt_bf, sub_keys)


def _gelu_sigmoid_form(x):
    k = 2.0 * math.sqrt(2.0 / math.pi) * math.log2(math.e)
    u = x * (-k - (k * 0.044715) * (x * x))
    return x * (1.0 / (1.0 + jnp.exp2(u)))


def _pdense_kernel(h_ref, u_ref, v_ref, r2_ref, e2_ref, cut_ref, e1_ref, x_ref, g_ref, o_ref,
                   hid0, hid1, w0, w1, acc_scr, *, nj, n_real):
    s = pl.program_id(0)
    s3 = jnp.clip(s - 2, 0, n_real - 1)
    j3 = s3 % nj
    te, tq = hid0.shape
    bzero = jnp.zeros((), BF16)

    @pl.when(s == 0)
    def _():
        for r in (hid0, hid1, w0, w1):
            r[...] = jnp.zeros(r.shape, r.dtype)

    @pl.when(j3 == 0)
    def _():
        acc_scr[...] = jnp.zeros(acc_scr.shape, F32)

    def step(hid_new, hid_old, w_new, w_old):
        for a in range(te // N_KEYS):
            rs = slice(a * N_KEYS, (a + 1) * N_KEYS)
            if a % 2 == 0:
                rs2 = slice(a * N_KEYS, (a + 2) * N_KEYS)
                hid_new[rs2, :] = _dot(u_ref[rs2, :], h_ref[...], NT)
            for lc in range(tq // (2 * LANE)):
                ls = slice(lc * 2 * LANE, (lc + 1) * 2 * LANE)
                gate = None
                for hh in range(PEER_HEADS):
                    cut = jnp.broadcast_to(cut_ref[hh, a:a + 1, ls], (N_KEYS, 2 * LANE)).astype(BF16)
                    e1 = jnp.broadcast_to(e1_ref[hh, a:a + 1, ls], (N_KEYS, 2 * LANE)).astype(BF16)
                    term = jnp.where(r2_ref[hh, :, ls] < cut, e2_ref[hh, :, ls] * e1, bzero)
                    gate = term if gate is None else gate + term
                w_new[rs, ls] = _gelu_sigmoid_form(hid_old[rs, ls]).astype(BF16) * gate
            if a % 2 == 1:
                acc_scr[...] += _dot(w_old[rs2, :], v_ref[rs2, :], TN)

    @pl.when(s % 2 == 0)
    def _():
        step(hid0, hid1, w0, w1)

    @pl.when(s % 2 == 1)
    def _():
        step(hid1, hid0, w1, w0)

    @pl.when(jnp.logical_and(j3 == nj - 1, s >= 2))
    def _():
        o_ref[...] = x_ref[...] + g_ref[0, 0] * acc_scr[...]


def _pdense_call(h_bf, u_bf, v_bf, r2, e2, cut, e1, x, mod4, layer, rows, tq=512, te=2048):
    t, d = x.shape
    ne = u_bf.shape[0]
    ni, nj = t // tq, ne // te
    n_real = ni * nj
    na = te // N_KEYS

    def tile(shift):
        def f(s):
            ss = jnp.clip(s - shift, 0, n_real - 1)
            return ss // nj, ss % nj
        return f

    t1, t2, t3 = tile(0), tile(1), tile(2)
    return pl.pallas_call(
        functools.partial(_pdense_kernel, nj=nj, n_real=n_real),
        out_shape=jax.ShapeDtypeStruct((t, d), F32),
        grid=(n_real + 2,),
        in_specs=[
            pl.BlockSpec((tq, d), lambda s: (t1(s)[0], 0)),
            pl.BlockSpec((te, d), lambda s: (t1(s)[1], 0)),
            pl.BlockSpec((te, d), lambda s: (t3(s)[1], 0)),
            pl.BlockSpec((PEER_HEADS, N_KEYS, tq), lambda s: (0, 0, t2(s)[0])),
            pl.BlockSpec((PEER_HEADS, N_KEYS, tq), lambda s: (0, 0, t2(s)[0])),
            pl.BlockSpec((PEER_HEADS, na, tq), lambda s: (0, t2(s)[1], t2(s)[0])),
            pl.BlockSpec((PEER_HEADS, na, tq), lambda s: (0, t2(s)[1], t2(s)[0])),
            pl.BlockSpec((tq, d), lambda s: (t3(s)[0], 0)),
            rows.mod_spec(layer, 5, tq, tile_of=lambda s: t3(s)[0]),
        ],
        out_specs=pl.BlockSpec((tq, d), lambda s: (t3(s)[0], 0)),
        scratch_shapes=[pltpu.VMEM((te, tq), F32), pltpu.VMEM((te, tq), F32),
                        pltpu.VMEM((te, tq), BF16), pltpu.VMEM((te, tq), BF16),
                        pltpu.VMEM((tq, d), F32)],
        compiler_params=_cparams(1),
        name="peer_dense",
    )(h_bf, u_bf, v_bf, r2, e2, cut, e1, x, mod4)


def _final_kernel(x_ref, w_ref, o_ref):
    o_ref[...] = _rms(x_ref[...]) * w_ref[...]


def _final_call(x, w, row_block0, n_rows, tm=512):
    d = x.shape[1]
    return pl.pallas_call(
        _final_kernel,
        out_shape=jax.ShapeDtypeStruct((n_rows, d), F32),
        grid=(n_rows // tm,),
        in_specs=[pl.BlockSpec((tm, d), lambda i: (row_block0 + i, 0)),
                  pl.BlockSpec((1, d), lambda i: (0, 0))],
        out_specs=pl.BlockSpec((tm, d), lambda i: (i, 0)),
        compiler_params=_cparams(1),
        name="final_norm",
    )(x, w.reshape(1, d))


def _grid_pos_embed(rows, dtype):
    t = jnp.arange(rows * GRID_W)
    r = (t // GRID_W).astype(F32)
    col = (t % GRID_W).astype(F32)
    nf = D_MODEL // 4
    freqs = 1.0 / (10000.0 ** (jnp.arange(nf, dtype=F32) / nf))

    def emb(p):
        a = p[:, None] * freqs[None, :]
        return jnp.concatenate([jnp.sin(a), jnp.cos(a)], axis=-1)
    return jnp.concatenate([emb(r), emb(col)], axis=-1).astype(dtype)


def _even_w_in_layout(w):
    d = w.shape[0]
    a_end = 4 * W_A
    lr = w[:, a_end:a_end + GATE_RANK]
    uv = w[:, a_end + GATE_RANK:]
    pad = jnp.zeros((d, LANE - GATE_RANK), w.dtype)
    return jnp.concatenate([w[:, :a_end], uv, lr, pad], axis=1)


def _pad_gate_w(w):
    return jnp.concatenate([w, jnp.zeros((LANE - GATE_RANK, w.shape[1]), w.dtype)], axis=0)


def kernel(x_prompt, x_sample, state_gla, c, c_ctx, w_mod, b_mod, norm_w, final_norm_w, even_w_in, even_w_gate_f, even_b_gate_f, even_w_gate_b, even_b_gate_b, gla_norm_w, sgu_norm_w, sgu_w_s, sgu_b_s, even_w_out, odd_w_in, odd_conv_w, odd_w_out, peer_w_q, peer_sub_keys, peer_u, peer_v):
    bp, lp, d = x_prompt.shape
    bs, ls, _ = x_sample.shape
    depth = w_mod.shape[0]
    n_even = even_w_in.shape[0]
    tp = bp * lp
    ts = bs * ls
    rows = _Rows(tp, ls)

    x = _embed_call(x_prompt.reshape(tp, d), x_sample.reshape(ts, d),
                    _grid_pos_embed(ls // GRID_W, x_sample.dtype))

    cc = jnp.concatenate([c_ctx[None], c, jnp.zeros((MOD_ROWS - 1 - bs, d), F32)], axis=0)
    mod4 = _mod_call(cc, w_mod, b_mod).reshape(depth, MOD_ROWS, 1, 6 * d)

    cd, sd = _dft_mats(DD)
    clp, slp = _dft_mats(lp)
    cls, sls = _dft_mats(ls)
    s0_all = state_gla.reshape(bs, n_even, 2 * H_A, DK_A, DV_A)

    new_states = []
    for l in range(depth):
        if l % 2 == 0:
            e = l // 2
            z = _normproj_call(x, norm_w[l, 0], mod4, l, rows, _even_w_in_layout(even_w_in[e]).astype(BF16))
            gate_args = (_pad_gate_w(even_w_gate_f[e]), even_b_gate_f[e][None],
                         _pad_gate_w(even_w_gate_b[e]), even_b_gate_b[e][None], gla_norm_w[e][None])
            o_p, sf, sb = _gla_call(z, lp, bp, 0, *gate_args, want_state=True)
            (o_s,) = _gla_call(z, ls, bs, tp // ls, *gate_args,
                               s0=s0_all[:, e].reshape(bs * 2 * H_A, DK_A, DV_A))
            a1 = (o_p, o_s)
            bs_full = jnp.broadcast_to(sgu_b_s[e][:, :, None], (H_B, CHUNK_B, DB))
            a2 = _sgu_call(z, sgu_norm_w[e][None], sgu_w_s[e].astype(BF16), bs_full)
            x = _outproj_call(a1, a2, even_w_out[e].astype(BF16), x, mod4, l, rows)
            sfb = jnp.stack([sf.reshape(bp, H_A, DK_A, DV_A), sb.reshape(bp, H_A, DK_A, DV_A)], axis=1)
            new_states.append(sfb.astype(x_prompt.dtype))
        else:
            o = l // 2
            z = _normproj_call(x, norm_w[l, 0], mod4, l, rows, odd_w_in[o].astype(BF16))
            yc = (_conv_call(z, lp, bp, 0, odd_conv_w[o]), _conv_call(z, ls, bs, tp // ls, odd_conv_w[o]))
            yd = (_fft_call(z, lp, bp, 0, cd, sd, clp, slp), _fft_call(z, ls, bs, tp // ls, cd, sd, cls, sls))
            x = _outproj_call(yc, yd, odd_w_out[o].astype(BF16), x, mod4, l, rows)
        h_bf, r2, e2, cut, e1 = _psel_call(x, norm_w[l, 1], mod4, l, rows,
                                           peer_w_q[l].T.astype(BF16), peer_sub_keys[l])
        x = _pdense_call(h_bf, peer_u[l].astype(BF16), peer_v[l].astype(BF16),
                         r2, e2, cut, e1, x, mod4, l, rows)

    y_prompt = _final_call(x, final_norm_w, 0, tp).reshape(bp, lp, d)
    y_sample = _final_call(x, final_norm_w, tp // 512, ts).reshape(bs, ls, d)
    state_new = jnp.stack(new_states, axis=1)
    return (y_prompt, y_sample, state_new)
```

```python
import functools
import math

import jax
import jax.numpy as jnp
import numpy as np
from jax import lax
from jax.experimental import pallas as pl
from jax.experimental.pallas import tpu as pltpu

F32 = jnp.float32
BF16 = jnp.bfloat16

D_MODEL = 1024
DEPTH = 4
GRID_W = 64
RMS_EPS = 1e-6
H_A = 4
DK_A = 128
DV_A = 128
GATE_RANK = 16
GATE_NORM = 16.0
GLA_CHUNK = 64
W_A = H_A * DV_A
H_B = 4
DB = 128
CHUNK_B = 128
W_B = H_B * DB
H_C = 4
DC = 128
W_C = H_C * DC
H_D = 4
DD = 128
W_D = H_D * DD
N_KEYS = 128
N_EXPERTS = N_KEYS * N_KEYS
PEER_HEADS = 8
PEER_TOPK = 16
D_KEY = 256

LANE = 128
MOD_ROWS = 8
EVEN_IN_PAD = 4 * W_A + 2 * W_B + LANE
VMEM_LIMIT = 48 * 1024 * 1024

NN = (((1,), (0,)), ((), ()))
NT = (((1,), (1,)), ((), ()))
TN = (((0,), (0,)), ((), ()))


def _cparams(n_axes, flags=None):
    return pltpu.CompilerParams(dimension_semantics=("arbitrary",) * n_axes,
                                vmem_limit_bytes=VMEM_LIMIT, flags=flags)


def _dot(a, b, dims=NN):
    return lax.dot_general(a, b, dims, preferred_element_type=F32)


def _split2(x):
    hi = x.astype(BF16)
    lo = (x - hi.astype(F32)).astype(BF16)
    return hi, lo


def _split3(x):
    hi = x.astype(BF16)
    r = x - hi.astype(F32)
    mid = r.astype(BF16)
    lo = (r - mid.astype(F32)).astype(BF16)
    return hi, mid, lo


def _dot3(a, b, dims=NN):
    ah, al = _split2(a)
    bh, bl = _split2(b)
    return _dot(ah, bh, dims) + _dot(al, bh, dims) + _dot(ah, bl, dims)


def _silu(x):
    return x * (1.0 / (1.0 + jnp.exp(-x)))


def _gelu(x):
    c = math.sqrt(2.0 / math.pi)
    return x * (0.5 * (1.0 + jnp.tanh(c * (x + 0.044715 * (x * x * x)))))


def _log_sigmoid(x):
    return jnp.minimum(x, 0.0) - jnp.log(1.0 + jnp.exp(-jnp.abs(x)))


def _rms(x):
    return x * lax.rsqrt(jnp.mean(x * x, axis=-1, keepdims=True) + RMS_EPS)


def _modnorm(x, nw, shift, scale):
    return (_rms(x) * nw) * (1.0 + scale) + shift


def _mod_kernel(cc_ref, w_ref, b_ref, o_ref):
    a = _silu(cc_ref[...])
    o_ref[0] = _dot3(a, w_ref[0]) + b_ref[0]


def _mod_call(cc, w_mod, b_mod):
    depth, d, n = w_mod.shape
    tn = 1024
    return pl.pallas_call(
        _mod_kernel,
        out_shape=jax.ShapeDtypeStruct((depth, MOD_ROWS, n), F32),
        grid=(depth, n // tn),
        in_specs=[
            pl.BlockSpec((MOD_ROWS, d), lambda l, j: (0, 0)),
            pl.BlockSpec((1, d, tn), lambda l, j: (l, 0, j)),
            pl.BlockSpec((1, 1, tn), lambda l, j: (l, 0, j)),
        ],
        out_specs=pl.BlockSpec((1, MOD_ROWS, tn), lambda l, j: (l, 0, j)),
        compiler_params=_cparams(2),
        name="mod_rows",
    )(cc, w_mod, b_mod.reshape(depth, 1, n))


class _Rows:
    def __init__(self, n_prompt_rows, dec_seq):
        self.n_prompt_rows = n_prompt_rows
        self.dec_seq = dec_seq

    def mod_row(self, i, tm):
        npt = self.n_prompt_rows // tm
        tps = self.dec_seq // tm
        return jnp.where(i < npt, 0, 1 + (i - npt) // tps)

    def mod_spec(self, layer, section, tm, tile_of=lambda i, *_: i):
        return pl.BlockSpec((1, 1, 1, D_MODEL),
                            lambda *g: (layer, self.mod_row(tile_of(*g), tm), 0, section))


def _normproj_kernel(x_ref, nw_ref, sh_ref, sc_ref, w_ref, o_ref):
    h = _modnorm(x_ref[...], nw_ref[...], sh_ref[0, 0], sc_ref[0, 0])
    o_ref[...] = _dot(h.astype(BF16), w_ref[...])


def _normproj_call(x, nw, mod4, layer, rows, w_bf, tm=256):
    t, d = x.shape
    n = w_bf.shape[1]
    return pl.pallas_call(
        _normproj_kernel,
        out_shape=jax.ShapeDtypeStruct((t, n), F32),
        grid=(t // tm,),
        in_specs=[
            pl.BlockSpec((tm, d), lambda i: (i, 0)),
            pl.BlockSpec((1, d), lambda i: (0, 0)),
            rows.mod_spec(layer, 0, tm),
            rows.mod_spec(layer, 1, tm),
            pl.BlockSpec((d, n), lambda i: (0, 0)),
        ],
        out_specs=pl.BlockSpec((tm, n), lambda i: (i, 0)),
        compiler_params=_cparams(1),
        name="normproj",
    )(x, nw.reshape(1, d), mod4, mod4, w_bf)


def _gla_kernel(*refs, has_s0, want_state):
    (q_ref, k_ref, v_ref, g_ref, lr_ref, wgf_ref, bgf_ref, wgb_ref, bgb_ref, nw_ref) = refs[:10]
    pos = 10
    if has_s0:
        s0f_ref, s0b_ref = refs[pos:pos + 2]
        pos += 2
    o_ref = refs[pos]
    pos += 1
    if want_state:
        sf_ref, sb_ref = refs[pos:pos + 2]
        pos += 2
    laf_scr, lab_scr, of_scr, ob_scr, s_scr = refs[pos:]

    seq = q_ref.shape[0]
    c = GLA_CHUNK
    per_group = 4
    gw = per_group * c
    n_groups = seq // gw
    scale = DK_A ** -0.5

    lr = lr_ref[...]
    laf_scr[...] = _log_sigmoid(_dot3(lr, wgf_ref[...]) + bgf_ref[...]) * (1.0 / GATE_NORM)
    lab_scr[...] = _log_sigmoid(_dot3(lr, wgb_ref[...]) + bgb_ref[...]) * (1.0 / GATE_NORM)
    if has_s0:
        s_scr[0] = s0f_ref[0].T
        s_scr[1] = s0b_ref[0].T
    else:
        s_scr[...] = jnp.zeros(s_scr.shape, F32)

    ri = lax.broadcasted_iota(jnp.int32, (gw, gw), 0)
    ci = lax.broadcasted_iota(jnp.int32, (gw, gw), 1)
    same = (ri // c) == (ci // c)
    lower = jnp.logical_and(same, ci <= ri)
    upper = jnp.logical_and(same, ci >= ri)
    tril = jnp.where(lower, 1.0, 0.0).astype(BF16)
    triu = jnp.where(upper, 1.0, 0.0).astype(BF16)
    ones = jnp.where(same, 1.0, 0.0).astype(BF16)

    def group(r0, la_scr, d, tri, mask, order, out_scr):
        rs = pl.ds(r0, gw)
        la_h, la_m, la_l = _split3(la_scr[rs, :])
        b = _dot(tri, la_h) + _dot(tri, la_m) + _dot(tri, la_l)
        be = _dot(ones, la_h) + _dot(ones, la_m) + _dot(ones, la_l)
        q = q_ref[rs, :] * scale
        k = k_ref[rs, :]
        v = v_ref[rs, :].astype(BF16)
        qd = (q * jnp.exp(b)).astype(BF16)
        ki = (k * jnp.exp(-b)).astype(BF16)
        ks = (k * jnp.exp(be - b)).astype(BF16)
        att = jnp.where(mask, _dot(qd, ki, NT), 0.0)
        o_intra = _dot(att.astype(BF16), v)
        dec = jnp.exp(be)
        st = s_scr[d]
        inter = [None] * per_group
        for ch in order:
            cs = slice(ch * c, (ch + 1) * c)
            inter[ch] = _dot(qd[cs, :], st.astype(BF16), NT)
            st = st * dec[ch * c:ch * c + 1, :] + _dot(v[cs, :], ks[cs, :], TN)
        s_scr[d] = st
        out_scr[rs, :] = o_intra + jnp.concatenate(inter, axis=0)

    fwd_order = tuple(range(per_group))
    bwd_order = tuple(reversed(fwd_order))

    def body(i, carry):
        group(pl.multiple_of(i * gw, gw), laf_scr, 0, tril, lower, fwd_order, of_scr)
        group(pl.multiple_of((n_groups - 1 - i) * gw, gw), lab_scr, 1, triu, upper, bwd_order, ob_scr)
        return carry

    lax.fori_loop(0, n_groups, body, 0)

    o = _rms(of_scr[...] + ob_scr[...]) * nw_ref[...]
    o_ref[...] = (o * _silu(g_ref[...])).astype(o_ref.dtype)
    if want_state:
        sf_ref[0] = s_scr[0].T
        sb_ref[0] = s_scr[1].T


def _gla_call(z, seq, n_seq, row_block0, wgf, bgf, wgb, bgb, nw, s0=None, want_state=False):
    has_s0 = s0 is not None

    def col(cb):
        return pl.BlockSpec((seq, LANE), lambda s, h: (row_block0 + s, cb + h))

    in_specs = [col(0), col(H_A), col(2 * H_A), col(3 * H_A),
                pl.BlockSpec((seq, LANE), lambda s, h: (row_block0 + s, (4 * W_A + 2 * W_B) // LANE)),
                pl.BlockSpec((LANE, DK_A), lambda s, h: (0, h)),
                pl.BlockSpec((1, DK_A), lambda s, h: (0, h)),
                pl.BlockSpec((LANE, DK_A), lambda s, h: (0, h)),
                pl.BlockSpec((1, DK_A), lambda s, h: (0, h)),
                pl.BlockSpec((1, DV_A), lambda s, h: (0, 0))]
    args = [z, z, z, z, z, wgf, bgf, wgb, bgb, nw]
    if has_s0:
        in_specs += [pl.BlockSpec((1, DK_A, DV_A), lambda s, h: ((s * 2 + 0) * H_A + h, 0, 0)),
                     pl.BlockSpec((1, DK_A, DV_A), lambda s, h: ((s * 2 + 1) * H_A + h, 0, 0))]
        args += [s0, s0]
    out_shape = [jax.ShapeDtypeStruct((n_seq * seq, W_A), BF16)]
    out_specs = [pl.BlockSpec((seq, LANE), lambda s, h: (s, h))]
    if want_state:
        out_shape += [jax.ShapeDtypeStruct((n_seq * H_A, DK_A, DV_A), F32)] * 2
        out_specs += [pl.BlockSpec((1, DK_A, DV_A), lambda s, h: (s * H_A + h, 0, 0))] * 2
    return pl.pallas_call(
        functools.partial(_gla_kernel, has_s0=has_s0, want_state=want_state),
        out_shape=out_shape,
        grid=(n_seq, H_A),
        in_specs=in_specs,
        out_specs=out_specs,
        scratch_shapes=[pltpu.VMEM((seq, DK_A), F32), pltpu.VMEM((seq, DK_A), F32),
                        pltpu.VMEM((seq, DV_A), F32), pltpu.VMEM((seq, DV_A), F32),
                        pltpu.VMEM((2, DK_A, DV_A), F32)],
        compiler_params=_cparams(2),
        name="gla_seq%d" % seq,
    )(*args)


def _sgu_kernel(u_ref, v_ref, nw_ref, ws_ref, bs_ref, o_ref):
    tm = u_ref.shape[0]
    u = _gelu(u_ref[...])
    v = (_rms(_gelu(v_ref[...])) * nw_ref[...]).astype(BF16)
    ws = ws_ref[0]
    bs = bs_ref[0]
    for cidx in range(tm // CHUNK_B):
        rs = slice(cidx * CHUNK_B, (cidx + 1) * CHUNK_B)
        sg = _dot(ws, v[rs, :]) + bs
        o_ref[rs, :] = (u[rs, :] * sg).astype(o_ref.dtype)


def _sgu_call(z, nw, ws_bf, bs_full, tm=512):
    t = z.shape[0]
    ub0 = (4 * W_A) // LANE
    vb0 = (4 * W_A + W_B) // LANE
    return pl.pallas_call(
        _sgu_kernel,
        out_shape=jax.ShapeDtypeStruct((t, W_B), BF16),
        grid=(t // tm, H_B),
        in_specs=[
            pl.BlockSpec((tm, LANE), lambda i, h: (i, ub0 + h)),
            pl.BlockSpec((tm, LANE), lambda i, h: (i, vb0 + h)),
            pl.BlockSpec((1, DB), lambda i, h: (0, 0)),
            pl.BlockSpec((1, CHUNK_B, CHUNK_B), lambda i, h: (h, 0, 0)),
            pl.BlockSpec((1, CHUNK_B, DB), lambda i, h: (h, 0, 0)),
        ],
        out_specs=pl.BlockSpec((tm, LANE), lambda i, h: (i, h)),
        compiler_params=_cparams(2),
        name="sgu",
    )(z, z, nw, ws_bf, bs_full)


def _outproj_kernel(a1p_ref, a1s_ref, a2p_ref, a2s_ref, w_ref, x_ref, g_ref, o_ref, *, n_prompt_tiles):
    n1 = a1p_ref.shape[1]

    def run(a1_ref, a2_ref):
        y = _dot(a1_ref[...], w_ref[:n1, :]) + _dot(a2_ref[...], w_ref[n1:, :])
        o_ref[...] = x_ref[...] + g_ref[0, 0] * y

    @pl.when(pl.program_id(0) < n_prompt_tiles)
    def _():
        run(a1p_ref, a2p_ref)

    @pl.when(pl.program_id(0) >= n_prompt_tiles)
    def _():
        run(a1s_ref, a2s_ref)


def _outproj_call(a1, a2, w_bf, x, mod4, layer, rows, tm=512):
    t, d = x.shape
    npt = rows.n_prompt_rows // tm

    def parts(a):
        if isinstance(a, tuple):
            n = a[0].shape[1]
            return (a[0], a[1],
                    pl.BlockSpec((tm, n), lambda i: (jnp.minimum(i, npt - 1), 0)),
                    pl.BlockSpec((tm, n), lambda i: (jnp.maximum(i - npt, 0), 0)))
        spec = pl.BlockSpec((tm, a.shape[1]), lambda i: (i, 0))
        return a, a, spec, spec

    a1p, a1s, s1p, s1s = parts(a1)
    a2p, a2s, s2p, s2s = parts(a2)
    n1, n2 = a1p.shape[1], a2p.shape[1]
    return pl.pallas_call(
        functools.partial(_outproj_kernel, n_prompt_tiles=npt),
        out_shape=jax.ShapeDtypeStruct((t, d), F32),
        grid=(t // tm,),
        in_specs=[
            s1p, s1s, s2p, s2s,
            pl.BlockSpec((n1 + n2, d), lambda i: (0, 0)),
            pl.BlockSpec((tm, d), lambda i: (i, 0)),
            rows.mod_spec(layer, 2, tm),
        ],
        out_specs=pl.BlockSpec((tm, d), lambda i: (i, 0)),
        compiler_params=_cparams(1),
        name="outproj",
    )(a1p, a1s, a2p, a2s, w_bf, x, mod4)


def _embed_kernel(xp_ref, xs_ref, pe_ref, o_ref, *, n_prompt_tiles):
    @pl.when(pl.program_id(0) < n_prompt_tiles)
    def _():
        o_ref[...] = xp_ref[...]

    @pl.when(pl.program_id(0) >= n_prompt_tiles)
    def _():
        o_ref[...] = xs_ref[...] + pe_ref[...]


def _embed_call(xp, xs, pe, tm=512):
    tp, d = xp.shape
    ts = xs.shape[0]
    npt = tp // tm
    pe_tiles = pe.shape[0] // tm
    return pl.pallas_call(
        functools.partial(_embed_kernel, n_prompt_tiles=npt),
        out_shape=jax.ShapeDtypeStruct((tp + ts, d), F32),
        grid=((tp + ts) // tm,),
        in_specs=[
            pl.BlockSpec((tm, d), lambda i: (jnp.minimum(i, npt - 1), 0)),
            pl.BlockSpec((tm, d), lambda i: (jnp.maximum(i - npt, 0), 0)),
            pl.BlockSpec((tm, d), lambda i: (jnp.maximum(i - npt, 0) % pe_tiles, 0)),
        ],
        out_specs=pl.BlockSpec((tm, d), lambda i: (i, 0)),
        compiler_params=_cparams(1),
        name="embed",
    )(xp, xs, pe)


def _conv_kernel(bg_ref, cg_ref, xi_ref, w_ref, o_ref):
    seq = bg_ref.shape[0]
    t = cg_ref[...] * xi_ref[...]
    row = lax.broadcasted_iota(jnp.int32, t.shape, 0)
    t_prev = jnp.where(row == 0, 0.0, pltpu.roll(t, 1, 0))
    t_next = jnp.where(row == seq - 1, 0.0, pltpu.roll(t, seq - 1, 0))
    conv = t_prev * w_ref[0:1, :] + t * w_ref[1:2, :] + t_next * w_ref[2:3, :]
    o_ref[...] = (bg_ref[...] * conv).astype(o_ref.dtype)


def _conv_call(z, seq, n_seq, row_block0, conv_w):
    def col(cb):
        return pl.BlockSpec((seq, LANE), lambda s, j: (row_block0 + s, cb + j))
    return pl.pallas_call(
        _conv_kernel,
        out_shape=jax.ShapeDtypeStruct((n_seq * seq, W_C), BF16),
        grid=(n_seq, W_C // LANE),
        in_specs=[col(0), col(H_C), col(2 * H_C),
                  pl.BlockSpec((3, LANE), lambda s, j: (0, j))],
        out_specs=pl.BlockSpec((seq, LANE), lambda s, j: (s, j)),
        compiler_params=_cparams(2),
        name="conv_seq%d" % seq,
    )(z, z, z, conv_w)


def _fft_kernel(f_ref, cd_ref, sd_ref, cl_ref, sl_ref, o_ref, p_scr, q_scr, *, scale):
    @pl.when(pl.program_id(1) == 0)
    def _():
        cd = cd_ref[...]
        sd = sd_ref[...]
        for h in range(H_D):
            cs = slice(h * DD, (h + 1) * DD)
            fh = f_ref[:, cs].astype(BF16)
            p_scr[:, cs] = _dot(fh, cd).astype(BF16)
            q_scr[:, cs] = _dot(fh, sd).astype(BF16)

    y = _dot(cl_ref[...], p_scr[...]) - _dot(sl_ref[...], q_scr[...])
    o_ref[...] = (y * scale).astype(o_ref.dtype)


def _dft_mats(n):
    idx = jnp.arange(n, dtype=jnp.int32)
    ang = ((idx[:, None] * idx[None, :]) % n).astype(F32) * (2.0 * math.pi / n)
    return jnp.cos(ang).astype(BF16), jnp.sin(ang).astype(BF16)


def _fft_call(z, seq, n_seq, row_block0, cd, sd, cl, sl):
    tl = min(seq, 512)
    fcol = (3 * W_C) // W_D
    return pl.pallas_call(
        functools.partial(_fft_kernel, scale=1.0 / math.sqrt(seq * DD)),
        out_shape=jax.ShapeDtypeStruct((n_seq * seq, W_D), BF16),
        grid=(n_seq, seq // tl),
        in_specs=[
            pl.BlockSpec((seq, W_D), lambda s, i: (row_block0 + s, fcol)),
            pl.BlockSpec((DD, DD), lambda s, i: (0, 0)),
            pl.BlockSpec((DD, DD), lambda s, i: (0, 0)),
            pl.BlockSpec((tl, seq), lambda s, i: (i, 0)),
            pl.BlockSpec((tl, seq), lambda s, i: (i, 0)),
        ],
        out_specs=pl.BlockSpec((tl, W_D), lambda s, i: (s * (seq // tl) + i, 0)),
        scratch_shapes=[pltpu.VMEM((seq, W_D), BF16), pltpu.VMEM((seq, W_D), BF16)],
        compiler_params=_cparams(2),
        name="fft_seq%d" % seq,
    )(z, cd, sd, cl, sl)


def _topk16(s):
    nrows = s.shape[0]
    iota = lax.broadcasted_iota(jnp.int32, s.shape, 0)
    i16 = lax.broadcasted_iota(jnp.int32, (PEER_TOPK, s.shape[1]), 0)

    def body(k, carry):
        s, rank, top = carry
        m = jnp.max(s, axis=0, keepdims=True)
        idx = jnp.min(jnp.where(s == m, iota, nrows), axis=0, keepdims=True)
        sel = iota == idx
        rank = jnp.where(sel, k, rank)
        s = jnp.where(sel, -jnp.inf, s)
        top = jnp.where(i16 == k, m, top)
        return s, rank, top

    init = (s, jnp.full(s.shape, PEER_TOPK, jnp.int32), jnp.zeros((PEER_TOPK, s.shape[1]), F32))
    _, rank, top = lax.fori_loop(0, PEER_TOPK, body, init)
    return rank, top


_CAND_GROUPS = [(0, 16), (1, 8)] + [(k1, 8) for k1 in range(2, 8)]
_CAND_SINGLE0 = sum(n for _, n in _CAND_GROUPS)


_SCORE_FLOOR = -(2.0 ** 126)
_CODE_UNIT = 2.0 ** 127


def _rank_code(k):
    return -_CODE_UNIT * (1.0 + (k + 1) / 32.0)


def _topk16_untied(scores):
    n = scores[0].shape[1]
    i16 = lax.broadcasted_iota(jnp.int32, (PEER_TOPK, n), 0)
    keys = [jnp.maximum(s, _SCORE_FLOOR) for s in scores]
    tops = [jnp.zeros((PEER_TOPK, n), F32) for _ in scores]
    for k in range(PEER_TOPK):
        for c in range(len(keys)):
            m = jnp.max(keys[c], axis=0, keepdims=True)
            keys[c] = jnp.where(keys[c] == m, _rank_code(k), keys[c])
            tops[c] = jnp.where(i16 == k, m, tops[c])
    out = []
    for key, top in zip(keys, tops):
        taken = key < _SCORE_FLOOR
        rank = jnp.where(taken, key * -(32.0 / _CODE_UNIT) - 33.0, float(PEER_TOPK))
        count = jnp.sum(jnp.where(taken, 1.0, 0.0), axis=0, keepdims=True)
        clamped = jnp.sum(jnp.where(top <= _SCORE_FLOOR, 1.0, 0.0), axis=0, keepdims=True)
        out.append((rank, top, (count - PEER_TOPK) + clamped))
    return out


def _cand_scores(t1, t2):
    n = t1.shape[1]
    neg = jnp.full((8, n), -jnp.inf, F32)
    row8 = lax.broadcasted_iota(jnp.int32, (8, n), 0)
    pieces = []
    for k1, nrow in _CAND_GROUPS:
        blk = t1[k1:k1 + 1, :] + t2[0:nrow, :]
        nvalid = PEER_TOPK // (k1 + 1)
        if nvalid < nrow:
            blk = jnp.where(row8 < nvalid, blk, neg)
        pieces.append(blk)
    pieces.append(t1[8:16, :] + t2[0:1, :])
    return jnp.concatenate(pieces, axis=0)


def _psel_finish(s1, s2, r1, t1, t2, cand, sel):
    n = s1.shape[1]
    cmax = t1[0:1, :] + t2[0:1, :]
    z = jnp.sum(jnp.where(sel, jnp.exp(cand - cmax), 0.0), axis=0, keepdims=True)
    self32 = jnp.where(sel, 1.0, 0.0)
    cut = jnp.zeros((N_KEYS, n), F32)
    r0 = 0
    for k1, nrow in _CAND_GROUPS:
        cnt = jnp.sum(self32[r0:r0 + nrow, :], axis=0, keepdims=True)
        cut = jnp.where(r1 == k1, cnt, cut)
        r0 += nrow
    for j in range(8):
        cut = jnp.where(r1 == 8 + j, self32[r0 + j:r0 + j + 1, :], cut)
    e1 = jnp.exp(s1 - t1[0:1, :]) * (1.0 / z)
    e2 = jnp.exp(s2 - t2[0:1, :])
    return cut, e1, e2


def _psel_kernel(x_ref, nw_ref, sh_ref, sc_ref, wq_ref, sk_ref,
                 h_ref, r2_ref, e2_ref, cut_ref, e1_ref, q_scr):
    tq = x_ref.shape[0]
    n_chunks = tq // LANE
    hb = _modnorm(x_ref[...], nw_ref[...], sh_ref[0, 0], sc_ref[0, 0]).astype(BF16)
    h_ref[...] = hb
    q_scr[...] = _dot(wq_ref[...], hb, NT)
    half = D_KEY // 2

    def scores(hh):
        q1 = q_scr[pl.ds(pl.multiple_of(hh * D_KEY, D_KEY), half), :]
        q2 = q_scr[pl.ds(pl.multiple_of(hh * D_KEY + half, half), half), :]
        return _dot3(sk_ref[0], q1), _dot3(sk_ref[1], q2)

    def write(hh, ls, r2, cut, e1, e2):
        r2_ref[hh, :, ls] = r2.astype(BF16)
        e2_ref[hh, :, ls] = e2.astype(BF16)
        cut_ref[hh, :, ls] = cut
        e1_ref[hh, :, ls] = e1

    def exact_chunk(hh, ls, s1c, s2c):
        r1, t1 = _topk16(s1c)
        r2, t2 = _topk16(s2c)
        cand = _cand_scores(t1, t2)
        crank, _ = _topk16(cand)
        cut, e1, e2 = _psel_finish(s1c, s2c, r1.astype(F32), t1, t2, cand, crank < PEER_TOPK)
        write(hh, ls, r2.astype(F32), cut, e1, e2)

    def head(hh, carry):
        s1, s2 = scores(hh)
        halves = []
        for lc in range(n_chunks):
            ls = slice(lc * LANE, (lc + 1) * LANE)
            halves.append(_topk16_untied([s1[:, ls], s2[:, ls]]))
        cands = [_cand_scores(h1[1], h2[1]) for h1, h2 in halves]
        picked = _topk16_untied(cands)
        for lc in range(n_chunks):
            ls = slice(lc * LANE, (lc + 1) * LANE)
            (r1, t1, c1), (r2, t2, c2) = halves[lc]
            crank, _, cc = picked[lc]
            cut, e1, e2 = _psel_finish(s1[:, ls], s2[:, ls], r1, t1, t2, cands[lc], crank < PEER_TOPK)
            write(hh, ls, r2, cut, e1, e2)
            extra = c1 + c2 + cc

            @pl.when(jnp.max(extra) > 0)
            def _():
                exact_chunk(hh, ls, s1[:, ls], s2[:, ls])
        return carry

    lax.fori_loop(0, PEER_HEADS, head, 0)


def _psel_call(x, nw, mod4, layer, rows, wqt_bf, sub_keys, tq=256):
    t, d = x.shape
    nq = wqt_bf.shape[0]
    sel_shape = jax.ShapeDtypeStruct((PEER_HEADS, N_KEYS, t), F32)
    sel_bf = jax.ShapeDtypeStruct((PEER_HEADS, N_KEYS, t), BF16)
    sel_spec = pl.BlockSpec((PEER_HEADS, N_KEYS, tq), lambda i: (0, 0, i))
    return pl.pallas_call(
        _psel_kernel,
        out_shape=[jax.ShapeDtypeStruct((t, d), BF16), sel_bf, sel_bf, sel_shape, sel_shape],
        grid=(t // tq,),
        in_specs=[
            pl.BlockSpec((tq, d), lambda i: (i, 0)),
            pl.BlockSpec((1, d), lambda i: (0, 0)),
            rows.mod_spec(layer, 3, tq),
            rows.mod_spec(layer, 4, tq),
            pl.BlockSpec((nq, d), lambda i: (0, 0)),
            pl.BlockSpec((2, N_KEYS, D_KEY // 2), lambda i: (0, 0, 0)),
        ],
        out_specs=[pl.BlockSpec((tq, d), lambda i: (i, 0)), sel_spec, sel_spec, sel_spec, sel_spec],
        scratch_shapes=[pltpu.VMEM((nq, tq), F32)],
        compiler_params=_cparams(1),
        name="peer_select",
    )(x, nw.reshape(1, d), mod4, mod4, wqt_bf, sub_keys)


def _gelu_sigmoid_form(x):
    k = 2.0 * math.sqrt(2.0 / math.pi) * math.log2(math.e)
    u = x * (-k - (k * 0.044715) * (x * x))
    return x * (1.0 / (1.0 + jnp.exp2(u)))


def _pdense_kernel(h_ref, u_ref, v_ref, r2_ref, e2_ref, cut_ref, e1_ref, x_ref, g_ref, o_ref,
                   hid0, hid1, w0, w1, acc_scr, *, nj, n_real):
    s = pl.program_id(0)
    s3 = jnp.clip(s - 2, 0, n_real - 1)
    j3 = s3 % nj
    te, tq = hid0.shape
    bzero = jnp.zeros((), BF16)

    @pl.when(s == 0)
    def _():
        for r in (hid0, hid1, w0, w1):
            r[...] = jnp.zeros(r.shape, r.dtype)

    @pl.when(j3 == 0)
    def _():
        acc_scr[...] = jnp.zeros(acc_scr.shape, F32)

    def step(hid_new, hid_old, w_new, w_old):
        for a in range(te // N_KEYS):
            rs = slice(a * N_KEYS, (a + 1) * N_KEYS)
            if a % 2 == 0:
                rs2 = slice(a * N_KEYS, (a + 2) * N_KEYS)
                hid_new[rs2, :] = _dot(u_ref[rs2, :], h_ref[...], NT)
            for lc in range(tq // (2 * LANE)):
                ls = slice(lc * 2 * LANE, (lc + 1) * 2 * LANE)
                gate = None
                for hh in range(PEER_HEADS):
                    cut = jnp.broadcast_to(cut_ref[hh, a:a + 1, ls], (N_KEYS, 2 * LANE)).astype(BF16)
                    e1 = jnp.broadcast_to(e1_ref[hh, a:a + 1, ls], (N_KEYS, 2 * LANE)).astype(BF16)
                    term = jnp.where(r2_ref[hh, :, ls] < cut, e2_ref[hh, :, ls] * e1, bzero)
                    gate = term if gate is None else gate + term
                w_new[rs, ls] = _gelu_sigmoid_form(hid_old[rs, ls].astype(BF16)) * gate
            if a % 2 == 1:
                acc_scr[...] += _dot(w_old[rs2, :], v_ref[rs2, :], TN)

    @pl.when(s % 2 == 0)
    def _():
        step(hid0, hid1, w0, w1)

    @pl.when(s % 2 == 1)
    def _():
        step(hid1, hid0, w1, w0)

    @pl.when(jnp.logical_and(j3 == nj - 1, s >= 2))
    def _():
        o_ref[...] = x_ref[...] + g_ref[0, 0] * acc_scr[...]


def _pdense_call(h_bf, u_bf, v_bf, r2, e2, cut, e1, x, mod4, layer, rows, tq=512, te=2048):
    t, d = x.shape
    ne = u_bf.shape[1]
    ni, nj = t // tq, ne // te
    n_real = ni * nj
    na = te // N_KEYS

    def tile(shift):
        def f(s):
            ss = jnp.clip(s - shift, 0, n_real - 1)
            return ss // nj, ss % nj
        return f

    t1, t2, t3 = tile(0), tile(1), tile(2)
    return pl.pallas_call(
        functools.partial(_pdense_kernel, nj=nj, n_real=n_real),
        out_shape=jax.ShapeDtypeStruct((t, d), F32),
        grid=(n_real + 2,),
        in_specs=[
            pl.BlockSpec((tq, d), lambda s: (t1(s)[0], 0)),
            pl.BlockSpec((None, te, d), lambda s: (layer, t1(s)[1], 0)),
            pl.BlockSpec((None, te, d), lambda s: (layer, t3(s)[1], 0)),
            pl.BlockSpec((PEER_HEADS, N_KEYS, tq), lambda s: (0, 0, t2(s)[0])),
            pl.BlockSpec((PEER_HEADS, N_KEYS, tq), lambda s: (0, 0, t2(s)[0])),
            pl.BlockSpec((PEER_HEADS, na, tq), lambda s: (0, t2(s)[1], t2(s)[0])),
            pl.BlockSpec((PEER_HEADS, na, tq), lambda s: (0, t2(s)[1], t2(s)[0])),
            pl.BlockSpec((tq, d), lambda s: (t3(s)[0], 0)),
            rows.mod_spec(layer, 5, tq, tile_of=lambda s: t3(s)[0]),
        ],
        out_specs=pl.BlockSpec((tq, d), lambda s: (t3(s)[0], 0)),
        scratch_shapes=[pltpu.VMEM((te, tq), F32), pltpu.VMEM((te, tq), F32),
                        pltpu.VMEM((te, tq), BF16), pltpu.VMEM((te, tq), BF16),
                        pltpu.VMEM((tq, d), F32)],
        compiler_params=_cparams(1),
        name="peer_dense",
    )(h_bf, u_bf, v_bf, r2, e2, cut, e1, x, mod4)


def _final_kernel(x_ref, w_ref, o_ref):
    o_ref[...] = _rms(x_ref[...]) * w_ref[...]


def _final_call(x, w, row_block0, n_rows, tm=512):
    d = x.shape[1]
    return pl.pallas_call(
        _final_kernel,
        out_shape=jax.ShapeDtypeStruct((n_rows, d), F32),
        grid=(n_rows // tm,),
        in_specs=[pl.BlockSpec((tm, d), lambda i: (row_block0 + i, 0)),
                  pl.BlockSpec((1, d), lambda i: (0, 0))],
        out_specs=pl.BlockSpec((tm, d), lambda i: (i, 0)),
        compiler_params=_cparams(1),
        name="final_norm",
    )(x, w.reshape(1, d))


def _grid_pos_embed(rows, dtype):
    t = jnp.arange(rows * GRID_W)
    r = (t // GRID_W).astype(F32)
    col = (t % GRID_W).astype(F32)
    nf = D_MODEL // 4
    freqs = 1.0 / (10000.0 ** (jnp.arange(nf, dtype=F32) / nf))

    def emb(p):
        a = p[:, None] * freqs[None, :]
        return jnp.concatenate([jnp.sin(a), jnp.cos(a)], axis=-1)
    return jnp.concatenate([emb(r), emb(col)], axis=-1).astype(dtype)


def _even_w_in_layout(w):
    d = w.shape[0]
    a_end = 4 * W_A
    lr = w[:, a_end:a_end + GATE_RANK]
    uv = w[:, a_end + GATE_RANK:]
    pad = jnp.zeros((d, LANE - GATE_RANK), w.dtype)
    return jnp.concatenate([w[:, :a_end], uv, lr, pad], axis=1)


def _pad_gate_w(w):
    return jnp.concatenate([w, jnp.zeros((LANE - GATE_RANK, w.shape[1]), w.dtype)], axis=0)


def kernel(x_prompt, x_sample, state_gla, c, c_ctx, w_mod, b_mod, norm_w, final_norm_w, even_w_in, even_w_gate_f, even_b_gate_f, even_w_gate_b, even_b_gate_b, gla_norm_w, sgu_norm_w, sgu_w_s, sgu_b_s, even_w_out, odd_w_in, odd_conv_w, odd_w_out, peer_w_q, peer_sub_keys, peer_u, peer_v):
    bp, lp, d = x_prompt.shape
    bs, ls, _ = x_sample.shape
    depth = w_mod.shape[0]
    n_even = even_w_in.shape[0]
    tp = bp * lp
    ts = bs * ls
    rows = _Rows(tp, ls)

    x = _embed_call(x_prompt.reshape(tp, d), x_sample.reshape(ts, d),
                    _grid_pos_embed(ls // GRID_W, x_sample.dtype))

    cc = jnp.concatenate([c_ctx[None], c, jnp.zeros((MOD_ROWS - 1 - bs, d), F32)], axis=0)
    mod4 = _mod_call(cc, w_mod, b_mod).reshape(depth, MOD_ROWS, 1, 6 * d)

    cd, sd = _dft_mats(DD)
    clp, slp = _dft_mats(lp)
    cls, sls = _dft_mats(ls)
    s0_all = state_gla.reshape(bs, n_even, 2 * H_A, DK_A, DV_A)
    u_all = peer_u.astype(BF16)
    v_all = peer_v.astype(BF16)

    new_states = []
    for l in range(depth):
        if l % 2 == 0:
            e = l // 2
            z = _normproj_call(x, norm_w[l, 0], mod4, l, rows, _even_w_in_layout(even_w_in[e]).astype(BF16))
            gate_args = (_pad_gate_w(even_w_gate_f[e]), even_b_gate_f[e][None],
                         _pad_gate_w(even_w_gate_b[e]), even_b_gate_b[e][None], gla_norm_w[e][None])
            o_p, sf, sb = _gla_call(z, lp, bp, 0, *gate_args, want_state=True)
            (o_s,) = _gla_call(z, ls, bs, tp // ls, *gate_args,
                               s0=s0_all[:, e].reshape(bs * 2 * H_A, DK_A, DV_A))
            a1 = (o_p, o_s)
            bs_full = jnp.broadcast_to(sgu_b_s[e][:, :, None], (H_B, CHUNK_B, DB))
            a2 = _sgu_call(z, sgu_norm_w[e][None], sgu_w_s[e].astype(BF16), bs_full)
            x = _outproj_call(a1, a2, even_w_out[e].astype(BF16), x, mod4, l, rows)
            sfb = jnp.stack([sf.reshape(bp, H_A, DK_A, DV_A), sb.reshape(bp, H_A, DK_A, DV_A)], axis=1)
            new_states.append(sfb.astype(x_prompt.dtype))
        else:
            o = l // 2
            z = _normproj_call(x, norm_w[l, 0], mod4, l, rows, odd_w_in[o].astype(BF16))
            yc = (_conv_call(z, lp, bp, 0, odd_conv_w[o]), _conv_call(z, ls, bs, tp // ls, odd_conv_w[o]))
            yd = (_fft_call(z, lp, bp, 0, cd, sd, clp, slp), _fft_call(z, ls, bs, tp // ls, cd, sd, cls, sls))
            x = _outproj_call(yc, yd, odd_w_out[o].astype(BF16), x, mod4, l, rows)
        h_bf, r2, e2, cut, e1 = _psel_call(x, norm_w[l, 1], mod4, l, rows,
                                           peer_w_q[l].T.astype(BF16), peer_sub_keys[l])
        x = _pdense_call(h_bf, u_all, v_all, r2, e2, cut, e1, x, mod4, l, rows)

    y_prompt = _final_call(x, final_norm_w, 0, tp).reshape(bp, lp, d)
    y_sample = _final_call(x, final_norm_w, tp // 512, ts).reshape(bs, ls, d)
    state_new = jnp.stack(new_states, axis=1)
    return (y_prompt, y_sample, state_new)
```

```python
import functools
import math

import jax
import jax.numpy as jnp
import numpy as np
from jax import lax
from jax.experimental import pallas as pl
from jax.experimental.pallas import tpu as pltpu

F32 = jnp.float32
BF16 = jnp.bfloat16

D_MODEL = 1024
DEPTH = 4
GRID_W = 64
RMS_EPS = 1e-6
H_A = 4
DK_A = 128
DV_A = 128
GATE_RANK = 16
GATE_NORM = 16.0
GLA_CHUNK = 64
W_A = H_A * DV_A
H_B = 4
DB = 128
CHUNK_B = 128
W_B = H_B * DB
H_C = 4
DC = 128
W_C = H_C * DC
H_D = 4
DD = 128
W_D = H_D * DD
N_KEYS = 128
N_EXPERTS = N_KEYS * N_KEYS
PEER_HEADS = 8
PEER_TOPK = 16
D_KEY = 256

LANE = 128
MOD_ROWS = 8
EVEN_IN_PAD = 4 * W_A + 2 * W_B + LANE
VMEM_LIMIT = 48 * 1024 * 1024

NN = (((1,), (0,)), ((), ()))
NT = (((1,), (1,)), ((), ()))
TN = (((0,), (0,)), ((), ()))


def _cparams(n_axes, flags=None):
    return pltpu.CompilerParams(dimension_semantics=("arbitrary",) * n_axes,
                                vmem_limit_bytes=VMEM_LIMIT, flags=flags)


def _dot(a, b, dims=NN):
    return lax.dot_general(a, b, dims, preferred_element_type=F32)


def _split2(x):
    hi = x.astype(BF16)
    lo = (x - hi.astype(F32)).astype(BF16)
    return hi, lo


def _split3(x):
    hi = x.astype(BF16)
    r = x - hi.astype(F32)
    mid = r.astype(BF16)
    lo = (r - mid.astype(F32)).astype(BF16)
    return hi, mid, lo


def _dot3(a, b, dims=NN):
    ah, al = _split2(a)
    bh, bl = _split2(b)
    return _dot(ah, bh, dims) + _dot(al, bh, dims) + _dot(ah, bl, dims)


def _silu(x):
    return x * (1.0 / (1.0 + jnp.exp(-x)))


def _gelu(x):
    c = math.sqrt(2.0 / math.pi)
    return x * (0.5 * (1.0 + jnp.tanh(c * (x + 0.044715 * (x * x * x)))))


def _log_sigmoid(x):
    return jnp.minimum(x, 0.0) - jnp.log(1.0 + jnp.exp(-jnp.abs(x)))


def _rms(x):
    return x * lax.rsqrt(jnp.mean(x * x, axis=-1, keepdims=True) + RMS_EPS)


def _modnorm(x, nw, shift, scale):
    return (_rms(x) * nw) * (1.0 + scale) + shift


def _mod_kernel(cc_ref, w_ref, b_ref, o_ref):
    a = _silu(cc_ref[...])
    o_ref[0] = _dot3(a, w_ref[0]) + b_ref[0]


def _mod_call(cc, w_mod, b_mod):
    depth, d, n = w_mod.shape
    tn = 1024
    return pl.pallas_call(
        _mod_kernel,
        out_shape=jax.ShapeDtypeStruct((depth, MOD_ROWS, n), F32),
        grid=(depth, n // tn),
        in_specs=[
            pl.BlockSpec((MOD_ROWS, d), lambda l, j: (0, 0)),
            pl.BlockSpec((1, d, tn), lambda l, j: (l, 0, j)),
            pl.BlockSpec((1, 1, tn), lambda l, j: (l, 0, j)),
        ],
        out_specs=pl.BlockSpec((1, MOD_ROWS, tn), lambda l, j: (l, 0, j)),
        compiler_params=_cparams(2),
        name="mod_rows",
    )(cc, w_mod, b_mod.reshape(depth, 1, n))


class _Rows:
    def __init__(self, n_prompt_rows, dec_seq):
        self.n_prompt_rows = n_prompt_rows
        self.dec_seq = dec_seq

    def mod_row(self, i, tm):
        npt = self.n_prompt_rows // tm
        tps = self.dec_seq // tm
        return jnp.where(i < npt, 0, 1 + (i - npt) // tps)

    def mod_spec(self, layer, section, tm, tile_of=lambda i, *_: i):
        return pl.BlockSpec((1, 1, 1, D_MODEL),
                            lambda *g: (layer, self.mod_row(tile_of(*g), tm), 0, section))


def _normproj_kernel(x_ref, nw_ref, sh_ref, sc_ref, w_ref, o_ref):
    h = _modnorm(x_ref[...], nw_ref[...], sh_ref[0, 0], sc_ref[0, 0])
    o_ref[...] = _dot(h.astype(BF16), w_ref[...])


def _normproj_call(x, nw, mod4, layer, rows, w_bf, tm=256):
    t, d = x.shape
    n = w_bf.shape[1]
    return pl.pallas_call(
        _normproj_kernel,
        out_shape=jax.ShapeDtypeStruct((t, n), F32),
        grid=(t // tm,),
        in_specs=[
            pl.BlockSpec((tm, d), lambda i: (i, 0)),
            pl.BlockSpec((1, d), lambda i: (0, 0)),
            rows.mod_spec(layer, 0, tm),
            rows.mod_spec(layer, 1, tm),
            pl.BlockSpec((d, n), lambda i: (0, 0)),
        ],
        out_specs=pl.BlockSpec((tm, n), lambda i: (i, 0)),
        compiler_params=_cparams(1),
        name="normproj",
    )(x, nw.reshape(1, d), mod4, mod4, w_bf)


def _gla_kernel(*refs, has_s0, want_state):
    (q_ref, k_ref, v_ref, g_ref, lr_ref, wgf_ref, bgf_ref, wgb_ref, bgb_ref, nw_ref) = refs[:10]
    pos = 10
    if has_s0:
        s0f_ref, s0b_ref = refs[pos:pos + 2]
        pos += 2
    o_ref = refs[pos]
    pos += 1
    if want_state:
        sf_ref, sb_ref = refs[pos:pos + 2]
        pos += 2
    laf_scr, lab_scr, of_scr, ob_scr, s_scr = refs[pos:]

    seq = q_ref.shape[0]
    c = GLA_CHUNK
    per_group = 4
    gw = per_group * c
    n_groups = seq // gw
    scale = DK_A ** -0.5

    lr = lr_ref[...]
    laf_scr[...] = _log_sigmoid(_dot3(lr, wgf_ref[...]) + bgf_ref[...]) * (1.0 / GATE_NORM)
    lab_scr[...] = _log_sigmoid(_dot3(lr, wgb_ref[...]) + bgb_ref[...]) * (1.0 / GATE_NORM)
    if has_s0:
        s_scr[0] = s0f_ref[0].T
        s_scr[1] = s0b_ref[0].T
    else:
        s_scr[...] = jnp.zeros(s_scr.shape, F32)

    ri = lax.broadcasted_iota(jnp.int32, (gw, gw), 0)
    ci = lax.broadcasted_iota(jnp.int32, (gw, gw), 1)
    same = (ri // c) == (ci // c)
    lower = jnp.logical_and(same, ci <= ri)
    upper = jnp.logical_and(same, ci >= ri)
    tril = jnp.where(lower, 1.0, 0.0).astype(BF16)
    triu = jnp.where(upper, 1.0, 0.0).astype(BF16)
    ones = jnp.where(same, 1.0, 0.0).astype(BF16)

    la_scrs = (laf_scr, lab_scr)
    tris = (tril, triu)
    masks = (lower, upper)
    out_scrs = (of_scr, ob_scr)
    orders = (tuple(range(per_group)), tuple(reversed(range(per_group))))
    dirs = (0, 1)

    def body(i, carry):
        r0s = (pl.multiple_of(i * gw, gw), pl.multiple_of((n_groups - 1 - i) * gw, gw))
        rss = [pl.ds(r0, gw) for r0 in r0s]
        las = [_split3(la_scrs[d][rss[d], :]) for d in dirs]
        b = [_dot(tris[d], las[d][0]) + _dot(tris[d], las[d][1]) + _dot(tris[d], las[d][2])
             for d in dirs]
        be = [_dot(ones, las[d][0]) + _dot(ones, las[d][1]) + _dot(ones, las[d][2])
              for d in dirs]
        v = [v_ref[rss[d], :].astype(BF16) for d in dirs]
        qd = [(q_ref[rss[d], :] * scale * jnp.exp(b[d])).astype(BF16) for d in dirs]
        ki = [(k_ref[rss[d], :] * jnp.exp(-b[d])).astype(BF16) for d in dirs]
        ks = [(k_ref[rss[d], :] * jnp.exp(be[d] - b[d])).astype(BF16) for d in dirs]
        att = [jnp.where(masks[d], _dot(qd[d], ki[d], NT), 0.0).astype(BF16) for d in dirs]
        o_intra = [_dot(att[d], v[d]) for d in dirs]
        dec = [jnp.exp(be[d]) for d in dirs]
        st = [s_scr[d] for d in dirs]
        inter = [[None] * per_group for _ in dirs]
        for step in range(per_group):
            for d in dirs:
                ch = orders[d][step]
                cs = slice(ch * c, (ch + 1) * c)
                inter[d][ch] = _dot(qd[d][cs, :], st[d].astype(BF16), NT)
                st[d] = st[d] * dec[d][ch * c:ch * c + 1, :] + _dot(v[d][cs, :], ks[d][cs, :], TN)
        for d in dirs:
            s_scr[d] = st[d]
            out_scrs[d][rss[d], :] = o_intra[d] + jnp.concatenate(inter[d], axis=0)
        return carry

    lax.fori_loop(0, n_groups, body, 0)

    o = _rms(of_scr[...] + ob_scr[...]) * nw_ref[...]
    o_ref[...] = (o * _silu(g_ref[...])).astype(o_ref.dtype)
    if want_state:
        sf_ref[0] = s_scr[0].T
        sb_ref[0] = s_scr[1].T


def _gla_call(z, seq, n_seq, row_block0, wgf, bgf, wgb, bgb, nw, s0=None, want_state=False):
    has_s0 = s0 is not None

    def col(cb):
        return pl.BlockSpec((seq, LANE), lambda s, h: (row_block0 + s, cb + h))

    in_specs = [col(0), col(H_A), col(2 * H_A), col(3 * H_A),
                pl.BlockSpec((seq, LANE), lambda s, h: (row_block0 + s, (4 * W_A + 2 * W_B) // LANE)),
                pl.BlockSpec((LANE, DK_A), lambda s, h: (0, h)),
                pl.BlockSpec((1, DK_A), lambda s, h: (0, h)),
                pl.BlockSpec((LANE, DK_A), lambda s, h: (0, h)),
                pl.BlockSpec((1, DK_A), lambda s, h: (0, h)),
                pl.BlockSpec((1, DV_A), lambda s, h: (0, 0))]
    args = [z, z, z, z, z, wgf, bgf, wgb, bgb, nw]
    if has_s0:
        in_specs += [pl.BlockSpec((1, DK_A, DV_A), lambda s, h: ((s * 2 + 0) * H_A + h, 0, 0)),
                     pl.BlockSpec((1, DK_A, DV_A), lambda s, h: ((s * 2 + 1) * H_A + h, 0, 0))]
        args += [s0, s0]
    out_shape = [jax.ShapeDtypeStruct((n_seq * seq, W_A), BF16)]
    out_specs = [pl.BlockSpec((seq, LANE), lambda s, h: (s, h))]
    if want_state:
        out_shape += [jax.ShapeDtypeStruct((n_seq * H_A, DK_A, DV_A), F32)] * 2
        out_specs += [pl.BlockSpec((1, DK_A, DV_A), lambda s, h: (s * H_A + h, 0, 0))] * 2
    return pl.pallas_call(
        functools.partial(_gla_kernel, has_s0=has_s0, want_state=want_state),
        out_shape=out_shape,
        grid=(n_seq, H_A),
        in_specs=in_specs,
        out_specs=out_specs,
        scratch_shapes=[pltpu.VMEM((seq, DK_A), F32), pltpu.VMEM((seq, DK_A), F32),
                        pltpu.VMEM((seq, DV_A), F32), pltpu.VMEM((seq, DV_A), F32),
                        pltpu.VMEM((2, DK_A, DV_A), F32)],
        compiler_params=_cparams(2),
        name="gla_seq%d" % seq,
    )(*args)


def _sgu_kernel(u_ref, v_ref, nw_ref, ws_ref, bs_ref, o_ref):
    tm = u_ref.shape[0]
    u = _gelu(u_ref[...])
    v = (_rms(_gelu(v_ref[...])) * nw_ref[...]).astype(BF16)
    ws = ws_ref[0]
    bs = bs_ref[0]
    for cidx in range(tm // CHUNK_B):
        rs = slice(cidx * CHUNK_B, (cidx + 1) * CHUNK_B)
        sg = _dot(ws, v[rs, :]) + bs
        o_ref[rs, :] = (u[rs, :] * sg).astype(o_ref.dtype)


def _sgu_call(z, nw, ws_bf, bs_full, tm=512):
    t = z.shape[0]
    ub0 = (4 * W_A) // LANE
    vb0 = (4 * W_A + W_B) // LANE
    return pl.pallas_call(
        _sgu_kernel,
        out_shape=jax.ShapeDtypeStruct((t, W_B), BF16),
        grid=(t // tm, H_B),
        in_specs=[
            pl.BlockSpec((tm, LANE), lambda i, h: (i, ub0 + h)),
            pl.BlockSpec((tm, LANE), lambda i, h: (i, vb0 + h)),
            pl.BlockSpec((1, DB), lambda i, h: (0, 0)),
            pl.BlockSpec((1, CHUNK_B, CHUNK_B), lambda i, h: (h, 0, 0)),
            pl.BlockSpec((1, CHUNK_B, DB), lambda i, h: (h, 0, 0)),
        ],
        out_specs=pl.BlockSpec((tm, LANE), lambda i, h: (i, h)),
        compiler_params=_cparams(2),
        name="sgu",
    )(z, z, nw, ws_bf, bs_full)


def _outproj_kernel(a1p_ref, a1s_ref, a2p_ref, a2s_ref, w_ref, x_ref, g_ref, o_ref, *, n_prompt_tiles):
    n1 = a1p_ref.shape[1]

    def run(a1_ref, a2_ref):
        y = _dot(a1_ref[...], w_ref[:n1, :]) + _dot(a2_ref[...], w_ref[n1:, :])
        o_ref[...] = x_ref[...] + g_ref[0, 0] * y

    @pl.when(pl.program_id(0) < n_prompt_tiles)
    def _():
        run(a1p_ref, a2p_ref)

    @pl.when(pl.program_id(0) >= n_prompt_tiles)
    def _():
        run(a1s_ref, a2s_ref)


def _outproj_call(a1, a2, w_bf, x, mod4, layer, rows, tm=512):
    t, d = x.shape
    npt = rows.n_prompt_rows // tm

    def parts(a):
        if isinstance(a, tuple):
            n = a[0].shape[1]
            return (a[0], a[1],
                    pl.BlockSpec((tm, n), lambda i: (jnp.minimum(i, npt - 1), 0)),
                    pl.BlockSpec((tm, n), lambda i: (jnp.maximum(i - npt, 0), 0)))
        spec = pl.BlockSpec((tm, a.shape[1]), lambda i: (i, 0))
        return a, a, spec, spec

    a1p, a1s, s1p, s1s = parts(a1)
    a2p, a2s, s2p, s2s = parts(a2)
    n1, n2 = a1p.shape[1], a2p.shape[1]
    return pl.pallas_call(
        functools.partial(_outproj_kernel, n_prompt_tiles=npt),
        out_shape=jax.ShapeDtypeStruct((t, d), F32),
        grid=(t // tm,),
        in_specs=[
            s1p, s1s, s2p, s2s,
            pl.BlockSpec((n1 + n2, d), lambda i: (0, 0)),
            pl.BlockSpec((tm, d), lambda i: (i, 0)),
            rows.mod_spec(layer, 2, tm),
        ],
        out_specs=pl.BlockSpec((tm, d), lambda i: (i, 0)),
        compiler_params=_cparams(1),
        name="outproj",
    )(a1p, a1s, a2p, a2s, w_bf, x, mod4)


def _embed_kernel(xp_ref, xs_ref, pe_ref, o_ref, *, n_prompt_tiles):
    @pl.when(pl.program_id(0) < n_prompt_tiles)
    def _():
        o_ref[...] = xp_ref[...]

    @pl.when(pl.program_id(0) >= n_prompt_tiles)
    def _():
        o_ref[...] = xs_ref[...] + pe_ref[...]


def _embed_call(xp, xs, pe, tm=512):
    tp, d = xp.shape
    ts = xs.shape[0]
    npt = tp // tm
    pe_tiles = pe.shape[0] // tm
    return pl.pallas_call(
        functools.partial(_embed_kernel, n_prompt_tiles=npt),
        out_shape=jax.ShapeDtypeStruct((tp + ts, d), F32),
        grid=((tp + ts) // tm,),
        in_specs=[
            pl.BlockSpec((tm, d), lambda i: (jnp.minimum(i, npt - 1), 0)),
            pl.BlockSpec((tm, d), lambda i: (jnp.maximum(i - npt, 0), 0)),
            pl.BlockSpec((tm, d), lambda i: (jnp.maximum(i - npt, 0) % pe_tiles, 0)),
        ],
        out_specs=pl.BlockSpec((tm, d), lambda i: (i, 0)),
        compiler_params=_cparams(1),
        name="embed",
    )(xp, xs, pe)


def _conv_kernel(bg_ref, cg_ref, xi_ref, w_ref, o_ref):
    seq = bg_ref.shape[0]
    t = cg_ref[...] * xi_ref[...]
    row = lax.broadcasted_iota(jnp.int32, t.shape, 0)
    t_prev = jnp.where(row == 0, 0.0, pltpu.roll(t, 1, 0))
    t_next = jnp.where(row == seq - 1, 0.0, pltpu.roll(t, seq - 1, 0))
    conv = t_prev * w_ref[0:1, :] + t * w_ref[1:2, :] + t_next * w_ref[2:3, :]
    o_ref[...] = (bg_ref[...] * conv).astype(o_ref.dtype)


def _conv_call(z, seq, n_seq, row_block0, conv_w):
    def col(cb):
        return pl.BlockSpec((seq, LANE), lambda s, j: (row_block0 + s, cb + j))
    return pl.pallas_call(
        _conv_kernel,
        out_shape=jax.ShapeDtypeStruct((n_seq * seq, W_C), BF16),
        grid=(n_seq, W_C // LANE),
        in_specs=[col(0), col(H_C), col(2 * H_C),
                  pl.BlockSpec((3, LANE), lambda s, j: (0, j))],
        out_specs=pl.BlockSpec((seq, LANE), lambda s, j: (s, j)),
        compiler_params=_cparams(2),
        name="conv_seq%d" % seq,
    )(z, z, z, conv_w)


def _fft_kernel(f_ref, cd_ref, sd_ref, cl_ref, sl_ref, o_ref, p_scr, q_scr, *, scale):
    @pl.when(pl.program_id(1) == 0)
    def _():
        cd = cd_ref[...]
        sd = sd_ref[...]
        for h in range(H_D):
            cs = slice(h * DD, (h + 1) * DD)
            fh = f_ref[:, cs].astype(BF16)
            p_scr[:, cs] = _dot(fh, cd).astype(BF16)
            q_scr[:, cs] = _dot(fh, sd).astype(BF16)

    y = _dot(cl_ref[...], p_scr[...]) - _dot(sl_ref[...], q_scr[...])
    o_ref[...] = (y * scale).astype(o_ref.dtype)


def _dft_mats(n):
    idx = jnp.arange(n, dtype=jnp.int32)
    ang = ((idx[:, None] * idx[None, :]) % n).astype(F32) * (2.0 * math.pi / n)
    return jnp.cos(ang).astype(BF16), jnp.sin(ang).astype(BF16)


def _fft_call(z, seq, n_seq, row_block0, cd, sd, cl, sl):
    tl = min(seq, 512)
    fcol = (3 * W_C) // W_D
    return pl.pallas_call(
        functools.partial(_fft_kernel, scale=1.0 / math.sqrt(seq * DD)),
        out_shape=jax.ShapeDtypeStruct((n_seq * seq, W_D), BF16),
        grid=(n_seq, seq // tl),
        in_specs=[
            pl.BlockSpec((seq, W_D), lambda s, i: (row_block0 + s, fcol)),
            pl.BlockSpec((DD, DD), lambda s, i: (0, 0)),
            pl.BlockSpec((DD, DD), lambda s, i: (0, 0)),
            pl.BlockSpec((tl, seq), lambda s, i: (i, 0)),
            pl.BlockSpec((tl, seq), lambda s, i: (i, 0)),
        ],
        out_specs=pl.BlockSpec((tl, W_D), lambda s, i: (s * (seq // tl) + i, 0)),
        scratch_shapes=[pltpu.VMEM((seq, W_D), BF16), pltpu.VMEM((seq, W_D), BF16)],
        compiler_params=_cparams(2),
        name="fft_seq%d" % seq,
    )(z, cd, sd, cl, sl)


def _topk16(s):
    nrows = s.shape[0]
    iota = lax.broadcasted_iota(jnp.int32, s.shape, 0)
    i16 = lax.broadcasted_iota(jnp.int32, (PEER_TOPK, s.shape[1]), 0)

    def body(k, carry):
        s, rank, top = carry
        m = jnp.max(s, axis=0, keepdims=True)
        idx = jnp.min(jnp.where(s == m, iota, nrows), axis=0, keepdims=True)
        sel = iota == idx
        rank = jnp.where(sel, k, rank)
        s = jnp.where(sel, -jnp.inf, s)
        top = jnp.where(i16 == k, m, top)
        return s, rank, top

    init = (s, jnp.full(s.shape, PEER_TOPK, jnp.int32), jnp.zeros((PEER_TOPK, s.shape[1]), F32))
    _, rank, top = lax.fori_loop(0, PEER_TOPK, body, init)
    return rank, top


_CAND_GROUPS = [(0, 16), (1, 8)] + [(k1, 8) for k1 in range(2, 8)]
_CAND_SINGLE0 = sum(n for _, n in _CAND_GROUPS)


_SCORE_FLOOR = -(2.0 ** 126)
_CODE_UNIT = 2.0 ** 127


def _rank_code(k):
    return -_CODE_UNIT * (1.0 + (k + 1) / 32.0)


def _topk16_untied(scores):
    n = scores[0].shape[1]
    i16 = lax.broadcasted_iota(jnp.int32, (PEER_TOPK, n), 0)
    keys = [jnp.maximum(s, _SCORE_FLOOR) for s in scores]
    tops = [jnp.zeros((PEER_TOPK, n), F32) for _ in scores]
    for k in range(PEER_TOPK):
        for c in range(len(keys)):
            m = jnp.max(keys[c], axis=0, keepdims=True)
            keys[c] = jnp.where(keys[c] == m, _rank_code(k), keys[c])
            tops[c] = jnp.where(i16 == k, m, tops[c])
    out = []
    for key, top in zip(keys, tops):
        taken = key < _SCORE_FLOOR
        rank = jnp.where(taken, key * -(32.0 / _CODE_UNIT) - 33.0, float(PEER_TOPK))
        count = jnp.sum(jnp.where(taken, 1.0, 0.0), axis=0, keepdims=True)
        clamped = jnp.sum(jnp.where(top <= _SCORE_FLOOR, 1.0, 0.0), axis=0, keepdims=True)
        out.append((rank, top, (count - PEER_TOPK) + clamped))
    return out


def _cand_scores(t1, t2):
    n = t1.shape[1]
    neg = jnp.full((8, n), -jnp.inf, F32)
    row8 = lax.broadcasted_iota(jnp.int32, (8, n), 0)
    pieces = []
    for k1, nrow in _CAND_GROUPS:
        blk = t1[k1:k1 + 1, :] + t2[0:nrow, :]
        nvalid = PEER_TOPK // (k1 + 1)
        if nvalid < nrow:
            blk = jnp.where(row8 < nvalid, blk, neg)
        pieces.append(blk)
    pieces.append(t1[8:16, :] + t2[0:1, :])
    return jnp.concatenate(pieces, axis=0)


def _psel_finish(s1, s2, r1, t1, t2, cand, sel):
    n = s1.shape[1]
    cmax = t1[0:1, :] + t2[0:1, :]
    z = jnp.sum(jnp.where(sel, jnp.exp(cand - cmax), 0.0), axis=0, keepdims=True)
    self32 = jnp.where(sel, 1.0, 0.0)
    cut = jnp.zeros((N_KEYS, n), F32)
    r0 = 0
    for k1, nrow in _CAND_GROUPS:
        cnt = jnp.sum(self32[r0:r0 + nrow, :], axis=0, keepdims=True)
        cut = jnp.where(r1 == k1, cnt, cut)
        r0 += nrow
    for j in range(8):
        cut = jnp.where(r1 == 8 + j, self32[r0 + j:r0 + j + 1, :], cut)
    e1 = jnp.exp(s1 - t1[0:1, :]) * (1.0 / z)
    e2 = jnp.exp(s2 - t2[0:1, :])
    return cut, e1, e2


def _psel_kernel(x_ref, nw_ref, sh_ref, sc_ref, wq_ref, sk_ref,
                 h_ref, r2_ref, e2_ref, cut_ref, e1_ref, q_scr):
    tq = x_ref.shape[0]
    n_chunks = tq // LANE
    hb = _modnorm(x_ref[...], nw_ref[...], sh_ref[0, 0], sc_ref[0, 0]).astype(BF16)
    h_ref[...] = hb
    q_scr[...] = _dot(wq_ref[...], hb, NT)
    half = D_KEY // 2

    def scores(hh):
        q1 = q_scr[pl.ds(pl.multiple_of(hh * D_KEY, D_KEY), half), :]
        q2 = q_scr[pl.ds(pl.multiple_of(hh * D_KEY + half, half), half), :]
        return _dot3(sk_ref[0], q1), _dot3(sk_ref[1], q2)

    def write(hh, ls, r2, cut, e1, e2):
        r2_ref[hh, :, ls] = r2.astype(BF16)
        e2_ref[hh, :, ls] = e2.astype(BF16)
        cut_ref[hh, :, ls] = cut
        e1_ref[hh, :, ls] = e1

    def exact_chunk(hh, ls, s1c, s2c):
        r1, t1 = _topk16(s1c)
        r2, t2 = _topk16(s2c)
        cand = _cand_scores(t1, t2)
        crank, _ = _topk16(cand)
        cut, e1, e2 = _psel_finish(s1c, s2c, r1.astype(F32), t1, t2, cand, crank < PEER_TOPK)
        write(hh, ls, r2.astype(F32), cut, e1, e2)

    def head(hh, carry):
        s1, s2 = scores(hh)
        halves = []
        for lc in range(n_chunks):
            ls = slice(lc * LANE, (lc + 1) * LANE)
            halves.append(_topk16_untied([s1[:, ls], s2[:, ls]]))
        cands = [_cand_scores(h1[1], h2[1]) for h1, h2 in halves]
        picked = _topk16_untied(cands)
        for lc in range(n_chunks):
            ls = slice(lc * LANE, (lc + 1) * LANE)
            (r1, t1, c1), (r2, t2, c2) = halves[lc]
            crank, _, cc = picked[lc]
            cut, e1, e2 = _psel_finish(s1[:, ls], s2[:, ls], r1, t1, t2, cands[lc], crank < PEER_TOPK)
            write(hh, ls, r2, cut, e1, e2)
            extra = c1 + c2 + cc

            @pl.when(jnp.max(extra) > 0)
            def _():
                exact_chunk(hh, ls, s1[:, ls], s2[:, ls])
        return carry

    lax.fori_loop(0, PEER_HEADS, head, 0)


def _psel_call(x, nw, mod4, layer, rows, wqt_bf, sub_keys, tq=256):
    t, d = x.shape
    nq = wqt_bf.shape[0]
    sel_shape = jax.ShapeDtypeStruct((PEER_HEADS, N_KEYS, t), F32)
    sel_bf = jax.ShapeDtypeStruct((PEER_HEADS, N_KEYS, t), BF16)
    sel_spec = pl.BlockSpec((PEER_HEADS, N_KEYS, tq), lambda i: (0, 0, i))
    return pl.pallas_call(
        _psel_kernel,
        out_shape=[jax.ShapeDtypeStruct((t, d), BF16), sel_bf, sel_bf, sel_shape, sel_shape],
        grid=(t // tq,),
        in_specs=[
            pl.BlockSpec((tq, d), lambda i: (i, 0)),
            pl.BlockSpec((1, d), lambda i: (0, 0)),
            rows.mod_spec(layer, 3, tq),
            rows.mod_spec(layer, 4, tq),
            pl.BlockSpec((nq, d), lambda i: (0, 0)),
            pl.BlockSpec((2, N_KEYS, D_KEY // 2), lambda i: (0, 0, 0)),
        ],
        out_specs=[pl.BlockSpec((tq, d), lambda i: (i, 0)), sel_spec, sel_spec, sel_spec, sel_spec],
        scratch_shapes=[pltpu.VMEM((nq, tq), F32)],
        compiler_params=_cparams(1),
        name="peer_select",
    )(x, nw.reshape(1, d), mod4, mod4, wqt_bf, sub_keys)


def _gelu_sigmoid_form(x):
    k = 2.0 * math.sqrt(2.0 / math.pi) * math.log2(math.e)
    u = x * (-k - (k * 0.044715) * (x * x))
    return x * (1.0 / (1.0 + jnp.exp2(u)))


def _pdense_kernel(h_ref, u_ref, v_ref, r2_ref, e2_ref, cut_ref, e1_ref, x_ref, g_ref, o_ref,
                   hid0, hid1, w0, w1, acc_scr, *, nj, n_real):
    s = pl.program_id(0)
    s3 = jnp.clip(s - 2, 0, n_real - 1)
    j3 = s3 % nj
    te, tq = hid0.shape
    bzero = jnp.zeros((), BF16)

    @pl.when(s == 0)
    def _():
        for r in (hid0, hid1, w0, w1):
            r[...] = jnp.zeros(r.shape, r.dtype)

    @pl.when(j3 == 0)
    def _():
        acc_scr[...] = jnp.zeros(acc_scr.shape, F32)

    def step(hid_new, hid_old, w_new, w_old):
        wide = 2 * LANE
        n_lc = tq // wide
        dcols = acc_scr.shape[1] // n_lc
        for a in range(te // N_KEYS):
            rs = slice(a * N_KEYS, (a + 1) * N_KEYS)
            rs2 = slice((a - a % 2) * N_KEYS, (a - a % 2 + 2) * N_KEYS)
            for lc in range(n_lc):
                ls = slice(lc * wide, (lc + 1) * wide)
                gate = None
                for hh in range(PEER_HEADS):
                    cut = jnp.broadcast_to(cut_ref[hh, a:a + 1, ls], (N_KEYS, wide)).astype(BF16)
                    e1 = jnp.broadcast_to(e1_ref[hh, a:a + 1, ls], (N_KEYS, wide)).astype(BF16)
                    term = jnp.where(r2_ref[hh, :, ls] < cut, e2_ref[hh, :, ls] * e1, bzero)
                    gate = term if gate is None else gate + term
                w_new[rs, ls] = _gelu_sigmoid_form(hid_old[rs, ls].astype(BF16)) * gate
                if a % 2 == 0:
                    hid_new[rs2, ls] = _dot(u_ref[rs2, :], h_ref[ls, :], NT)
                else:
                    ds_ = slice(lc * dcols, (lc + 1) * dcols)
                    acc_scr[:, ds_] += _dot(w_old[rs2, :], v_ref[rs2, ds_], TN)

    @pl.when(s % 2 == 0)
    def _():
        step(hid0, hid1, w0, w1)

    @pl.when(s % 2 == 1)
    def _():
        step(hid1, hid0, w1, w0)

    @pl.when(jnp.logical_and(j3 == nj - 1, s >= 2))
    def _():
        o_ref[...] = x_ref[...] + g_ref[0, 0] * acc_scr[...]


def _pdense_call(h_bf, u_bf, v_bf, r2, e2, cut, e1, x, mod4, layer, rows, tq=512, te=2048):
    t, d = x.shape
    ne = u_bf.shape[1]
    ni, nj = t // tq, ne // te
    n_real = ni * nj
    na = te // N_KEYS

    def tile(shift):
        def f(s):
            ss = jnp.clip(s - shift, 0, n_real - 1)
            return ss // nj, ss % nj
        return f

    t1, t2, t3 = tile(0), tile(1), tile(2)
    return pl.pallas_call(
        functools.partial(_pdense_kernel, nj=nj, n_real=n_real),
        out_shape=jax.ShapeDtypeStruct((t, d), F32),
        grid=(n_real + 2,),
        in_specs=[
            pl.BlockSpec((tq, d), lambda s: (t1(s)[0], 0)),
            pl.BlockSpec((None, te, d), lambda s: (layer, t1(s)[1], 0)),
            pl.BlockSpec((None, te, d), lambda s: (layer, t3(s)[1], 0)),
            pl.BlockSpec((PEER_HEADS, N_KEYS, tq), lambda s: (0, 0, t2(s)[0])),
            pl.BlockSpec((PEER_HEADS, N_KEYS, tq), lambda s: (0, 0, t2(s)[0])),
            pl.BlockSpec((PEER_HEADS, na, tq), lambda s: (0, t2(s)[1], t2(s)[0])),
            pl.BlockSpec((PEER_HEADS, na, tq), lambda s: (0, t2(s)[1], t2(s)[0])),
            pl.BlockSpec((tq, d), lambda s: (t3(s)[0], 0)),
            rows.mod_spec(layer, 5, tq, tile_of=lambda s: t3(s)[0]),
        ],
        out_specs=pl.BlockSpec((tq, d), lambda s: (t3(s)[0], 0)),
        scratch_shapes=[pltpu.VMEM((te, tq), F32), pltpu.VMEM((te, tq), F32),
                        pltpu.VMEM((te, tq), BF16), pltpu.VMEM((te, tq), BF16),
                        pltpu.VMEM((tq, d), F32)],
        compiler_params=_cparams(1),
        name="peer_dense",
    )(h_bf, u_bf, v_bf, r2, e2, cut, e1, x, mod4)


def _final_kernel(x_ref, w_ref, o_ref):
    o_ref[...] = _rms(x_ref[...]) * w_ref[...]


def _final_call(x, w, row_block0, n_rows, tm=512):
    d = x.shape[1]
    return pl.pallas_call(
        _final_kernel,
        out_shape=jax.ShapeDtypeStruct((n_rows, d), F32),
        grid=(n_rows // tm,),
        in_specs=[pl.BlockSpec((tm, d), lambda i: (row_block0 + i, 0)),
                  pl.BlockSpec((1, d), lambda i: (0, 0))],
        out_specs=pl.BlockSpec((tm, d), lambda i: (i, 0)),
        compiler_params=_cparams(1),
        name="final_norm",
    )(x, w.reshape(1, d))


def _grid_pos_embed(rows, dtype):
    t = jnp.arange(rows * GRID_W)
    r = (t // GRID_W).astype(F32)
    col = (t % GRID_W).astype(F32)
    nf = D_MODEL // 4
    freqs = 1.0 / (10000.0 ** (jnp.arange(nf, dtype=F32) / nf))

    def emb(p):
        a = p[:, None] * freqs[None, :]
        return jnp.concatenate([jnp.sin(a), jnp.cos(a)], axis=-1)
    return jnp.concatenate([emb(r), emb(col)], axis=-1).astype(dtype)


def _even_w_in_layout(w):
    d = w.shape[0]
    a_end = 4 * W_A
    lr = w[:, a_end:a_end + GATE_RANK]
    uv = w[:, a_end + GATE_RANK:]
    pad = jnp.zeros((d, LANE - GATE_RANK), w.dtype)
    return jnp.concatenate([w[:, :a_end], uv, lr, pad], axis=1)


def _pad_gate_w(w):
    return jnp.concatenate([w, jnp.zeros((LANE - GATE_RANK, w.shape[1]), w.dtype)], axis=0)


def kernel(x_prompt, x_sample, state_gla, c, c_ctx, w_mod, b_mod, norm_w, final_norm_w, even_w_in, even_w_gate_f, even_b_gate_f, even_w_gate_b, even_b_gate_b, gla_norm_w, sgu_norm_w, sgu_w_s, sgu_b_s, even_w_out, odd_w_in, odd_conv_w, odd_w_out, peer_w_q, peer_sub_keys, peer_u, peer_v):
    bp, lp, d = x_prompt.shape
    bs, ls, _ = x_sample.shape
    depth = w_mod.shape[0]
    n_even = even_w_in.shape[0]
    tp = bp * lp
    ts = bs * ls
    rows = _Rows(tp, ls)

    x = _embed_call(x_prompt.reshape(tp, d), x_sample.reshape(ts, d),
                    _grid_pos_embed(ls // GRID_W, x_sample.dtype))

    cc = jnp.concatenate([c_ctx[None], c, jnp.zeros((MOD_ROWS - 1 - bs, d), F32)], axis=0)
    mod4 = _mod_call(cc, w_mod, b_mod).reshape(depth, MOD_ROWS, 1, 6 * d)

    cd, sd = _dft_mats(DD)
    clp, slp = _dft_mats(lp)
    cls, sls = _dft_mats(ls)
    s0_all = state_gla.reshape(bs, n_even, 2 * H_A, DK_A, DV_A)
    u_all = peer_u.astype(BF16)
    v_all = peer_v.astype(BF16)

    new_states = []
    for l in range(depth):
        if l % 2 == 0:
            e = l // 2
            z = _normproj_call(x, norm_w[l, 0], mod4, l, rows, _even_w_in_layout(even_w_in[e]).astype(BF16))
            gate_args = (_pad_gate_w(even_w_gate_f[e]), even_b_gate_f[e][None],
                         _pad_gate_w(even_w_gate_b[e]), even_b_gate_b[e][None], gla_norm_w[e][None])
            o_p, sf, sb = _gla_call(z, lp, bp, 0, *gate_args, want_state=True)
            (o_s,) = _gla_call(z, ls, bs, tp // ls, *gate_args,
                               s0=s0_all[:, e].reshape(bs * 2 * H_A, DK_A, DV_A))
            a1 = (o_p, o_s)
            bs_full = jnp.broadcast_to(sgu_b_s[e][:, :, None], (H_B, CHUNK_B, DB))
            a2 = _sgu_call(z, sgu_norm_w[e][None], sgu_w_s[e].astype(BF16), bs_full)
            x = _outproj_call(a1, a2, even_w_out[e].astype(BF16), x, mod4, l, rows)
            sfb = jnp.stack([sf.reshape(bp, H_A, DK_A, DV_A), sb.reshape(bp, H_A, DK_A, DV_A)], axis=1)
            new_states.append(sfb.astype(x_prompt.dtype))
        else:
            o = l // 2
            z = _normproj_call(x, norm_w[l, 0], mod4, l, rows, odd_w_in[o].astype(BF16))
            yc = (_conv_call(z, lp, bp, 0, odd_conv_w[o]), _conv_call(z, ls, bs, tp // ls, odd_conv_w[o]))
            yd = (_fft_call(z, lp, bp, 0, cd, sd, clp, slp), _fft_call(z, ls, bs, tp // ls, cd, sd, cls, sls))
            x = _outproj_call(yc, yd, odd_w_out[o].astype(BF16), x, mod4, l, rows)
        h_bf, r2, e2, cut, e1 = _psel_call(x, norm_w[l, 1], mod4, l, rows,
                                           peer_w_q[l].T.astype(BF16), peer_sub_keys[l])
        x = _pdense_call(h_bf, u_all, v_all, r2, e2, cut, e1, x, mod4, l, rows)

    y_prompt = _final_call(x, final_norm_w, 0, tp).reshape(bp, lp, d)
    y_sample = _final_call(x, final_norm_w, tp // 512, ts).reshape(bs, ls, d)
    state_new = jnp.stack(new_states, axis=1)
    return (y_prompt, y_sample, state_new)
```

```python
import functools
import math

import jax
import jax.numpy as jnp
import numpy as np
from jax import lax
from jax.experimental import pallas as pl
from jax.experimental.pallas import tpu as pltpu

F32 = jnp.float32
BF16 = jnp.bfloat16

D_MODEL = 1024
DEPTH = 4
GRID_W = 64
RMS_EPS = 1e-6
H_A = 4
DK_A = 128
DV_A = 128
GATE_RANK = 16
GATE_NORM = 16.0
GLA_CHUNK = 64
W_A = H_A * DV_A
H_B = 4
DB = 128
CHUNK_B = 128
W_B = H_B * DB
H_C = 4
DC = 128
W_C = H_C * DC
H_D = 4
DD = 128
W_D = H_D * DD
N_KEYS = 128
N_EXPERTS = N_KEYS * N_KEYS
PEER_HEADS = 8
PEER_TOPK = 16
D_KEY = 256

LANE = 128
MOD_ROWS = 8
EVEN_IN_PAD = 4 * W_A + 2 * W_B + LANE
VMEM_LIMIT = 48 * 1024 * 1024

NN = (((1,), (0,)), ((), ()))
NT = (((1,), (1,)), ((), ()))
TN = (((0,), (0,)), ((), ()))


def _cparams(n_axes, flags=None):
    return pltpu.CompilerParams(dimension_semantics=("arbitrary",) * n_axes,
                                vmem_limit_bytes=VMEM_LIMIT, flags=flags)


def _dot(a, b, dims=NN):
    return lax.dot_general(a, b, dims, preferred_element_type=F32)


def _split2(x):
    hi = x.astype(BF16)
    lo = (x - hi.astype(F32)).astype(BF16)
    return hi, lo


def _split3(x):
    hi = x.astype(BF16)
    r = x - hi.astype(F32)
    mid = r.astype(BF16)
    lo = (r - mid.astype(F32)).astype(BF16)
    return hi, mid, lo


def _dot3(a, b, dims=NN):
    ah, al = _split2(a)
    bh, bl = _split2(b)
    return _dot(ah, bh, dims) + _dot(al, bh, dims) + _dot(ah, bl, dims)


def _silu(x):
    return x * (1.0 / (1.0 + jnp.exp(-x)))


def _gelu(x):
    c = math.sqrt(2.0 / math.pi)
    return x * (0.5 * (1.0 + jnp.tanh(c * (x + 0.044715 * (x * x * x)))))


def _log_sigmoid(x):
    return jnp.minimum(x, 0.0) - jnp.log(1.0 + jnp.exp(-jnp.abs(x)))


def _rms(x):
    return x * lax.rsqrt(jnp.mean(x * x, axis=-1, keepdims=True) + RMS_EPS)


def _modnorm(x, nw, shift, scale):
    return (_rms(x) * nw) * (1.0 + scale) + shift


def _mod_kernel(cc_ref, w_ref, b_ref, o_ref):
    a = _silu(cc_ref[...])
    o_ref[0] = _dot3(a, w_ref[0]) + b_ref[0]


def _mod_call(cc, w_mod, b_mod):
    depth, d, n = w_mod.shape
    tn = 1024
    return pl.pallas_call(
        _mod_kernel,
        out_shape=jax.ShapeDtypeStruct((depth, MOD_ROWS, n), F32),
        grid=(depth, n // tn),
        in_specs=[
            pl.BlockSpec((MOD_ROWS, d), lambda l, j: (0, 0)),
            pl.BlockSpec((1, d, tn), lambda l, j: (l, 0, j)),
            pl.BlockSpec((1, 1, tn), lambda l, j: (l, 0, j)),
        ],
        out_specs=pl.BlockSpec((1, MOD_ROWS, tn), lambda l, j: (l, 0, j)),
        compiler_params=_cparams(2),
        name="mod_rows",
    )(cc, w_mod, b_mod.reshape(depth, 1, n))


class _Rows:
    def __init__(self, n_prompt_rows, dec_seq):
        self.n_prompt_rows = n_prompt_rows
        self.dec_seq = dec_seq

    def mod_row(self, i, tm):
        npt = self.n_prompt_rows // tm
        tps = self.dec_seq // tm
        return jnp.where(i < npt, 0, 1 + (i - npt) // tps)

    def mod_spec(self, layer, section, tm, tile_of=lambda i, *_: i):
        return pl.BlockSpec((1, 1, 1, D_MODEL),
                            lambda *g: (layer, self.mod_row(tile_of(*g), tm), 0, section))


def _normproj_kernel(x_ref, nw_ref, sh_ref, sc_ref, w_ref, o_ref):
    h = _modnorm(x_ref[...], nw_ref[...], sh_ref[0, 0], sc_ref[0, 0])
    o_ref[...] = _dot(h.astype(BF16), w_ref[...])


def _normproj_call(x, nw, mod4, layer, rows, w_bf, tm=256):
    t, d = x.shape
    n = w_bf.shape[1]
    return pl.pallas_call(
        _normproj_kernel,
        out_shape=jax.ShapeDtypeStruct((t, n), F32),
        grid=(t // tm,),
        in_specs=[
            pl.BlockSpec((tm, d), lambda i: (i, 0)),
            pl.BlockSpec((1, d), lambda i: (0, 0)),
            rows.mod_spec(layer, 0, tm),
            rows.mod_spec(layer, 1, tm),
            pl.BlockSpec((d, n), lambda i: (0, 0)),
        ],
        out_specs=pl.BlockSpec((tm, n), lambda i: (i, 0)),
        compiler_params=_cparams(1),
        name="normproj",
    )(x, nw.reshape(1, d), mod4, mod4, w_bf)


def _gla_kernel(*refs, has_s0, want_state):
    (q_ref, k_ref, v_ref, g_ref, lr_ref, wgf_ref, bgf_ref, wgb_ref, bgb_ref, nw_ref) = refs[:10]
    pos = 10
    if has_s0:
        s0f_ref, s0b_ref = refs[pos:pos + 2]
        pos += 2
    o_ref = refs[pos]
    pos += 1
    if want_state:
        sf_ref, sb_ref = refs[pos:pos + 2]
        pos += 2
    laf_scr, lab_scr, of_scr, ob_scr, s_scr = refs[pos:]

    seq = q_ref.shape[0]
    c = GLA_CHUNK
    per_group = 4
    gw = per_group * c
    n_groups = seq // gw
    scale = DK_A ** -0.5

    lr = lr_ref[...]
    laf_scr[...] = _log_sigmoid(_dot3(lr, wgf_ref[...]) + bgf_ref[...]) * (1.0 / GATE_NORM)
    lab_scr[...] = _log_sigmoid(_dot3(lr, wgb_ref[...]) + bgb_ref[...]) * (1.0 / GATE_NORM)
    if has_s0:
        s_scr[0] = s0f_ref[0].T
        s_scr[1] = s0b_ref[0].T
    else:
        s_scr[...] = jnp.zeros(s_scr.shape, F32)

    ri = lax.broadcasted_iota(jnp.int32, (gw, gw), 0)
    ci = lax.broadcasted_iota(jnp.int32, (gw, gw), 1)
    same = (ri // c) == (ci // c)
    lower = jnp.logical_and(same, ci <= ri)
    upper = jnp.logical_and(same, ci >= ri)
    tril = jnp.where(lower, 1.0, 0.0).astype(BF16)
    triu = jnp.where(upper, 1.0, 0.0).astype(BF16)
    ones = jnp.where(same, 1.0, 0.0).astype(BF16)

    la_scrs = (laf_scr, lab_scr)
    tris = (tril, triu)
    masks = (lower, upper)
    out_scrs = (of_scr, ob_scr)
    orders = (tuple(range(per_group)), tuple(reversed(range(per_group))))
    dirs = (0, 1)

    def body(i, carry):
        r0s = (pl.multiple_of(i * gw, gw), pl.multiple_of((n_groups - 1 - i) * gw, gw))
        rss = [pl.ds(r0, gw) for r0 in r0s]
        las = [_split3(la_scrs[d][rss[d], :]) for d in dirs]
        b = [_dot(tris[d], las[d][0]) + _dot(tris[d], las[d][1]) + _dot(tris[d], las[d][2])
             for d in dirs]
        be = [_dot(ones, las[d][0]) + _dot(ones, las[d][1]) + _dot(ones, las[d][2])
              for d in dirs]
        v = [v_ref[rss[d], :].astype(BF16) for d in dirs]
        qd = [(q_ref[rss[d], :] * scale * jnp.exp(b[d])).astype(BF16) for d in dirs]
        ki = [(k_ref[rss[d], :] * jnp.exp(-b[d])).astype(BF16) for d in dirs]
        ks = [(k_ref[rss[d], :] * jnp.exp(be[d] - b[d])).astype(BF16) for d in dirs]
        att = [jnp.where(masks[d], _dot(qd[d], ki[d], NT), 0.0).astype(BF16) for d in dirs]
        o_intra = [_dot(att[d], v[d]) for d in dirs]
        dec = [jnp.exp(be[d]) for d in dirs]
        st = [s_scr[d] for d in dirs]
        inter = [[None] * per_group for _ in dirs]
        for step in range(per_group):
            for d in dirs:
                ch = orders[d][step]
                cs = slice(ch * c, (ch + 1) * c)
                inter[d][ch] = _dot(qd[d][cs, :], st[d].astype(BF16), NT)
                st[d] = st[d] * dec[d][ch * c:ch * c + 1, :] + _dot(v[d][cs, :], ks[d][cs, :], TN)
        for d in dirs:
            s_scr[d] = st[d]
            out_scrs[d][rss[d], :] = o_intra[d] + jnp.concatenate(inter[d], axis=0)
        return carry

    lax.fori_loop(0, n_groups, body, 0)

    o = _rms(of_scr[...] + ob_scr[...]) * nw_ref[...]
    o_ref[...] = (o * _silu(g_ref[...])).astype(o_ref.dtype)
    if want_state:
        sf_ref[0] = s_scr[0].T
        sb_ref[0] = s_scr[1].T


def _gla_call(z, seq, n_seq, row_block0, wgf, bgf, wgb, bgb, nw, s0=None, want_state=False):
    has_s0 = s0 is not None

    def col(cb):
        return pl.BlockSpec((seq, LANE), lambda s, h: (row_block0 + s, cb + h))

    in_specs = [col(0), col(H_A), col(2 * H_A), col(3 * H_A),
                pl.BlockSpec((seq, LANE), lambda s, h: (row_block0 + s, (4 * W_A + 2 * W_B) // LANE)),
                pl.BlockSpec((LANE, DK_A), lambda s, h: (0, h)),
                pl.BlockSpec((1, DK_A), lambda s, h: (0, h)),
                pl.BlockSpec((LANE, DK_A), lambda s, h: (0, h)),
                pl.BlockSpec((1, DK_A), lambda s, h: (0, h)),
                pl.BlockSpec((1, DV_A), lambda s, h: (0, 0))]
    args = [z, z, z, z, z, wgf, bgf, wgb, bgb, nw]
    if has_s0:
        in_specs += [pl.BlockSpec((1, DK_A, DV_A), lambda s, h: ((s * 2 + 0) * H_A + h, 0, 0)),
                     pl.BlockSpec((1, DK_A, DV_A), lambda s, h: ((s * 2 + 1) * H_A + h, 0, 0))]
        args += [s0, s0]
    out_shape = [jax.ShapeDtypeStruct((n_seq * seq, W_A), BF16)]
    out_specs = [pl.BlockSpec((seq, LANE), lambda s, h: (s, h))]
    if want_state:
        out_shape += [jax.ShapeDtypeStruct((n_seq * H_A, DK_A, DV_A), F32)] * 2
        out_specs += [pl.BlockSpec((1, DK_A, DV_A), lambda s, h: (s * H_A + h, 0, 0))] * 2
    return pl.pallas_call(
        functools.partial(_gla_kernel, has_s0=has_s0, want_state=want_state),
        out_shape=out_shape,
        grid=(n_seq, H_A),
        in_specs=in_specs,
        out_specs=out_specs,
        scratch_shapes=[pltpu.VMEM((seq, DK_A), F32), pltpu.VMEM((seq, DK_A), F32),
                        pltpu.VMEM((seq, DV_A), F32), pltpu.VMEM((seq, DV_A), F32),
                        pltpu.VMEM((2, DK_A, DV_A), F32)],
        compiler_params=_cparams(2),
        name="gla_seq%d" % seq,
    )(*args)


def _sgu_kernel(u_ref, v_ref, nw_ref, ws_ref, bs_ref, o_ref):
    tm = u_ref.shape[0]
    u = _gelu(u_ref[...])
    v = (_rms(_gelu(v_ref[...])) * nw_ref[...]).astype(BF16)
    ws = ws_ref[0]
    bs = bs_ref[0]
    for cidx in range(tm // CHUNK_B):
        rs = slice(cidx * CHUNK_B, (cidx + 1) * CHUNK_B)
        sg = _dot(ws, v[rs, :]) + bs
        o_ref[rs, :] = (u[rs, :] * sg).astype(o_ref.dtype)


def _sgu_call(z, nw, ws_bf, bs_full, tm=512):
    t = z.shape[0]
    ub0 = (4 * W_A) // LANE
    vb0 = (4 * W_A + W_B) // LANE
    return pl.pallas_call(
        _sgu_kernel,
        out_shape=jax.ShapeDtypeStruct((t, W_B), BF16),
        grid=(t // tm, H_B),
        in_specs=[
            pl.BlockSpec((tm, LANE), lambda i, h: (i, ub0 + h)),
            pl.BlockSpec((tm, LANE), lambda i, h: (i, vb0 + h)),
            pl.BlockSpec((1, DB), lambda i, h: (0, 0)),
            pl.BlockSpec((1, CHUNK_B, CHUNK_B), lambda i, h: (h, 0, 0)),
            pl.BlockSpec((1, CHUNK_B, DB), lambda i, h: (h, 0, 0)),
        ],
        out_specs=pl.BlockSpec((tm, LANE), lambda i, h: (i, h)),
        compiler_params=_cparams(2),
        name="sgu",
    )(z, z, nw, ws_bf, bs_full)


def _outproj_kernel(a1p_ref, a1s_ref, a2p_ref, a2s_ref, w_ref, x_ref, g_ref, o_ref, *, n_prompt_tiles):
    n1 = a1p_ref.shape[1]

    def run(a1_ref, a2_ref):
        y = _dot(a1_ref[...], w_ref[:n1, :]) + _dot(a2_ref[...], w_ref[n1:, :])
        o_ref[...] = x_ref[...] + g_ref[0, 0] * y

    @pl.when(pl.program_id(0) < n_prompt_tiles)
    def _():
        run(a1p_ref, a2p_ref)

    @pl.when(pl.program_id(0) >= n_prompt_tiles)
    def _():
        run(a1s_ref, a2s_ref)


def _outproj_call(a1, a2, w_bf, x, mod4, layer, rows, tm=512):
    t, d = x.shape
    npt = rows.n_prompt_rows // tm

    def parts(a):
        if isinstance(a, tuple):
            n = a[0].shape[1]
            return (a[0], a[1],
                    pl.BlockSpec((tm, n), lambda i: (jnp.minimum(i, npt - 1), 0)),
                    pl.BlockSpec((tm, n), lambda i: (jnp.maximum(i - npt, 0), 0)))
        spec = pl.BlockSpec((tm, a.shape[1]), lambda i: (i, 0))
        return a, a, spec, spec

    a1p, a1s, s1p, s1s = parts(a1)
    a2p, a2s, s2p, s2s = parts(a2)
    n1, n2 = a1p.shape[1], a2p.shape[1]
    return pl.pallas_call(
        functools.partial(_outproj_kernel, n_prompt_tiles=npt),
        out_shape=jax.ShapeDtypeStruct((t, d), F32),
        grid=(t // tm,),
        in_specs=[
            s1p, s1s, s2p, s2s,
            pl.BlockSpec((n1 + n2, d), lambda i: (0, 0)),
            pl.BlockSpec((tm, d), lambda i: (i, 0)),
            rows.mod_spec(layer, 2, tm),
        ],
        out_specs=pl.BlockSpec((tm, d), lambda i: (i, 0)),
        compiler_params=_cparams(1),
        name="outproj",
    )(a1p, a1s, a2p, a2s, w_bf, x, mod4)


def _embed_kernel(xp_ref, xs_ref, pe_ref, o_ref, *, n_prompt_tiles):
    @pl.when(pl.program_id(0) < n_prompt_tiles)
    def _():
        o_ref[...] = xp_ref[...]

    @pl.when(pl.program_id(0) >= n_prompt_tiles)
    def _():
        o_ref[...] = xs_ref[...] + pe_ref[...]


def _embed_call(xp, xs, pe, tm=512):
    tp, d = xp.shape
    ts = xs.shape[0]
    npt = tp // tm
    pe_tiles = pe.shape[0] // tm
    return pl.pallas_call(
        functools.partial(_embed_kernel, n_prompt_tiles=npt),
        out_shape=jax.ShapeDtypeStruct((tp + ts, d), F32),
        grid=((tp + ts) // tm,),
        in_specs=[
            pl.BlockSpec((tm, d), lambda i: (jnp.minimum(i, npt - 1), 0)),
            pl.BlockSpec((tm, d), lambda i: (jnp.maximum(i - npt, 0), 0)),
            pl.BlockSpec((tm, d), lambda i: (jnp.maximum(i - npt, 0) % pe_tiles, 0)),
        ],
        out_specs=pl.BlockSpec((tm, d), lambda i: (i, 0)),
        compiler_params=_cparams(1),
        name="embed",
    )(xp, xs, pe)


def _conv_kernel(bg_ref, cg_ref, xi_ref, w_ref, o_ref):
    seq = bg_ref.shape[0]
    t = cg_ref[...] * xi_ref[...]
    row = lax.broadcasted_iota(jnp.int32, t.shape, 0)
    t_prev = jnp.where(row == 0, 0.0, pltpu.roll(t, 1, 0))
    t_next = jnp.where(row == seq - 1, 0.0, pltpu.roll(t, seq - 1, 0))
    conv = t_prev * w_ref[0:1, :] + t * w_ref[1:2, :] + t_next * w_ref[2:3, :]
    o_ref[...] = (bg_ref[...] * conv).astype(o_ref.dtype)


def _conv_call(z, seq, n_seq, row_block0, conv_w):
    def col(cb):
        return pl.BlockSpec((seq, LANE), lambda s, j: (row_block0 + s, cb + j))
    return pl.pallas_call(
        _conv_kernel,
        out_shape=jax.ShapeDtypeStruct((n_seq * seq, W_C), BF16),
        grid=(n_seq, W_C // LANE),
        in_specs=[col(0), col(H_C), col(2 * H_C),
                  pl.BlockSpec((3, LANE), lambda s, j: (0, j))],
        out_specs=pl.BlockSpec((seq, LANE), lambda s, j: (s, j)),
        compiler_params=_cparams(2),
        name="conv_seq%d" % seq,
    )(z, z, z, conv_w)


def _fft_kernel(f_ref, cd_ref, sd_ref, cl_ref, sl_ref, o_ref, p_scr, q_scr, *, scale):
    @pl.when(pl.program_id(1) == 0)
    def _():
        cd = cd_ref[...]
        sd = sd_ref[...]
        for h in range(H_D):
            cs = slice(h * DD, (h + 1) * DD)
            fh = f_ref[:, cs].astype(BF16)
            p_scr[:, cs] = _dot(fh, cd).astype(BF16)
            q_scr[:, cs] = _dot(fh, sd).astype(BF16)

    y = _dot(cl_ref[...], p_scr[...]) - _dot(sl_ref[...], q_scr[...])
    o_ref[...] = (y * scale).astype(o_ref.dtype)


def _dft_mats(n):
    idx = jnp.arange(n, dtype=jnp.int32)
    ang = ((idx[:, None] * idx[None, :]) % n).astype(F32) * (2.0 * math.pi / n)
    return jnp.cos(ang).astype(BF16), jnp.sin(ang).astype(BF16)


def _fft_call(z, seq, n_seq, row_block0, cd, sd, cl, sl):
    tl = min(seq, 512)
    fcol = (3 * W_C) // W_D
    return pl.pallas_call(
        functools.partial(_fft_kernel, scale=1.0 / math.sqrt(seq * DD)),
        out_shape=jax.ShapeDtypeStruct((n_seq * seq, W_D), BF16),
        grid=(n_seq, seq // tl),
        in_specs=[
            pl.BlockSpec((seq, W_D), lambda s, i: (row_block0 + s, fcol)),
            pl.BlockSpec((DD, DD), lambda s, i: (0, 0)),
            pl.BlockSpec((DD, DD), lambda s, i: (0, 0)),
            pl.BlockSpec((tl, seq), lambda s, i: (i, 0)),
            pl.BlockSpec((tl, seq), lambda s, i: (i, 0)),
        ],
        out_specs=pl.BlockSpec((tl, W_D), lambda s, i: (s * (seq // tl) + i, 0)),
        scratch_shapes=[pltpu.VMEM((seq, W_D), BF16), pltpu.VMEM((seq, W_D), BF16)],
        compiler_params=_cparams(2),
        name="fft_seq%d" % seq,
    )(z, cd, sd, cl, sl)


def _topk16(s):
    nrows = s.shape[0]
    iota = lax.broadcasted_iota(jnp.int32, s.shape, 0)
    i16 = lax.broadcasted_iota(jnp.int32, (PEER_TOPK, s.shape[1]), 0)

    def body(k, carry):
        s, rank, top = carry
        m = jnp.max(s, axis=0, keepdims=True)
        idx = jnp.min(jnp.where(s == m, iota, nrows), axis=0, keepdims=True)
        sel = iota == idx
        rank = jnp.where(sel, k, rank)
        s = jnp.where(sel, -jnp.inf, s)
        top = jnp.where(i16 == k, m, top)
        return s, rank, top

    init = (s, jnp.full(s.shape, PEER_TOPK, jnp.int32), jnp.zeros((PEER_TOPK, s.shape[1]), F32))
    _, rank, top = lax.fori_loop(0, PEER_TOPK, body, init)
    return rank, top


_CAND_GROUPS = [(0, 16), (1, 8)] + [(k1, 8) for k1 in range(2, 8)]
_CAND_SINGLE0 = sum(n for _, n in _CAND_GROUPS)


_SCORE_FLOOR = -(2.0 ** 126)
_CODE_UNIT = 2.0 ** 127


def _rank_code(k):
    return -_CODE_UNIT * (1.0 + (k + 1) / 32.0)


def _topk16_untied(scores):
    n = scores[0].shape[1]
    i16 = lax.broadcasted_iota(jnp.int32, (PEER_TOPK, n), 0)
    keys = [jnp.maximum(s, _SCORE_FLOOR) for s in scores]
    tops = [jnp.zeros((PEER_TOPK, n), F32) for _ in scores]
    for k in range(PEER_TOPK):
        for c in range(len(keys)):
            m = jnp.max(keys[c], axis=0, keepdims=True)
            keys[c] = jnp.where(keys[c] == m, _rank_code(k), keys[c])
            tops[c] = jnp.where(i16 == k, m, tops[c])
    out = []
    for key, top in zip(keys, tops):
        taken = key < _SCORE_FLOOR
        rank = jnp.where(taken, key * -(32.0 / _CODE_UNIT) - 33.0, float(PEER_TOPK))
        count = jnp.sum(jnp.where(taken, 1.0, 0.0), axis=0, keepdims=True)
        clamped = jnp.sum(jnp.where(top <= _SCORE_FLOOR, 1.0, 0.0), axis=0, keepdims=True)
        out.append((rank, top, (count - PEER_TOPK) + clamped))
    return out


def _cand_scores(t1, t2):
    n = t1.shape[1]
    neg = jnp.full((8, n), -jnp.inf, F32)
    row8 = lax.broadcasted_iota(jnp.int32, (8, n), 0)
    pieces = []
    for k1, nrow in _CAND_GROUPS:
        blk = t1[k1:k1 + 1, :] + t2[0:nrow, :]
        nvalid = PEER_TOPK // (k1 + 1)
        if nvalid < nrow:
            blk = jnp.where(row8 < nvalid, blk, neg)
        pieces.append(blk)
    pieces.append(t1[8:16, :] + t2[0:1, :])
    return jnp.concatenate(pieces, axis=0)


def _psel_finish(s1, s2, r1, t1, t2, cand, sel):
    n = s1.shape[1]
    cmax = t1[0:1, :] + t2[0:1, :]
    z = jnp.sum(jnp.where(sel, jnp.exp(cand - cmax), 0.0), axis=0, keepdims=True)
    self32 = jnp.where(sel, 1.0, 0.0)
    cut = jnp.zeros((N_KEYS, n), F32)
    r0 = 0
    for k1, nrow in _CAND_GROUPS:
        cnt = jnp.sum(self32[r0:r0 + nrow, :], axis=0, keepdims=True)
        cut = jnp.where(r1 == k1, cnt, cut)
        r0 += nrow
    for j in range(8):
        cut = jnp.where(r1 == 8 + j, self32[r0 + j:r0 + j + 1, :], cut)
    e1 = jnp.exp(s1 - t1[0:1, :]) * (1.0 / z)
    e2 = jnp.exp(s2 - t2[0:1, :])
    return cut, e1, e2


def _psel_kernel(x_ref, nw_ref, sh_ref, sc_ref, wq_ref, sk_ref,
                 h_ref, r2_ref, e2_ref, cut_ref, e1_ref, q_scr):
    tq = x_ref.shape[0]
    n_chunks = tq // LANE
    hb = _modnorm(x_ref[...], nw_ref[...], sh_ref[0, 0], sc_ref[0, 0]).astype(BF16)
    h_ref[...] = hb
    q_scr[...] = _dot(wq_ref[...], hb, NT)
    half = D_KEY // 2

    def scores(hh):
        q1 = q_scr[pl.ds(pl.multiple_of(hh * D_KEY, D_KEY), half), :]
        q2 = q_scr[pl.ds(pl.multiple_of(hh * D_KEY + half, half), half), :]
        return _dot3(sk_ref[0], q1), _dot3(sk_ref[1], q2)

    def write(hh, ls, r2, cut, e1, e2):
        r2_ref[hh, :, ls] = r2.astype(BF16)
        e2_ref[hh, :, ls] = e2.astype(BF16)
        cut_ref[hh, :, ls] = cut
        e1_ref[hh, :, ls] = e1

    def exact_chunk(hh, ls, s1c, s2c):
        r1, t1 = _topk16(s1c)
        r2, t2 = _topk16(s2c)
        cand = _cand_scores(t1, t2)
        crank, _ = _topk16(cand)
        cut, e1, e2 = _psel_finish(s1c, s2c, r1.astype(F32), t1, t2, cand, crank < PEER_TOPK)
        write(hh, ls, r2.astype(F32), cut, e1, e2)

    def head(hh, carry):
        s1, s2 = scores(hh)
        halves = []
        for lc in range(n_chunks):
            ls = slice(lc * LANE, (lc + 1) * LANE)
            halves.append(_topk16_untied([s1[:, ls], s2[:, ls]]))
        cands = [_cand_scores(h1[1], h2[1]) for h1, h2 in halves]
        picked = _topk16_untied(cands)
        for lc in range(n_chunks):
            ls = slice(lc * LANE, (lc + 1) * LANE)
            (r1, t1, c1), (r2, t2, c2) = halves[lc]
            crank, _, cc = picked[lc]
            cut, e1, e2 = _psel_finish(s1[:, ls], s2[:, ls], r1, t1, t2, cands[lc], crank < PEER_TOPK)
            write(hh, ls, r2, cut, e1, e2)
            extra = c1 + c2 + cc

            @pl.when(jnp.max(extra) > 0)
            def _():
                exact_chunk(hh, ls, s1[:, ls], s2[:, ls])
        return carry

    lax.fori_loop(0, PEER_HEADS, head, 0)


def _psel_call(x, nw, mod4, layer, rows, wqt_bf, sub_keys, tq=256):
    t, d = x.shape
    nq = wqt_bf.shape[0]
    sel_shape = jax.ShapeDtypeStruct((PEER_HEADS, N_KEYS, t), F32)
    sel_bf = jax.ShapeDtypeStruct((PEER_HEADS, N_KEYS, t), BF16)
    sel_spec = pl.BlockSpec((PEER_HEADS, N_KEYS, tq), lambda i: (0, 0, i))
    return pl.pallas_call(
        _psel_kernel,
        out_shape=[jax.ShapeDtypeStruct((t, d), BF16), sel_bf, sel_bf, sel_shape, sel_shape],
        grid=(t // tq,),
        in_specs=[
            pl.BlockSpec((tq, d), lambda i: (i, 0)),
            pl.BlockSpec((1, d), lambda i: (0, 0)),
            rows.mod_spec(layer, 3, tq),
            rows.mod_spec(layer, 4, tq),
            pl.BlockSpec((nq, d), lambda i: (0, 0)),
            pl.BlockSpec((2, N_KEYS, D_KEY // 2), lambda i: (0, 0, 0)),
        ],
        out_specs=[pl.BlockSpec((tq, d), lambda i: (i, 0)), sel_spec, sel_spec, sel_spec, sel_spec],
        scratch_shapes=[pltpu.VMEM((nq, tq), F32)],
        compiler_params=_cparams(1),
        name="peer_select",
    )(x, nw.reshape(1, d), mod4, mod4, wqt_bf, sub_keys)


def _gelu_sigmoid_form(x):
    k = 2.0 * math.sqrt(2.0 / math.pi) * math.log2(math.e)
    u = x * (-k - (k * 0.044715) * (x * x))
    return x * (1.0 / (1.0 + jnp.exp2(u)))


def _pdense_kernel(h_ref, u_ref, v_ref, r2_ref, e2_ref, cut_ref, e1_ref, x_ref, g_ref, o_ref,
                   hid0, hid1, w0, w1, acc_scr, *, nj, n_real):
    s = pl.program_id(0)
    s3 = jnp.clip(s - 2, 0, n_real - 1)
    j3 = s3 % nj
    te, tq = hid0.shape
    bzero = jnp.zeros((), BF16)

    @pl.when(s == 0)
    def _():
        for r in (hid0, hid1, w0, w1):
            r[...] = jnp.zeros(r.shape, r.dtype)

    @pl.when(j3 == 0)
    def _():
        acc_scr[...] = jnp.zeros(acc_scr.shape, F32)

    def step(hid_new, hid_old, w_new, w_old):
        wide = 2 * LANE
        n_lc = tq // wide
        dcols = acc_scr.shape[1] // n_lc
        for a in range(te // N_KEYS):
            rs = slice(a * N_KEYS, (a + 1) * N_KEYS)
            rs2 = slice((a - a % 2) * N_KEYS, (a - a % 2 + 2) * N_KEYS)
            for lc in range(n_lc):
                ls = slice(lc * wide, (lc + 1) * wide)
                half = N_KEYS // 4
                bcast = [(jnp.broadcast_to(cut_ref[hh, a:a + 1, ls], (half, wide)).astype(BF16),
                          jnp.broadcast_to(e1_ref[hh, a:a + 1, ls], (half, wide)).astype(BF16))
                         for hh in range(PEER_HEADS)]
                for rb in range(N_KEYS // half):
                    ks_ = slice(rb * half, (rb + 1) * half)
                    rows_ = slice(a * N_KEYS + rb * half, a * N_KEYS + (rb + 1) * half)
                    gate = None
                    for hh in range(PEER_HEADS):
                        cut, e1 = bcast[hh]
                        term = jnp.where(r2_ref[hh, ks_, ls] < cut, e2_ref[hh, ks_, ls] * e1, bzero)
                        gate = term if gate is None else gate + term
                    w_new[rows_, ls] = _gelu_sigmoid_form(hid_old[rows_, ls].astype(BF16)) * gate
                if a % 2 == 0:
                    hid_new[rs2, ls] = _dot(u_ref[rs2, :], h_ref[ls, :], NT)
                else:
                    ds_ = slice(lc * dcols, (lc + 1) * dcols)
                    acc_scr[:, ds_] += _dot(w_old[rs2, :], v_ref[rs2, ds_], TN)

    @pl.when(s % 2 == 0)
    def _():
        step(hid0, hid1, w0, w1)

    @pl.when(s % 2 == 1)
    def _():
        step(hid1, hid0, w1, w0)

    @pl.when(jnp.logical_and(j3 == nj - 1, s >= 2))
    def _():
        o_ref[...] = x_ref[...] + g_ref[0, 0] * acc_scr[...]


def _pdense_call(h_bf, u_bf, v_bf, r2, e2, cut, e1, x, mod4, layer, rows, tq=512, te=2048):
    t, d = x.shape
    ne = u_bf.shape[1]
    ni, nj = t // tq, ne // te
    n_real = ni * nj
    na = te // N_KEYS

    def tile(shift):
        def f(s):
            ss = jnp.clip(s - shift, 0, n_real - 1)
            return ss // nj, ss % nj
        return f

    t1, t2, t3 = tile(0), tile(1), tile(2)
    return pl.pallas_call(
        functools.partial(_pdense_kernel, nj=nj, n_real=n_real),
        out_shape=jax.ShapeDtypeStruct((t, d), F32),
        grid=(n_real + 2,),
        in_specs=[
            pl.BlockSpec((tq, d), lambda s: (t1(s)[0], 0)),
            pl.BlockSpec((None, te, d), lambda s: (layer, t1(s)[1], 0)),
            pl.BlockSpec((None, te, d), lambda s: (layer, t3(s)[1], 0)),
            pl.BlockSpec((PEER_HEADS, N_KEYS, tq), lambda s: (0, 0, t2(s)[0])),
            pl.BlockSpec((PEER_HEADS, N_KEYS, tq), lambda s: (0, 0, t2(s)[0])),
            pl.BlockSpec((PEER_HEADS, na, tq), lambda s: (0, t2(s)[1], t2(s)[0])),
            pl.BlockSpec((PEER_HEADS, na, tq), lambda s: (0, t2(s)[1], t2(s)[0])),
            pl.BlockSpec((tq, d), lambda s: (t3(s)[0], 0)),
            rows.mod_spec(layer, 5, tq, tile_of=lambda s: t3(s)[0]),
        ],
        out_specs=pl.BlockSpec((tq, d), lambda s: (t3(s)[0], 0)),
        scratch_shapes=[pltpu.VMEM((te, tq), F32), pltpu.VMEM((te, tq), F32),
                        pltpu.VMEM((te, tq), BF16), pltpu.VMEM((te, tq), BF16),
                        pltpu.VMEM((tq, d), F32)],
        compiler_params=_cparams(1),
        name="peer_dense",
    )(h_bf, u_bf, v_bf, r2, e2, cut, e1, x, mod4)


def _final_kernel(x_ref, w_ref, o_ref):
    o_ref[...] = _rms(x_ref[...]) * w_ref[...]


def _final_call(x, w, row_block0, n_rows, tm=512):
    d = x.shape[1]
    return pl.pallas_call(
        _final_kernel,
        out_shape=jax.ShapeDtypeStruct((n_rows, d), F32),
        grid=(n_rows // tm,),
        in_specs=[pl.BlockSpec((tm, d), lambda i: (row_block0 + i, 0)),
                  pl.BlockSpec((1, d), lambda i: (0, 0))],
        out_specs=pl.BlockSpec((tm, d), lambda i: (i, 0)),
        compiler_params=_cparams(1),
        name="final_norm",
    )(x, w.reshape(1, d))


def _grid_pos_embed(rows, dtype):
    t = jnp.arange(rows * GRID_W)
    r = (t // GRID_W).astype(F32)
    col = (t % GRID_W).astype(F32)
    nf = D_MODEL // 4
    freqs = 1.0 / (10000.0 ** (jnp.arange(nf, dtype=F32) / nf))

    def emb(p):
        a = p[:, None] * freqs[None, :]
        return jnp.concatenate([jnp.sin(a), jnp.cos(a)], axis=-1)
    return jnp.concatenate([emb(r), emb(col)], axis=-1).astype(dtype)


def _even_w_in_layout(w):
    d = w.shape[0]
    a_end = 4 * W_A
    lr = w[:, a_end:a_end + GATE_RANK]
    uv = w[:, a_end + GATE_RANK:]
    pad = jnp.zeros((d, LANE - GATE_RANK), w.dtype)
    return jnp.concatenate([w[:, :a_end], uv, lr, pad], axis=1)


def _pad_gate_w(w):
    return jnp.concatenate([w, jnp.zeros((LANE - GATE_RANK, w.shape[1]), w.dtype)], axis=0)


def kernel(x_prompt, x_sample, state_gla, c, c_ctx, w_mod, b_mod, norm_w, final_norm_w, even_w_in, even_w_gate_f, even_b_gate_f, even_w_gate_b, even_b_gate_b, gla_norm_w, sgu_norm_w, sgu_w_s, sgu_b_s, even_w_out, odd_w_in, odd_conv_w, odd_w_out, peer_w_q, peer_sub_keys, peer_u, peer_v):
    bp, lp, d = x_prompt.shape
    bs, ls, _ = x_sample.shape
    depth = w_mod.shape[0]
    n_even = even_w_in.shape[0]
    tp = bp * lp
    ts = bs * ls
    rows = _Rows(tp, ls)

    x = _embed_call(x_prompt.reshape(tp, d), x_sample.reshape(ts, d),
                    _grid_pos_embed(ls // GRID_W, x_sample.dtype))

    cc = jnp.concatenate([c_ctx[None], c, jnp.zeros((MOD_ROWS - 1 - bs, d), F32)], axis=0)
    mod4 = _mod_call(cc, w_mod, b_mod).reshape(depth, MOD_ROWS, 1, 6 * d)

    cd, sd = _dft_mats(DD)
    clp, slp = _dft_mats(lp)
    cls, sls = _dft_mats(ls)
    s0_all = state_gla.reshape(bs, n_even, 2 * H_A, DK_A, DV_A)
    u_all = peer_u.astype(BF16)
    v_all = peer_v.astype(BF16)

    new_states = []
    for l in range(depth):
        if l % 2 == 0:
            e = l // 2
            z = _normproj_call(x, norm_w[l, 0], mod4, l, rows, _even_w_in_layout(even_w_in[e]).astype(BF16))
            gate_args = (_pad_gate_w(even_w_gate_f[e]), even_b_gate_f[e][None],
                         _pad_gate_w(even_w_gate_b[e]), even_b_gate_b[e][None], gla_norm_w[e][None])
            o_p, sf, sb = _gla_call(z, lp, bp, 0, *gate_args, want_state=True)
            (o_s,) = _gla_call(z, ls, bs, tp // ls, *gate_args,
                               s0=s0_all[:, e].reshape(bs * 2 * H_A, DK_A, DV_A))
            a1 = (o_p, o_s)
            bs_full = jnp.broadcast_to(sgu_b_s[e][:, :, None], (H_B, CHUNK_B, DB))
            a2 = _sgu_call(z, sgu_norm_w[e][None], sgu_w_s[e].astype(BF16), bs_full)
            x = _outproj_call(a1, a2, even_w_out[e].astype(BF16), x, mod4, l, rows)
            sfb = jnp.stack([sf.reshape(bp, H_A, DK_A, DV_A), sb.reshape(bp, H_A, DK_A, DV_A)], axis=1)
            new_states.append(sfb.astype(x_prompt.dtype))
        else:
            o = l // 2
            z = _normproj_call(x, norm_w[l, 0], mod4, l, rows, odd_w_in[o].astype(BF16))
            yc = (_conv_call(z, lp, bp, 0, odd_conv_w[o]), _conv_call(z, ls, bs, tp // ls, odd_conv_w[o]))
            yd = (_fft_call(z, lp, bp, 0, cd, sd, clp, slp), _fft_call(z, ls, bs, tp // ls, cd, sd, cls, sls))
            x = _outproj_call(yc, yd, odd_w_out[o].astype(BF16), x, mod4, l, rows)
        h_bf, r2, e2, cut, e1 = _psel_call(x, norm_w[l, 1], mod4, l, rows,
                                           peer_w_q[l].T.astype(BF16), peer_sub_keys[l])
        x = _pdense_call(h_bf, u_all, v_all, r2, e2, cut, e1, x, mod4, l, rows)

    y_prompt = _final_call(x, final_norm_w, 0, tp).reshape(bp, lp, d)
    y_sample = _final_call(x, final_norm_w, tp // 512, ts).reshape(bs, ls, d)
    state_new = jnp.stack(new_states, axis=1)
    return (y_prompt, y_sample, state_new)
```

```python
import functools
import math

import jax
import jax.numpy as jnp
import numpy as np
from jax import lax
from jax.experimental import pallas as pl
from jax.experimental.pallas import tpu as pltpu

F32 = jnp.float32
BF16 = jnp.bfloat16

D_MODEL = 1024
DEPTH = 4
GRID_W = 64
RMS_EPS = 1e-6
H_A = 4
DK_A = 128
DV_A = 128
GATE_RANK = 16
GATE_NORM = 16.0
GLA_CHUNK = 64
W_A = H_A * DV_A
H_B = 4
DB = 128
CHUNK_B = 128
W_B = H_B * DB
H_C = 4
DC = 128
W_C = H_C * DC
H_D = 4
DD = 128
W_D = H_D * DD
N_KEYS = 128
N_EXPERTS = N_KEYS * N_KEYS
PEER_HEADS = 8
PEER_TOPK = 16
D_KEY = 256

LANE = 128
MOD_ROWS = 8
EVEN_IN_PAD = 4 * W_A + 2 * W_B + LANE
VMEM_LIMIT = 48 * 1024 * 1024

NN = (((1,), (0,)), ((), ()))
NT = (((1,), (1,)), ((), ()))
TN = (((0,), (0,)), ((), ()))


def _cparams(n_axes, flags=None):
    return pltpu.CompilerParams(dimension_semantics=("arbitrary",) * n_axes,
                                vmem_limit_bytes=VMEM_LIMIT, flags=flags)


def _dot(a, b, dims=NN):
    return lax.dot_general(a, b, dims, preferred_element_type=F32)


def _split2(x):
    hi = x.astype(BF16)
    lo = (x - hi.astype(F32)).astype(BF16)
    return hi, lo


def _split3(x):
    hi = x.astype(BF16)
    r = x - hi.astype(F32)
    mid = r.astype(BF16)
    lo = (r - mid.astype(F32)).astype(BF16)
    return hi, mid, lo


def _dot3(a, b, dims=NN):
    ah, al = _split2(a)
    bh, bl = _split2(b)
    return _dot(ah, bh, dims) + _dot(al, bh, dims) + _dot(ah, bl, dims)


def _silu(x):
    return x * (1.0 / (1.0 + jnp.exp(-x)))


def _gelu(x):
    c = math.sqrt(2.0 / math.pi)
    return x * (0.5 * (1.0 + jnp.tanh(c * (x + 0.044715 * (x * x * x)))))


def _log_sigmoid(x):
    return jnp.minimum(x, 0.0) - jnp.log(1.0 + jnp.exp(-jnp.abs(x)))


def _rms(x):
    return x * lax.rsqrt(jnp.mean(x * x, axis=-1, keepdims=True) + RMS_EPS)


def _modnorm(x, nw, shift, scale):
    return (_rms(x) * nw) * (1.0 + scale) + shift


def _mod_kernel(cc_ref, w_ref, b_ref, o_ref):
    a = _silu(cc_ref[...])
    o_ref[0] = _dot3(a, w_ref[0]) + b_ref[0]


def _mod_call(cc, w_mod, b_mod):
    depth, d, n = w_mod.shape
    tn = 1024
    return pl.pallas_call(
        _mod_kernel,
        out_shape=jax.ShapeDtypeStruct((depth, MOD_ROWS, n), F32),
        grid=(depth, n // tn),
        in_specs=[
            pl.BlockSpec((MOD_ROWS, d), lambda l, j: (0, 0)),
            pl.BlockSpec((1, d, tn), lambda l, j: (l, 0, j)),
            pl.BlockSpec((1, 1, tn), lambda l, j: (l, 0, j)),
        ],
        out_specs=pl.BlockSpec((1, MOD_ROWS, tn), lambda l, j: (l, 0, j)),
        compiler_params=_cparams(2),
        name="mod_rows",
    )(cc, w_mod, b_mod.reshape(depth, 1, n))


class _Rows:
    def __init__(self, n_prompt_rows, dec_seq):
        self.n_prompt_rows = n_prompt_rows
        self.dec_seq = dec_seq

    def mod_row(self, i, tm):
        npt = self.n_prompt_rows // tm
        tps = self.dec_seq // tm
        return jnp.where(i < npt, 0, 1 + (i - npt) // tps)

    def mod_spec(self, layer, section, tm, tile_of=lambda i, *_: i):
        return pl.BlockSpec((1, 1, 1, D_MODEL),
                            lambda *g: (layer, self.mod_row(tile_of(*g), tm), 0, section))


def _normproj_kernel(x_ref, nw_ref, sh_ref, sc_ref, w_ref, o_ref):
    h = _modnorm(x_ref[...], nw_ref[...], sh_ref[0, 0], sc_ref[0, 0])
    o_ref[...] = _dot(h.astype(BF16), w_ref[...])


def _normproj_call(x, nw, mod4, layer, rows, w_bf, tm=256):
    t, d = x.shape
    n = w_bf.shape[1]
    return pl.pallas_call(
        _normproj_kernel,
        out_shape=jax.ShapeDtypeStruct((t, n), F32),
        grid=(t // tm,),
        in_specs=[
            pl.BlockSpec((tm, d), lambda i: (i, 0)),
            pl.BlockSpec((1, d), lambda i: (0, 0)),
            rows.mod_spec(layer, 0, tm),
            rows.mod_spec(layer, 1, tm),
            pl.BlockSpec((d, n), lambda i: (0, 0)),
        ],
        out_specs=pl.BlockSpec((tm, n), lambda i: (i, 0)),
        compiler_params=_cparams(1),
        name="normproj",
    )(x, nw.reshape(1, d), mod4, mod4, w_bf)


def _gla_kernel(*refs, has_s0, want_state):
    (q_ref, k_ref, v_ref, g_ref, lr_ref, wgf_ref, bgf_ref, wgb_ref, bgb_ref, nw_ref) = refs[:10]
    pos = 10
    if has_s0:
        s0f_ref, s0b_ref = refs[pos:pos + 2]
        pos += 2
    o_ref = refs[pos]
    pos += 1
    if want_state:
        sf_ref, sb_ref = refs[pos:pos + 2]
        pos += 2
    laf_scr, lab_scr, of_scr, ob_scr, s_scr = refs[pos:]

    seq = q_ref.shape[0]
    c = GLA_CHUNK
    per_group = 4
    gw = per_group * c
    n_groups = seq // gw
    scale = DK_A ** -0.5

    lr = lr_ref[...]
    laf_scr[...] = _log_sigmoid(_dot3(lr, wgf_ref[...]) + bgf_ref[...]) * (1.0 / GATE_NORM)
    lab_scr[...] = _log_sigmoid(_dot3(lr, wgb_ref[...]) + bgb_ref[...]) * (1.0 / GATE_NORM)
    if has_s0:
        s_scr[0] = s0f_ref[0].T
        s_scr[1] = s0b_ref[0].T
    else:
        s_scr[...] = jnp.zeros(s_scr.shape, F32)

    ri = lax.broadcasted_iota(jnp.int32, (gw, gw), 0)
    ci = lax.broadcasted_iota(jnp.int32, (gw, gw), 1)
    same = (ri // c) == (ci // c)
    lower = jnp.logical_and(same, ci <= ri)
    upper = jnp.logical_and(same, ci >= ri)
    tril = jnp.where(lower, 1.0, 0.0).astype(BF16)
    triu = jnp.where(upper, 1.0, 0.0).astype(BF16)
    ones = jnp.where(same, 1.0, 0.0).astype(BF16)

    la_scrs = (laf_scr, lab_scr)
    tris = (tril, triu)
    masks = (lower, upper)
    out_scrs = (of_scr, ob_scr)
    orders = (tuple(range(per_group)), tuple(reversed(range(per_group))))
    dirs = (0, 1)

    def body(i, carry):
        r0s = (pl.multiple_of(i * gw, gw), pl.multiple_of((n_groups - 1 - i) * gw, gw))
        rss = [pl.ds(r0, gw) for r0 in r0s]
        las = [_split3(la_scrs[d][rss[d], :]) for d in dirs]
        b = [_dot(tris[d], las[d][0]) + _dot(tris[d], las[d][1]) + _dot(tris[d], las[d][2])
             for d in dirs]
        be = [_dot(ones, las[d][0]) + _dot(ones, las[d][1]) + _dot(ones, las[d][2])
              for d in dirs]
        v = [v_ref[rss[d], :].astype(BF16) for d in dirs]
        qd = [(q_ref[rss[d], :] * scale * jnp.exp(b[d])).astype(BF16) for d in dirs]
        ki = [(k_ref[rss[d], :] * jnp.exp(-b[d])).astype(BF16) for d in dirs]
        ks = [(k_ref[rss[d], :] * jnp.exp(be[d] - b[d])).astype(BF16) for d in dirs]
        att = [jnp.where(masks[d], _dot(qd[d], ki[d], NT), 0.0).astype(BF16) for d in dirs]
        o_intra = [_dot(att[d], v[d]) for d in dirs]
        dec = [jnp.exp(be[d]) for d in dirs]
        st = [s_scr[d] for d in dirs]
        inter = [[None] * per_group for _ in dirs]
        for step in range(per_group):
            for d in dirs:
                ch = orders[d][step]
                cs = slice(ch * c, (ch + 1) * c)
                inter[d][ch] = _dot(qd[d][cs, :], st[d].astype(BF16), NT)
                st[d] = st[d] * dec[d][ch * c:ch * c + 1, :] + _dot(v[d][cs, :], ks[d][cs, :], TN)
        for d in dirs:
            s_scr[d] = st[d]
            out_scrs[d][rss[d], :] = o_intra[d] + jnp.concatenate(inter[d], axis=0)
        return carry

    lax.fori_loop(0, n_groups, body, 0)

    o = _rms(of_scr[...] + ob_scr[...]) * nw_ref[...]
    o_ref[...] = (o * _silu(g_ref[...])).astype(o_ref.dtype)
    if want_state:
        sf_ref[0] = s_scr[0].T
        sb_ref[0] = s_scr[1].T


def _gla_call(z, seq, n_seq, row_block0, wgf, bgf, wgb, bgb, nw, s0=None, want_state=False):
    has_s0 = s0 is not None

    def col(cb):
        return pl.BlockSpec((seq, LANE), lambda s, h: (row_block0 + s, cb + h))

    in_specs = [col(0), col(H_A), col(2 * H_A), col(3 * H_A),
                pl.BlockSpec((seq, LANE), lambda s, h: (row_block0 + s, (4 * W_A + 2 * W_B) // LANE)),
                pl.BlockSpec((LANE, DK_A), lambda s, h: (0, h)),
                pl.BlockSpec((1, DK_A), lambda s, h: (0, h)),
                pl.BlockSpec((LANE, DK_A), lambda s, h: (0, h)),
                pl.BlockSpec((1, DK_A), lambda s, h: (0, h)),
                pl.BlockSpec((1, DV_A), lambda s, h: (0, 0))]
    args = [z, z, z, z, z, wgf, bgf, wgb, bgb, nw]
    if has_s0:
        in_specs += [pl.BlockSpec((1, DK_A, DV_A), lambda s, h: ((s * 2 + 0) * H_A + h, 0, 0)),
                     pl.BlockSpec((1, DK_A, DV_A), lambda s, h: ((s * 2 + 1) * H_A + h, 0, 0))]
        args += [s0, s0]
    out_shape = [jax.ShapeDtypeStruct((n_seq * seq, W_A), BF16)]
    out_specs = [pl.BlockSpec((seq, LANE), lambda s, h: (s, h))]
    if want_state:
        out_shape += [jax.ShapeDtypeStruct((n_seq * H_A, DK_A, DV_A), F32)] * 2
        out_specs += [pl.BlockSpec((1, DK_A, DV_A), lambda s, h: (s * H_A + h, 0, 0))] * 2
    return pl.pallas_call(
        functools.partial(_gla_kernel, has_s0=has_s0, want_state=want_state),
        out_shape=out_shape,
        grid=(n_seq, H_A),
        in_specs=in_specs,
        out_specs=out_specs,
        scratch_shapes=[pltpu.VMEM((seq, DK_A), F32), pltpu.VMEM((seq, DK_A), F32),
                        pltpu.VMEM((seq, DV_A), F32), pltpu.VMEM((seq, DV_A), F32),
                        pltpu.VMEM((2, DK_A, DV_A), F32)],
        compiler_params=_cparams(2),
        name="gla_seq%d" % seq,
    )(*args)


def _sgu_kernel(u_ref, v_ref, nw_ref, ws_ref, bs_ref, o_ref):
    tm = u_ref.shape[0]
    u = _gelu(u_ref[...])
    v = (_rms(_gelu(v_ref[...])) * nw_ref[...]).astype(BF16)
    ws = ws_ref[0]
    bs = bs_ref[0]
    for cidx in range(tm // CHUNK_B):
        rs = slice(cidx * CHUNK_B, (cidx + 1) * CHUNK_B)
        sg = _dot(ws, v[rs, :]) + bs
        o_ref[rs, :] = (u[rs, :] * sg).astype(o_ref.dtype)


def _sgu_call(z, nw, ws_bf, bs_full, tm=512):
    t = z.shape[0]
    ub0 = (4 * W_A) // LANE
    vb0 = (4 * W_A + W_B) // LANE
    return pl.pallas_call(
        _sgu_kernel,
        out_shape=jax.ShapeDtypeStruct((t, W_B), BF16),
        grid=(t // tm, H_B),
        in_specs=[
            pl.BlockSpec((tm, LANE), lambda i, h: (i, ub0 + h)),
            pl.BlockSpec((tm, LANE), lambda i, h: (i, vb0 + h)),
            pl.BlockSpec((1, DB), lambda i, h: (0, 0)),
            pl.BlockSpec((1, CHUNK_B, CHUNK_B), lambda i, h: (h, 0, 0)),
            pl.BlockSpec((1, CHUNK_B, DB), lambda i, h: (h, 0, 0)),
        ],
        out_specs=pl.BlockSpec((tm, LANE), lambda i, h: (i, h)),
        compiler_params=_cparams(2),
        name="sgu",
    )(z, z, nw, ws_bf, bs_full)


def _outproj_kernel(a1p_ref, a1s_ref, a2p_ref, a2s_ref, w_ref, x_ref, g_ref, o_ref, *, n_prompt_tiles):
    n1 = a1p_ref.shape[1]

    def run(a1_ref, a2_ref):
        y = _dot(a1_ref[...], w_ref[:n1, :]) + _dot(a2_ref[...], w_ref[n1:, :])
        o_ref[...] = x_ref[...] + g_ref[0, 0] * y

    @pl.when(pl.program_id(0) < n_prompt_tiles)
    def _():
        run(a1p_ref, a2p_ref)

    @pl.when(pl.program_id(0) >= n_prompt_tiles)
    def _():
        run(a1s_ref, a2s_ref)


def _outproj_call(a1, a2, w_bf, x, mod4, layer, rows, tm=512):
    t, d = x.shape
    npt = rows.n_prompt_rows // tm

    def parts(a):
        if isinstance(a, tuple):
            n = a[0].shape[1]
            return (a[0], a[1],
                    pl.BlockSpec((tm, n), lambda i: (jnp.minimum(i, npt - 1), 0)),
                    pl.BlockSpec((tm, n), lambda i: (jnp.maximum(i - npt, 0), 0)))
        spec = pl.BlockSpec((tm, a.shape[1]), lambda i: (i, 0))
        return a, a, spec, spec

    a1p, a1s, s1p, s1s = parts(a1)
    a2p, a2s, s2p, s2s = parts(a2)
    n1, n2 = a1p.shape[1], a2p.shape[1]
    return pl.pallas_call(
        functools.partial(_outproj_kernel, n_prompt_tiles=npt),
        out_shape=jax.ShapeDtypeStruct((t, d), F32),
        grid=(t // tm,),
        in_specs=[
            s1p, s1s, s2p, s2s,
            pl.BlockSpec((n1 + n2, d), lambda i: (0, 0)),
            pl.BlockSpec((tm, d), lambda i: (i, 0)),
            rows.mod_spec(layer, 2, tm),
        ],
        out_specs=pl.BlockSpec((tm, d), lambda i: (i, 0)),
        compiler_params=_cparams(1),
        name="outproj",
    )(a1p, a1s, a2p, a2s, w_bf, x, mod4)


def _embed_kernel(xp_ref, xs_ref, pe_ref, o_ref, *, n_prompt_tiles):
    @pl.when(pl.program_id(0) < n_prompt_tiles)
    def _():
        o_ref[...] = xp_ref[...]

    @pl.when(pl.program_id(0) >= n_prompt_tiles)
    def _():
        o_ref[...] = xs_ref[...] + pe_ref[...]


def _embed_call(xp, xs, pe, tm=512):
    tp, d = xp.shape
    ts = xs.shape[0]
    npt = tp // tm
    pe_tiles = pe.shape[0] // tm
    return pl.pallas_call(
        functools.partial(_embed_kernel, n_prompt_tiles=npt),
        out_shape=jax.ShapeDtypeStruct((tp + ts, d), F32),
        grid=((tp + ts) // tm,),
        in_specs=[
            pl.BlockSpec((tm, d), lambda i: (jnp.minimum(i, npt - 1), 0)),
            pl.BlockSpec((tm, d), lambda i: (jnp.maximum(i - npt, 0), 0)),
            pl.BlockSpec((tm, d), lambda i: (jnp.maximum(i - npt, 0) % pe_tiles, 0)),
        ],
        out_specs=pl.BlockSpec((tm, d), lambda i: (i, 0)),
        compiler_params=_cparams(1),
        name="embed",
    )(xp, xs, pe)


def _conv_kernel(bg_ref, cg_ref, xi_ref, w_ref, o_ref):
    seq = bg_ref.shape[0]
    t = cg_ref[...] * xi_ref[...]
    row = lax.broadcasted_iota(jnp.int32, t.shape, 0)
    t_prev = jnp.where(row == 0, 0.0, pltpu.roll(t, 1, 0))
    t_next = jnp.where(row == seq - 1, 0.0, pltpu.roll(t, seq - 1, 0))
    conv = t_prev * w_ref[0:1, :] + t * w_ref[1:2, :] + t_next * w_ref[2:3, :]
    o_ref[...] = (bg_ref[...] * conv).astype(o_ref.dtype)


def _conv_call(z, seq, n_seq, row_block0, conv_w):
    def col(cb):
        return pl.BlockSpec((seq, LANE), lambda s, j: (row_block0 + s, cb + j))
    return pl.pallas_call(
        _conv_kernel,
        out_shape=jax.ShapeDtypeStruct((n_seq * seq, W_C), BF16),
        grid=(n_seq, W_C // LANE),
        in_specs=[col(0), col(H_C), col(2 * H_C),
                  pl.BlockSpec((3, LANE), lambda s, j: (0, j))],
        out_specs=pl.BlockSpec((seq, LANE), lambda s, j: (s, j)),
        compiler_params=_cparams(2),
        name="conv_seq%d" % seq,
    )(z, z, z, conv_w)


def _fft_kernel(f_ref, cd_ref, sd_ref, cl_ref, sl_ref, o_ref, p_scr, q_scr, *, scale):
    @pl.when(pl.program_id(1) == 0)
    def _():
        cd = cd_ref[...]
        sd = sd_ref[...]
        for h in range(H_D):
            cs = slice(h * DD, (h + 1) * DD)
            fh = f_ref[:, cs].astype(BF16)
            p_scr[:, cs] = _dot(fh, cd).astype(BF16)
            q_scr[:, cs] = _dot(fh, sd).astype(BF16)

    y = _dot(cl_ref[...], p_scr[...]) - _dot(sl_ref[...], q_scr[...])
    o_ref[...] = (y * scale).astype(o_ref.dtype)


def _dft_mats(n):
    idx = jnp.arange(n, dtype=jnp.int32)
    ang = ((idx[:, None] * idx[None, :]) % n).astype(F32) * (2.0 * math.pi / n)
    return jnp.cos(ang).astype(BF16), jnp.sin(ang).astype(BF16)


def _fft_call(z, seq, n_seq, row_block0, cd, sd, cl, sl):
    tl = min(seq, 512)
    fcol = (3 * W_C) // W_D
    return pl.pallas_call(
        functools.partial(_fft_kernel, scale=1.0 / math.sqrt(seq * DD)),
        out_shape=jax.ShapeDtypeStruct((n_seq * seq, W_D), BF16),
        grid=(n_seq, seq // tl),
        in_specs=[
            pl.BlockSpec((seq, W_D), lambda s, i: (row_block0 + s, fcol)),
            pl.BlockSpec((DD, DD), lambda s, i: (0, 0)),
            pl.BlockSpec((DD, DD), lambda s, i: (0, 0)),
            pl.BlockSpec((tl, seq), lambda s, i: (i, 0)),
            pl.BlockSpec((tl, seq), lambda s, i: (i, 0)),
        ],
        out_specs=pl.BlockSpec((tl, W_D), lambda s, i: (s * (seq // tl) + i, 0)),
        scratch_shapes=[pltpu.VMEM((seq, W_D), BF16), pltpu.VMEM((seq, W_D), BF16)],
        compiler_params=_cparams(2),
        name="fft_seq%d" % seq,
    )(z, cd, sd, cl, sl)


def _topk16(s):
    nrows = s.shape[0]
    iota = lax.broadcasted_iota(jnp.int32, s.shape, 0)
    i16 = lax.broadcasted_iota(jnp.int32, (PEER_TOPK, s.shape[1]), 0)

    def body(k, carry):
        s, rank, top = carry
        m = jnp.max(s, axis=0, keepdims=True)
        idx = jnp.min(jnp.where(s == m, iota, nrows), axis=0, keepdims=True)
        sel = iota == idx
        rank = jnp.where(sel, k, rank)
        s = jnp.where(sel, -jnp.inf, s)
        top = jnp.where(i16 == k, m, top)
        return s, rank, top

    init = (s, jnp.full(s.shape, PEER_TOPK, jnp.int32), jnp.zeros((PEER_TOPK, s.shape[1]), F32))
    _, rank, top = lax.fori_loop(0, PEER_TOPK, body, init)
    return rank, top


_CAND_GROUPS = [(0, 16), (1, 8)] + [(k1, 8) for k1 in range(2, 8)]
_CAND_SINGLE0 = sum(n for _, n in _CAND_GROUPS)


_SCORE_FLOOR = -(2.0 ** 126)
_CODE_UNIT = 2.0 ** 127


def _rank_code(k):
    return -_CODE_UNIT * (1.0 + (k + 1) / 32.0)


def _topk16_untied(scores):
    n = scores[0].shape[1]
    i16 = lax.broadcasted_iota(jnp.int32, (PEER_TOPK, n), 0)
    keys = [jnp.maximum(s, _SCORE_FLOOR) for s in scores]
    tops = [jnp.zeros((PEER_TOPK, n), F32) for _ in scores]
    for k in range(PEER_TOPK):
        for c in range(len(keys)):
            m = jnp.max(keys[c], axis=0, keepdims=True)
            keys[c] = jnp.where(keys[c] == m, _rank_code(k), keys[c])
            tops[c] = jnp.where(i16 == k, m, tops[c])
    out = []
    for key, top in zip(keys, tops):
        taken = key < _SCORE_FLOOR
        rank = jnp.where(taken, key * -(32.0 / _CODE_UNIT) - 33.0, float(PEER_TOPK))
        count = jnp.sum(jnp.where(taken, 1.0, 0.0), axis=0, keepdims=True)
        clamped = jnp.sum(jnp.where(top <= _SCORE_FLOOR, 1.0, 0.0), axis=0, keepdims=True)
        out.append((rank, top, (count - PEER_TOPK) + clamped))
    return out


def _cand_scores(t1, t2):
    n = t1.shape[1]
    neg = jnp.full((8, n), -jnp.inf, F32)
    row8 = lax.broadcasted_iota(jnp.int32, (8, n), 0)
    pieces = []
    for k1, nrow in _CAND_GROUPS:
        blk = t1[k1:k1 + 1, :] + t2[0:nrow, :]
        nvalid = PEER_TOPK // (k1 + 1)
        if nvalid < nrow:
            blk = jnp.where(row8 < nvalid, blk, neg)
        pieces.append(blk)
    pieces.append(t1[8:16, :] + t2[0:1, :])
    return jnp.concatenate(pieces, axis=0)


def _psel_finish(s1, s2, r1, t1, t2, cand, sel):
    n = s1.shape[1]
    cmax = t1[0:1, :] + t2[0:1, :]
    z = jnp.sum(jnp.where(sel, jnp.exp(cand - cmax), 0.0), axis=0, keepdims=True)
    self32 = jnp.where(sel, 1.0, 0.0)
    cut = jnp.zeros((N_KEYS, n), F32)
    r0 = 0
    for k1, nrow in _CAND_GROUPS:
        cnt = jnp.sum(self32[r0:r0 + nrow, :], axis=0, keepdims=True)
        cut = jnp.where(r1 == k1, cnt, cut)
        r0 += nrow
    for j in range(8):
        cut = jnp.where(r1 == 8 + j, self32[r0 + j:r0 + j + 1, :], cut)
    e1 = jnp.exp(s1 - t1[0:1, :]) * (1.0 / z)
    e2 = jnp.exp(s2 - t2[0:1, :])
    return cut, e1, e2


def _psel_kernel(x_ref, nw_ref, sh_ref, sc_ref, wq_ref, sk_ref,
                 h_ref, r2_ref, e2_ref, cut_ref, e1_ref, q_scr, hb_scr):
    tq = x_ref.shape[0]
    n_chunks = tq // LANE
    hb = _modnorm(x_ref[...], nw_ref[...], sh_ref[0, 0], sc_ref[0, 0]).astype(BF16)
    h_ref[...] = hb
    hb_scr[...] = hb
    half = D_KEY // 2

    def project(hh, slot):
        rows_ = pl.ds(pl.multiple_of(hh * D_KEY, D_KEY), D_KEY)
        q_scr[slot] = _dot(wq_ref[rows_, :], hb_scr[...], NT)

    project(0, 0)

    def scores(hh):
        slot = hh % 2
        return (_dot3(sk_ref[0], q_scr[slot, pl.ds(0, half), :]),
                _dot3(sk_ref[1], q_scr[slot, pl.ds(half, half), :]))

    def write(hh, ls, r2, cut, e1, e2):
        r2_ref[hh, :, ls] = r2.astype(BF16)
        e2_ref[hh, :, ls] = e2.astype(BF16)
        cut_ref[hh, :, ls] = cut
        e1_ref[hh, :, ls] = e1

    def exact_chunk(hh, ls, s1c, s2c):
        r1, t1 = _topk16(s1c)
        r2, t2 = _topk16(s2c)
        cand = _cand_scores(t1, t2)
        crank, _ = _topk16(cand)
        cut, e1, e2 = _psel_finish(s1c, s2c, r1.astype(F32), t1, t2, cand, crank < PEER_TOPK)
        write(hh, ls, r2.astype(F32), cut, e1, e2)

    def head(hh, carry):
        s1, s2 = scores(hh)
        halves = []
        for lc in range(n_chunks):
            ls = slice(lc * LANE, (lc + 1) * LANE)
            halves.append(_topk16_untied([s1[:, ls], s2[:, ls]]))
            if lc == 0:
                project(jnp.minimum(hh + 1, PEER_HEADS - 1), 1 - hh % 2)
        cands = [_cand_scores(h1[1], h2[1]) for h1, h2 in halves]
        picked = _topk16_untied(cands)
        for lc in range(n_chunks):
            ls = slice(lc * LANE, (lc + 1) * LANE)
            (r1, t1, c1), (r2, t2, c2) = halves[lc]
            crank, _, cc = picked[lc]
            cut, e1, e2 = _psel_finish(s1[:, ls], s2[:, ls], r1, t1, t2, cands[lc], crank < PEER_TOPK)
            write(hh, ls, r2, cut, e1, e2)
            extra = c1 + c2 + cc

            @pl.when(jnp.max(extra) > 0)
            def _():
                exact_chunk(hh, ls, s1[:, ls], s2[:, ls])
        return carry

    lax.fori_loop(0, PEER_HEADS, head, 0)


def _psel_call(x, nw, mod4, layer, rows, wqt_bf, sub_keys, tq=256):
    t, d = x.shape
    nq = wqt_bf.shape[0]
    sel_shape = jax.ShapeDtypeStruct((PEER_HEADS, N_KEYS, t), F32)
    sel_bf = jax.ShapeDtypeStruct((PEER_HEADS, N_KEYS, t), BF16)
    sel_spec = pl.BlockSpec((PEER_HEADS, N_KEYS, tq), lambda i: (0, 0, i))
    return pl.pallas_call(
        _psel_kernel,
        out_shape=[jax.ShapeDtypeStruct((t, d), BF16), sel_bf, sel_bf, sel_shape, sel_shape],
        grid=(t // tq,),
        in_specs=[
            pl.BlockSpec((tq, d), lambda i: (i, 0)),
            pl.BlockSpec((1, d), lambda i: (0, 0)),
            rows.mod_spec(layer, 3, tq),
            rows.mod_spec(layer, 4, tq),
            pl.BlockSpec((nq, d), lambda i: (0, 0)),
            pl.BlockSpec((2, N_KEYS, D_KEY // 2), lambda i: (0, 0, 0)),
        ],
        out_specs=[pl.BlockSpec((tq, d), lambda i: (i, 0)), sel_spec, sel_spec, sel_spec, sel_spec],
        scratch_shapes=[pltpu.VMEM((2, D_KEY, tq), F32), pltpu.VMEM((tq, d), BF16)],
        compiler_params=_cparams(1),
        name="peer_select",
    )(x, nw.reshape(1, d), mod4, mod4, wqt_bf, sub_keys)


def _gelu_sigmoid_form(x):
    k = 2.0 * math.sqrt(2.0 / math.pi) * math.log2(math.e)
    u = x * (-k - (k * 0.044715) * (x * x))
    return x * (1.0 / (1.0 + jnp.exp2(u)))


def _pdense_kernel(h_ref, u_ref, v_ref, r2_ref, e2_ref, cut_ref, e1_ref, x_ref, g_ref, o_ref,
                   hid0, hid1, w0, w1, acc_scr, *, nj, n_real):
    s = pl.program_id(0)
    s3 = jnp.clip(s - 2, 0, n_real - 1)
    j3 = s3 % nj
    te, tq = hid0.shape
    bzero = jnp.zeros((), BF16)

    @pl.when(s == 0)
    def _():
        for r in (hid0, hid1, w0, w1):
            r[...] = jnp.zeros(r.shape, r.dtype)

    @pl.when(j3 == 0)
    def _():
        acc_scr[...] = jnp.zeros(acc_scr.shape, F32)

    def step(hid_new, hid_old, w_new, w_old):
        wide = 2 * LANE
        n_lc = tq // wide
        dcols = acc_scr.shape[1] // n_lc
        for a in range(te // N_KEYS):
            rs = slice(a * N_KEYS, (a + 1) * N_KEYS)
            rs2 = slice((a - a % 2) * N_KEYS, (a - a % 2 + 2) * N_KEYS)
            for lc in range(n_lc):
                ls = slice(lc * wide, (lc + 1) * wide)
                half = N_KEYS // 4
                bcast = [(jnp.broadcast_to(cut_ref[hh, a:a + 1, ls], (half, wide)).astype(BF16),
                          jnp.broadcast_to(e1_ref[hh, a:a + 1, ls], (half, wide)).astype(BF16))
                         for hh in range(PEER_HEADS)]
                for rb in range(N_KEYS // half):
                    ks_ = slice(rb * half, (rb + 1) * half)
                    rows_ = slice(a * N_KEYS + rb * half, a * N_KEYS + (rb + 1) * half)
                    gate = None
                    for hh in range(PEER_HEADS):
                        cut, e1 = bcast[hh]
                        term = jnp.where(r2_ref[hh, ks_, ls] < cut, e2_ref[hh, ks_, ls] * e1, bzero)
                        gate = term if gate is None else gate + term
                    w_new[rows_, ls] = _gelu_sigmoid_form(hid_old[rows_, ls].astype(BF16)) * gate
                if a % 2 == 0:
                    hid_new[rs2, ls] = _dot(u_ref[rs2, :], h_ref[ls, :], NT)
                else:
                    ds_ = slice(lc * dcols, (lc + 1) * dcols)
                    acc_scr[:, ds_] += _dot(w_old[rs2, :], v_ref[rs2, ds_], TN)

    @pl.when(s % 2 == 0)
    def _():
        step(hid0, hid1, w0, w1)

    @pl.when(s % 2 == 1)
    def _():
        step(hid1, hid0, w1, w0)

    @pl.when(jnp.logical_and(j3 == nj - 1, s >= 2))
    def _():
        o_ref[...] = x_ref[...] + g_ref[0, 0] * acc_scr[...]


def _pdense_call(h_bf, u_bf, v_bf, r2, e2, cut, e1, x, mod4, layer, rows, tq=512, te=2048):
    t, d = x.shape
    ne = u_bf.shape[1]
    ni, nj = t // tq, ne // te
    n_real = ni * nj
    na = te // N_KEYS

    def tile(shift):
        def f(s):
            ss = jnp.clip(s - shift, 0, n_real - 1)
            return ss // nj, ss % nj
        return f

    t1, t2, t3 = tile(0), tile(1), tile(2)
    return pl.pallas_call(
        functools.partial(_pdense_kernel, nj=nj, n_real=n_real),
        out_shape=jax.ShapeDtypeStruct((t, d), F32),
        grid=(n_real + 2,),
        in_specs=[
            pl.BlockSpec((tq, d), lambda s: (t1(s)[0], 0)),
            pl.BlockSpec((None, te, d), lambda s: (layer, t1(s)[1], 0)),
            pl.BlockSpec((None, te, d), lambda s: (layer, t3(s)[1], 0)),
            pl.BlockSpec((PEER_HEADS, N_KEYS, tq), lambda s: (0, 0, t2(s)[0])),
            pl.BlockSpec((PEER_HEADS, N_KEYS, tq), lambda s: (0, 0, t2(s)[0])),
            pl.BlockSpec((PEER_HEADS, na, tq), lambda s: (0, t2(s)[1], t2(s)[0])),
            pl.BlockSpec((PEER_HEADS, na, tq), lambda s: (0, t2(s)[1], t2(s)[0])),
            pl.BlockSpec((tq, d), lambda s: (t3(s)[0], 0)),
            rows.mod_spec(layer, 5, tq, tile_of=lambda s: t3(s)[0]),
        ],
        out_specs=pl.BlockSpec((tq, d), lambda s: (t3(s)[0], 0)),
        scratch_shapes=[pltpu.VMEM((te, tq), F32), pltpu.VMEM((te, tq), F32),
                        pltpu.VMEM((te, tq), BF16), pltpu.VMEM((te, tq), BF16),
                        pltpu.VMEM((tq, d), F32)],
        compiler_params=_cparams(1),
        name="peer_dense",
    )(h_bf, u_bf, v_bf, r2, e2, cut, e1, x, mod4)


def _final_kernel(x_ref, w_ref, o_ref):
    o_ref[...] = _rms(x_ref[...]) * w_ref[...]


def _final_call(x, w, row_block0, n_rows, tm=512):
    d = x.shape[1]
    return pl.pallas_call(
        _final_kernel,
        out_shape=jax.ShapeDtypeStruct((n_rows, d), F32),
        grid=(n_rows // tm,),
        in_specs=[pl.BlockSpec((tm, d), lambda i: (row_block0 + i, 0)),
                  pl.BlockSpec((1, d), lambda i: (0, 0))],
        out_specs=pl.BlockSpec((tm, d), lambda i: (i, 0)),
        compiler_params=_cparams(1),
        name="final_norm",
    )(x, w.reshape(1, d))


def _grid_pos_embed(rows, dtype):
    t = jnp.arange(rows * GRID_W)
    r = (t // GRID_W).astype(F32)
    col = (t % GRID_W).astype(F32)
    nf = D_MODEL // 4
    freqs = 1.0 / (10000.0 ** (jnp.arange(nf, dtype=F32) / nf))

    def emb(p):
        a = p[:, None] * freqs[None, :]
        return jnp.concatenate([jnp.sin(a), jnp.cos(a)], axis=-1)
    return jnp.concatenate([emb(r), emb(col)], axis=-1).astype(dtype)


def _even_w_in_layout(w):
    d = w.shape[0]
    a_end = 4 * W_A
    lr = w[:, a_end:a_end + GATE_RANK]
    uv = w[:, a_end + GATE_RANK:]
    pad = jnp.zeros((d, LANE - GATE_RANK), w.dtype)
    return jnp.concatenate([w[:, :a_end], uv, lr, pad], axis=1)


def _pad_gate_w(w):
    return jnp.concatenate([w, jnp.zeros((LANE - GATE_RANK, w.shape[1]), w.dtype)], axis=0)


def kernel(x_prompt, x_sample, state_gla, c, c_ctx, w_mod, b_mod, norm_w, final_norm_w, even_w_in, even_w_gate_f, even_b_gate_f, even_w_gate_b, even_b_gate_b, gla_norm_w, sgu_norm_w, sgu_w_s, sgu_b_s, even_w_out, odd_w_in, odd_conv_w, odd_w_out, peer_w_q, peer_sub_keys, peer_u, peer_v):
    bp, lp, d = x_prompt.shape
    bs, ls, _ = x_sample.shape
    depth = w_mod.shape[0]
    n_even = even_w_in.shape[0]
    tp = bp * lp
    ts = bs * ls
    rows = _Rows(tp, ls)

    x = _embed_call(x_prompt.reshape(tp, d), x_sample.reshape(ts, d),
                    _grid_pos_embed(ls // GRID_W, x_sample.dtype))

    cc = jnp.concatenate([c_ctx[None], c, jnp.zeros((MOD_ROWS - 1 - bs, d), F32)], axis=0)
    mod4 = _mod_call(cc, w_mod, b_mod).reshape(depth, MOD_ROWS, 1, 6 * d)

    cd, sd = _dft_mats(DD)
    clp, slp = _dft_mats(lp)
    cls, sls = _dft_mats(ls)
    s0_all = state_gla.reshape(bs, n_even, 2 * H_A, DK_A, DV_A)
    u_all = peer_u.astype(BF16)
    v_all = peer_v.astype(BF16)

    new_states = []
    for l in range(depth):
        if l % 2 == 0:
            e = l // 2
            z = _normproj_call(x, norm_w[l, 0], mod4, l, rows, _even_w_in_layout(even_w_in[e]).astype(BF16))
            gate_args = (_pad_gate_w(even_w_gate_f[e]), even_b_gate_f[e][None],
                         _pad_gate_w(even_w_gate_b[e]), even_b_gate_b[e][None], gla_norm_w[e][None])
            o_p, sf, sb = _gla_call(z, lp, bp, 0, *gate_args, want_state=True)
            (o_s,) = _gla_call(z, ls, bs, tp // ls, *gate_args,
                               s0=s0_all[:, e].reshape(bs * 2 * H_A, DK_A, DV_A))
            a1 = (o_p, o_s)
            bs_full = jnp.broadcast_to(sgu_b_s[e][:, :, None], (H_B, CHUNK_B, DB))
            a2 = _sgu_call(z, sgu_norm_w[e][None], sgu_w_s[e].astype(BF16), bs_full)
            x = _outproj_call(a1, a2, even_w_out[e].astype(BF16), x, mod4, l, rows)
            sfb = jnp.stack([sf.reshape(bp, H_A, DK_A, DV_A), sb.reshape(bp, H_A, DK_A, DV_A)], axis=1)
            new_states.append(sfb.astype(x_prompt.dtype))
        else:
            o = l // 2
            z = _normproj_call(x, norm_w[l, 0], mod4, l, rows, odd_w_in[o].astype(BF16))
            yc = (_conv_call(z, lp, bp, 0, odd_conv_w[o]), _conv_call(z, ls, bs, tp // ls, odd_conv_w[o]))
            yd = (_fft_call(z, lp, bp, 0, cd, sd, clp, slp), _fft_call(z, ls, bs, tp // ls, cd, sd, cls, sls))
            x = _outproj_call(yc, yd, odd_w_out[o].astype(BF16), x, mod4, l, rows)
        h_bf, r2, e2, cut, e1 = _psel_call(x, norm_w[l, 1], mod4, l, rows,
                                           peer_w_q[l].T.astype(BF16), peer_sub_keys[l])
        x = _pdense_call(h_bf, u_all, v_all, r2, e2, cut, e1, x, mod4, l, rows)

    y_prompt = _final_call(x, final_norm_w, 0, tp).reshape(bp, lp, d)
    y_sample = _final_call(x, final_norm_w, tp // 512, ts).reshape(bs, ls, d)
    state_new = jnp.stack(new_states, axis=1)
    return (y_prompt, y_sample, state_new)
```

```python
import functools
import math

import jax
import jax.numpy as jnp
import numpy as np
from jax import lax
from jax.experimental import pallas as pl
from jax.experimental.pallas import tpu as pltpu

F32 = jnp.float32
BF16 = jnp.bfloat16

D_MODEL = 1024
DEPTH = 4
GRID_W = 64
RMS_EPS = 1e-6
H_A = 4
DK_A = 128
DV_A = 128
GATE_RANK = 16
GATE_NORM = 16.0
GLA_CHUNK = 64
W_A = H_A * DV_A
H_B = 4
DB = 128
CHUNK_B = 128
W_B = H_B * DB
H_C = 4
DC = 128
W_C = H_C * DC
H_D = 4
DD = 128
W_D = H_D * DD
N_KEYS = 128
N_EXPERTS = N_KEYS * N_KEYS
PEER_HEADS = 8
PEER_TOPK = 16
D_KEY = 256

LANE = 128
MOD_ROWS = 8
EVEN_IN_PAD = 4 * W_A + 2 * W_B + LANE
VMEM_LIMIT = 48 * 1024 * 1024

NN = (((1,), (0,)), ((), ()))
NT = (((1,), (1,)), ((), ()))
TN = (((0,), (0,)), ((), ()))


def _cparams(n_axes, flags=None):
    return pltpu.CompilerParams(dimension_semantics=("arbitrary",) * n_axes,
                                vmem_limit_bytes=VMEM_LIMIT, flags=flags)


def _dot(a, b, dims=NN):
    return lax.dot_general(a, b, dims, preferred_element_type=F32)


def _split2(x):
    hi = x.astype(BF16)
    lo = (x - hi.astype(F32)).astype(BF16)
    return hi, lo


def _split3(x):
    hi = x.astype(BF16)
    r = x - hi.astype(F32)
    mid = r.astype(BF16)
    lo = (r - mid.astype(F32)).astype(BF16)
    return hi, mid, lo


def _dot3(a, b, dims=NN):
    ah, al = _split2(a)
    bh, bl = _split2(b)
    return _dot(ah, bh, dims) + _dot(al, bh, dims) + _dot(ah, bl, dims)


def _silu(x):
    return x * (1.0 / (1.0 + jnp.exp(-x)))


def _gelu(x):
    c = math.sqrt(2.0 / math.pi)
    return x * (0.5 * (1.0 + jnp.tanh(c * (x + 0.044715 * (x * x * x)))))


def _log_sigmoid(x):
    return jnp.minimum(x, 0.0) - jnp.log(1.0 + jnp.exp(-jnp.abs(x)))


def _rms(x):
    return x * lax.rsqrt(jnp.mean(x * x, axis=-1, keepdims=True) + RMS_EPS)


def _modnorm(x, nw, shift, scale):
    return (_rms(x) * nw) * (1.0 + scale) + shift


def _mod_kernel(cc_ref, w_ref, b_ref, o_ref):
    a = _silu(cc_ref[...])
    o_ref[0] = _dot3(a, w_ref[0]) + b_ref[0]


def _mod_call(cc, w_mod, b_mod):
    depth, d, n = w_mod.shape
    tn = 1024
    return pl.pallas_call(
        _mod_kernel,
        out_shape=jax.ShapeDtypeStruct((depth, MOD_ROWS, n), F32),
        grid=(depth, n // tn),
        in_specs=[
            pl.BlockSpec((MOD_ROWS, d), lambda l, j: (0, 0)),
            pl.BlockSpec((1, d, tn), lambda l, j: (l, 0, j)),
            pl.BlockSpec((1, 1, tn), lambda l, j: (l, 0, j)),
        ],
        out_specs=pl.BlockSpec((1, MOD_ROWS, tn), lambda l, j: (l, 0, j)),
        compiler_params=_cparams(2),
        name="mod_rows",
    )(cc, w_mod, b_mod.reshape(depth, 1, n))


class _Rows:
    def __init__(self, n_prompt_rows, dec_seq):
        self.n_prompt_rows = n_prompt_rows
        self.dec_seq = dec_seq

    def mod_row(self, i, tm):
        npt = self.n_prompt_rows // tm
        tps = self.dec_seq // tm
        return jnp.where(i < npt, 0, 1 + (i - npt) // tps)

    def mod_spec(self, layer, section, tm, tile_of=lambda i, *_: i):
        return pl.BlockSpec((1, 1, 1, D_MODEL),
                            lambda *g: (layer, self.mod_row(tile_of(*g), tm), 0, section))


def _normproj_kernel(x_ref, nw_ref, sh_ref, sc_ref, w_ref, o_ref):
    h = _modnorm(x_ref[...], nw_ref[...], sh_ref[0, 0], sc_ref[0, 0])
    o_ref[...] = _dot(h.astype(BF16), w_ref[...])


def _normproj_call(x, nw, mod4, layer, rows, w_bf, tm=256):
    t, d = x.shape
    n = w_bf.shape[1]
    return pl.pallas_call(
        _normproj_kernel,
        out_shape=jax.ShapeDtypeStruct((t, n), F32),
        grid=(t // tm,),
        in_specs=[
            pl.BlockSpec((tm, d), lambda i: (i, 0)),
            pl.BlockSpec((1, d), lambda i: (0, 0)),
            rows.mod_spec(layer, 0, tm),
            rows.mod_spec(layer, 1, tm),
            pl.BlockSpec((d, n), lambda i: (0, 0)),
        ],
        out_specs=pl.BlockSpec((tm, n), lambda i: (i, 0)),
        compiler_params=_cparams(1),
        name="normproj",
    )(x, nw.reshape(1, d), mod4, mod4, w_bf)


def _gla_kernel(*refs, has_s0, want_state):
    (q_ref, k_ref, v_ref, g_ref, lr_ref, wgf_ref, bgf_ref, wgb_ref, bgb_ref, nw_ref) = refs[:10]
    pos = 10
    if has_s0:
        s0f_ref, s0b_ref = refs[pos:pos + 2]
        pos += 2
    o_ref = refs[pos]
    pos += 1
    if want_state:
        sf_ref, sb_ref = refs[pos:pos + 2]
        pos += 2
    laf_scr, lab_scr, of_scr, ob_scr, s_scr = refs[pos:]

    seq = q_ref.shape[0]
    c = GLA_CHUNK
    per_group = 4
    gw = per_group * c
    n_groups = seq // gw
    scale = DK_A ** -0.5

    lr = lr_ref[...]
    laf_scr[...] = _log_sigmoid(_dot3(lr, wgf_ref[...]) + bgf_ref[...]) * (1.0 / GATE_NORM)
    lab_scr[...] = _log_sigmoid(_dot3(lr, wgb_ref[...]) + bgb_ref[...]) * (1.0 / GATE_NORM)
    if has_s0:
        s_scr[0] = s0f_ref[0].T
        s_scr[1] = s0b_ref[0].T
    else:
        s_scr[...] = jnp.zeros(s_scr.shape, F32)

    ri = lax.broadcasted_iota(jnp.int32, (gw, gw), 0)
    ci = lax.broadcasted_iota(jnp.int32, (gw, gw), 1)
    same = (ri // c) == (ci // c)
    lower = jnp.logical_and(same, ci <= ri)
    upper = jnp.logical_and(same, ci >= ri)
    tril = jnp.where(lower, 1.0, 0.0).astype(BF16)
    triu = jnp.where(upper, 1.0, 0.0).astype(BF16)
    ones = jnp.where(same, 1.0, 0.0).astype(BF16)

    la_scrs = (laf_scr, lab_scr)
    tris = (tril, triu)
    masks = (lower, upper)
    out_scrs = (of_scr, ob_scr)
    orders = (tuple(range(per_group)), tuple(reversed(range(per_group))))
    dirs = (0, 1)

    def body(i, carry):
        r0s = (pl.multiple_of(i * gw, gw), pl.multiple_of((n_groups - 1 - i) * gw, gw))
        rss = [pl.ds(r0, gw) for r0 in r0s]
        las = [_split3(la_scrs[d][rss[d], :]) for d in dirs]
        b = [_dot(tris[d], las[d][0]) + _dot(tris[d], las[d][1]) + _dot(tris[d], las[d][2])
             for d in dirs]
        be = [_dot(ones, las[d][0]) + _dot(ones, las[d][1]) + _dot(ones, las[d][2])
              for d in dirs]
        v = [v_ref[rss[d], :].astype(BF16) for d in dirs]
        qd = [(q_ref[rss[d], :] * scale * jnp.exp(b[d])).astype(BF16) for d in dirs]
        ki = [(k_ref[rss[d], :] * jnp.exp(-b[d])).astype(BF16) for d in dirs]
        ks = [(k_ref[rss[d], :] * jnp.exp(be[d] - b[d])).astype(BF16) for d in dirs]
        att = [jnp.where(masks[d], _dot(qd[d], ki[d], NT), 0.0).astype(BF16) for d in dirs]
        o_intra = [_dot(att[d], v[d]) for d in dirs]
        dec = [jnp.exp(be[d]) for d in dirs]
        st = [s_scr[d] for d in dirs]
        inter = [[None] * per_group for _ in dirs]
        for step in range(per_group):
            for d in dirs:
                ch = orders[d][step]
                cs = slice(ch * c, (ch + 1) * c)
                inter[d][ch] = _dot(qd[d][cs, :], st[d].astype(BF16), NT)
                st[d] = st[d] * dec[d][ch * c:ch * c + 1, :] + _dot(v[d][cs, :], ks[d][cs, :], TN)
        for d in dirs:
            s_scr[d] = st[d]
            out_scrs[d][rss[d], :] = o_intra[d] + jnp.concatenate(inter[d], axis=0)
        return carry

    lax.fori_loop(0, n_groups, body, 0)

    o = _rms(of_scr[...] + ob_scr[...]) * nw_ref[...]
    o_ref[...] = (o * _silu(g_ref[...])).astype(o_ref.dtype)
    if want_state:
        sf_ref[0] = s_scr[0].T
        sb_ref[0] = s_scr[1].T


def _gla_call(z, seq, n_seq, row_block0, wgf, bgf, wgb, bgb, nw, s0=None, want_state=False):
    has_s0 = s0 is not None

    def col(cb):
        return pl.BlockSpec((seq, LANE), lambda s, h: (row_block0 + s, cb + h))

    in_specs = [col(0), col(H_A), col(2 * H_A), col(3 * H_A),
                pl.BlockSpec((seq, LANE), lambda s, h: (row_block0 + s, (4 * W_A + 2 * W_B) // LANE)),
                pl.BlockSpec((LANE, DK_A), lambda s, h: (0, h)),
                pl.BlockSpec((1, DK_A), lambda s, h: (0, h)),
                pl.BlockSpec((LANE, DK_A), lambda s, h: (0, h)),
                pl.BlockSpec((1, DK_A), lambda s, h: (0, h)),
                pl.BlockSpec((1, DV_A), lambda s, h: (0, 0))]
    args = [z, z, z, z, z, wgf, bgf, wgb, bgb, nw]
    if has_s0:
        in_specs += [pl.BlockSpec((1, DK_A, DV_A), lambda s, h: ((s * 2 + 0) * H_A + h, 0, 0)),
                     pl.BlockSpec((1, DK_A, DV_A), lambda s, h: ((s * 2 + 1) * H_A + h, 0, 0))]
        args += [s0, s0]
    out_shape = [jax.ShapeDtypeStruct((n_seq * seq, W_A), BF16)]
    out_specs = [pl.BlockSpec((seq, LANE), lambda s, h: (s, h))]
    if want_state:
        out_shape += [jax.ShapeDtypeStruct((n_seq * H_A, DK_A, DV_A), F32)] * 2
        out_specs += [pl.BlockSpec((1, DK_A, DV_A), lambda s, h: (s * H_A + h, 0, 0))] * 2
    return pl.pallas_call(
        functools.partial(_gla_kernel, has_s0=has_s0, want_state=want_state),
        out_shape=out_shape,
        grid=(n_seq, H_A),
        in_specs=in_specs,
        out_specs=out_specs,
        scratch_shapes=[pltpu.VMEM((seq, DK_A), F32), pltpu.VMEM((seq, DK_A), F32),
                        pltpu.VMEM((seq, DV_A), F32), pltpu.VMEM((seq, DV_A), F32),
                        pltpu.VMEM((2, DK_A, DV_A), F32)],
        compiler_params=_cparams(2),
        name="gla_seq%d" % seq,
    )(*args)


def _sgu_kernel(u_ref, v_ref, nw_ref, ws_ref, bs_ref, o_ref):
    tm = u_ref.shape[0]
    u = _gelu(u_ref[...])
    v = (_rms(_gelu(v_ref[...])) * nw_ref[...]).astype(BF16)
    ws = ws_ref[0]
    bs = bs_ref[0]
    for cidx in range(tm // CHUNK_B):
        rs = slice(cidx * CHUNK_B, (cidx + 1) * CHUNK_B)
        sg = _dot(ws, v[rs, :]) + bs
        o_ref[rs, :] = (u[rs, :] * sg).astype(o_ref.dtype)


def _sgu_call(z, nw, ws_bf, bs_full, tm=512):
    t = z.shape[0]
    ub0 = (4 * W_A) // LANE
    vb0 = (4 * W_A + W_B) // LANE
    return pl.pallas_call(
        _sgu_kernel,
        out_shape=jax.ShapeDtypeStruct((t, W_B), BF16),
        grid=(t // tm, H_B),
        in_specs=[
            pl.BlockSpec((tm, LANE), lambda i, h: (i, ub0 + h)),
            pl.BlockSpec((tm, LANE), lambda i, h: (i, vb0 + h)),
            pl.BlockSpec((1, DB), lambda i, h: (0, 0)),
            pl.BlockSpec((1, CHUNK_B, CHUNK_B), lambda i, h: (h, 0, 0)),
            pl.BlockSpec((1, CHUNK_B, DB), lambda i, h: (h, 0, 0)),
        ],
        out_specs=pl.BlockSpec((tm, LANE), lambda i, h: (i, h)),
        compiler_params=_cparams(2),
        name="sgu",
    )(z, z, nw, ws_bf, bs_full)


def _outproj_kernel(a1p_ref, a1s_ref, a2p_ref, a2s_ref, w_ref, x_ref, g_ref, o_ref, *, n_prompt_tiles):
    n1 = a1p_ref.shape[1]

    def run(a1_ref, a2_ref):
        y = _dot(a1_ref[...], w_ref[:n1, :]) + _dot(a2_ref[...], w_ref[n1:, :])
        o_ref[...] = x_ref[...] + g_ref[0, 0] * y

    @pl.when(pl.program_id(0) < n_prompt_tiles)
    def _():
        run(a1p_ref, a2p_ref)

    @pl.when(pl.program_id(0) >= n_prompt_tiles)
    def _():
        run(a1s_ref, a2s_ref)


def _outproj_call(a1, a2, w_bf, x, mod4, layer, rows, tm=512):
    t, d = x.shape
    npt = rows.n_prompt_rows // tm

    def parts(a):
        if isinstance(a, tuple):
            n = a[0].shape[1]
            return (a[0], a[1],
                    pl.BlockSpec((tm, n), lambda i: (jnp.minimum(i, npt - 1), 0)),
                    pl.BlockSpec((tm, n), lambda i: (jnp.maximum(i - npt, 0), 0)))
        spec = pl.BlockSpec((tm, a.shape[1]), lambda i: (i, 0))
        return a, a, spec, spec

    a1p, a1s, s1p, s1s = parts(a1)
    a2p, a2s, s2p, s2s = parts(a2)
    n1, n2 = a1p.shape[1], a2p.shape[1]
    return pl.pallas_call(
        functools.partial(_outproj_kernel, n_prompt_tiles=npt),
        out_shape=jax.ShapeDtypeStruct((t, d), F32),
        grid=(t // tm,),
        in_specs=[
            s1p, s1s, s2p, s2s,
            pl.BlockSpec((n1 + n2, d), lambda i: (0, 0)),
            pl.BlockSpec((tm, d), lambda i: (i, 0)),
            rows.mod_spec(layer, 2, tm),
        ],
        out_specs=pl.BlockSpec((tm, d), lambda i: (i, 0)),
        compiler_params=_cparams(1),
        name="outproj",
    )(a1p, a1s, a2p, a2s, w_bf, x, mod4)


def _embed_kernel(xp_ref, xs_ref, pe_ref, o_ref, *, n_prompt_tiles):
    @pl.when(pl.program_id(0) < n_prompt_tiles)
    def _():
        o_ref[...] = xp_ref[...]

    @pl.when(pl.program_id(0) >= n_prompt_tiles)
    def _():
        o_ref[...] = xs_ref[...] + pe_ref[...]


def _embed_call(xp, xs, pe, tm=512):
    tp, d = xp.shape
    ts = xs.shape[0]
    npt = tp // tm
    pe_tiles = pe.shape[0] // tm
    return pl.pallas_call(
        functools.partial(_embed_kernel, n_prompt_tiles=npt),
        out_shape=jax.ShapeDtypeStruct((tp + ts, d), F32),
        grid=((tp + ts) // tm,),
        in_specs=[
            pl.BlockSpec((tm, d), lambda i: (jnp.minimum(i, npt - 1), 0)),
            pl.BlockSpec((tm, d), lambda i: (jnp.maximum(i - npt, 0), 0)),
            pl.BlockSpec((tm, d), lambda i: (jnp.maximum(i - npt, 0) % pe_tiles, 0)),
        ],
        out_specs=pl.BlockSpec((tm, d), lambda i: (i, 0)),
        compiler_params=_cparams(1),
        name="embed",
    )(xp, xs, pe)


def _conv_kernel(bg_ref, cg_ref, xi_ref, w_ref, o_ref):
    seq = bg_ref.shape[0]
    t = cg_ref[...] * xi_ref[...]
    row = lax.broadcasted_iota(jnp.int32, t.shape, 0)
    t_prev = jnp.where(row == 0, 0.0, pltpu.roll(t, 1, 0))
    t_next = jnp.where(row == seq - 1, 0.0, pltpu.roll(t, seq - 1, 0))
    conv = t_prev * w_ref[0:1, :] + t * w_ref[1:2, :] + t_next * w_ref[2:3, :]
    o_ref[...] = (bg_ref[...] * conv).astype(o_ref.dtype)


def _conv_call(z, seq, n_seq, row_block0, conv_w):
    def col(cb):
        return pl.BlockSpec((seq, LANE), lambda s, j: (row_block0 + s, cb + j))
    return pl.pallas_call(
        _conv_kernel,
        out_shape=jax.ShapeDtypeStruct((n_seq * seq, W_C), BF16),
        grid=(n_seq, W_C // LANE),
        in_specs=[col(0), col(H_C), col(2 * H_C),
                  pl.BlockSpec((3, LANE), lambda s, j: (0, j))],
        out_specs=pl.BlockSpec((seq, LANE), lambda s, j: (s, j)),
        compiler_params=_cparams(2),
        name="conv_seq%d" % seq,
    )(z, z, z, conv_w)


def _fft_kernel(f_ref, cd_ref, sd_ref, cl_ref, sl_ref, o_ref, p_scr, q_scr, *, scale):
    @pl.when(pl.program_id(1) == 0)
    def _():
        cd = cd_ref[...]
        sd = sd_ref[...]
        for h in range(H_D):
            cs = slice(h * DD, (h + 1) * DD)
            fh = f_ref[:, cs].astype(BF16)
            p_scr[:, cs] = _dot(fh, cd).astype(BF16)
            q_scr[:, cs] = _dot(fh, sd).astype(BF16)

    y = _dot(cl_ref[...], p_scr[...]) - _dot(sl_ref[...], q_scr[...])
    o_ref[...] = (y * scale).astype(o_ref.dtype)


def _dft_mats(n):
    idx = jnp.arange(n, dtype=jnp.int32)
    ang = ((idx[:, None] * idx[None, :]) % n).astype(F32) * (2.0 * math.pi / n)
    return jnp.cos(ang).astype(BF16), jnp.sin(ang).astype(BF16)


def _fft_call(z, seq, n_seq, row_block0, cd, sd, cl, sl):
    tl = min(seq, 512)
    fcol = (3 * W_C) // W_D
    return pl.pallas_call(
        functools.partial(_fft_kernel, scale=1.0 / math.sqrt(seq * DD)),
        out_shape=jax.ShapeDtypeStruct((n_seq * seq, W_D), BF16),
        grid=(n_seq, seq // tl),
        in_specs=[
            pl.BlockSpec((seq, W_D), lambda s, i: (row_block0 + s, fcol)),
            pl.BlockSpec((DD, DD), lambda s, i: (0, 0)),
            pl.BlockSpec((DD, DD), lambda s, i: (0, 0)),
            pl.BlockSpec((tl, seq), lambda s, i: (i, 0)),
            pl.BlockSpec((tl, seq), lambda s, i: (i, 0)),
        ],
        out_specs=pl.BlockSpec((tl, W_D), lambda s, i: (s * (seq // tl) + i, 0)),
        scratch_shapes=[pltpu.VMEM((seq, W_D), BF16), pltpu.VMEM((seq, W_D), BF16)],
        compiler_params=_cparams(2),
        name="fft_seq%d" % seq,
    )(z, cd, sd, cl, sl)


def _topk16(s):
    nrows = s.shape[0]
    iota = lax.broadcasted_iota(jnp.int32, s.shape, 0)
    i16 = lax.broadcasted_iota(jnp.int32, (PEER_TOPK, s.shape[1]), 0)

    def body(k, carry):
        s, rank, top = carry
        m = jnp.max(s, axis=0, keepdims=True)
        idx = jnp.min(jnp.where(s == m, iota, nrows), axis=0, keepdims=True)
        sel = iota == idx
        rank = jnp.where(sel, k, rank)
        s = jnp.where(sel, -jnp.inf, s)
        top = jnp.where(i16 == k, m, top)
        return s, rank, top

    init = (s, jnp.full(s.shape, PEER_TOPK, jnp.int32), jnp.zeros((PEER_TOPK, s.shape[1]), F32))
    _, rank, top = lax.fori_loop(0, PEER_TOPK, body, init)
    return rank, top


_CAND_GROUPS = [(0, 16), (1, 8)] + [(k1, 8) for k1 in range(2, 8)]
_CAND_SINGLE0 = sum(n for _, n in _CAND_GROUPS)


_SCORE_FLOOR = -(2.0 ** 126)
_CODE_UNIT = 2.0 ** 127


def _rank_code(k):
    return -_CODE_UNIT * (1.0 + (k + 1) / 32.0)


def _topk16_untied(scores):
    n = scores[0].shape[1]
    i16 = lax.broadcasted_iota(jnp.int32, (PEER_TOPK, n), 0)
    keys = [jnp.maximum(s, _SCORE_FLOOR) for s in scores]
    tops = [jnp.zeros((PEER_TOPK, n), F32) for _ in scores]
    for k in range(PEER_TOPK):
        for c in range(len(keys)):
            m = jnp.max(keys[c], axis=0, keepdims=True)
            keys[c] = jnp.where(keys[c] == m, _rank_code(k), keys[c])
            tops[c] = jnp.where(i16 == k, m, tops[c])
    out = []
    for key, top in zip(keys, tops):
        taken = key < _SCORE_FLOOR
        rank = jnp.where(taken, key * -(32.0 / _CODE_UNIT) - 33.0, float(PEER_TOPK))
        count = jnp.sum(jnp.where(taken, 1.0, 0.0), axis=0, keepdims=True)
        clamped = jnp.sum(jnp.where(top <= _SCORE_FLOOR, 1.0, 0.0), axis=0, keepdims=True)
        out.append((rank, top, (count - PEER_TOPK) + clamped))
    return out


def _cand_scores(t1, t2):
    n = t1.shape[1]
    neg = jnp.full((8, n), -jnp.inf, F32)
    row8 = lax.broadcasted_iota(jnp.int32, (8, n), 0)
    pieces = []
    for k1, nrow in _CAND_GROUPS:
        blk = t1[k1:k1 + 1, :] + t2[0:nrow, :]
        nvalid = PEER_TOPK // (k1 + 1)
        if nvalid < nrow:
            blk = jnp.where(row8 < nvalid, blk, neg)
        pieces.append(blk)
    pieces.append(t1[8:16, :] + t2[0:1, :])
    return jnp.concatenate(pieces, axis=0)


def _psel_finish(s1, s2, r1, t1, t2, cand, sel):
    n = s1.shape[1]
    cmax = t1[0:1, :] + t2[0:1, :]
    z = jnp.sum(jnp.where(sel, jnp.exp(cand - cmax), 0.0), axis=0, keepdims=True)
    self32 = jnp.where(sel, 1.0, 0.0)
    cut = jnp.zeros((N_KEYS, n), F32)
    r0 = 0
    for k1, nrow in _CAND_GROUPS:
        cnt = jnp.sum(self32[r0:r0 + nrow, :], axis=0, keepdims=True)
        cut = jnp.where(r1 == k1, cnt, cut)
        r0 += nrow
    for j in range(8):
        cut = jnp.where(r1 == 8 + j, self32[r0 + j:r0 + j + 1, :], cut)
    e1 = jnp.exp(s1 - t1[0:1, :]) * (1.0 / z)
    e2 = jnp.exp(s2 - t2[0:1, :])
    return cut, e1, e2


def _psel_kernel(x_ref, nw_ref, sh_ref, sc_ref, wq_ref, sk_ref,
                 h_ref, r2_ref, e2_ref, cut_ref, e1_ref, q_scr, hb_scr):
    tq = x_ref.shape[0]
    n_chunks = tq // LANE
    hb = _modnorm(x_ref[...], nw_ref[...], sh_ref[0, 0], sc_ref[0, 0]).astype(BF16)
    h_ref[...] = hb
    hb_scr[...] = hb
    half = D_KEY // 2

    def project(hh, slot):
        rows_ = pl.ds(pl.multiple_of(hh * D_KEY, D_KEY), D_KEY)
        q_scr[slot] = _dot(wq_ref[rows_, :], hb_scr[...], NT)

    project(0, 0)

    def scores(hh):
        slot = hh % 2
        return (_dot3(sk_ref[0], q_scr[slot, pl.ds(0, half), :]),
                _dot3(sk_ref[1], q_scr[slot, pl.ds(half, half), :]))

    def write(hh, ls, r2, cut, e1, e2):
        r2_ref[hh, :, ls] = r2.astype(BF16)
        e2_ref[hh, :, ls] = e2.astype(BF16)
        cut_ref[hh, :, ls] = cut
        e1_ref[hh, :, ls] = e1

    def exact_chunk(hh, ls, s1c, s2c):
        r1, t1 = _topk16(s1c)
        r2, t2 = _topk16(s2c)
        cand = _cand_scores(t1, t2)
        crank, _ = _topk16(cand)
        cut, e1, e2 = _psel_finish(s1c, s2c, r1.astype(F32), t1, t2, cand, crank < PEER_TOPK)
        write(hh, ls, r2.astype(F32), cut, e1, e2)

    def head(hh, carry):
        s1, s2 = scores(hh)
        halves = []
        for lc in range(n_chunks):
            ls = slice(lc * LANE, (lc + 1) * LANE)
            halves.append(_topk16_untied([s1[:, ls], s2[:, ls]]))
            if lc == 0:
                project(jnp.minimum(hh + 1, PEER_HEADS - 1), 1 - hh % 2)
        cands = [_cand_scores(h1[1], h2[1]) for h1, h2 in halves]
        picked = _topk16_untied(cands)
        for lc in range(n_chunks):
            ls = slice(lc * LANE, (lc + 1) * LANE)
            (r1, t1, c1), (r2, t2, c2) = halves[lc]
            crank, _, cc = picked[lc]
            cut, e1, e2 = _psel_finish(s1[:, ls], s2[:, ls], r1, t1, t2, cands[lc], crank < PEER_TOPK)
            write(hh, ls, r2, cut, e1, e2)
            extra = c1 + c2 + cc

            @pl.when(jnp.max(extra) > 0)
            def _():
                exact_chunk(hh, ls, s1[:, ls], s2[:, ls])
        return carry

    lax.fori_loop(0, PEER_HEADS, head, 0)


def _psel_call(x, nw, mod4, layer, rows, wqt_bf, sub_keys, tq=256):
    t, d = x.shape
    nq = wqt_bf.shape[0]
    sel_shape = jax.ShapeDtypeStruct((t // tq, PEER_HEADS, N_KEYS, tq), F32)
    sel_bf = jax.ShapeDtypeStruct((t // tq, PEER_HEADS, N_KEYS, tq), BF16)
    sel_spec = pl.BlockSpec((None, PEER_HEADS, N_KEYS, tq), lambda i: (i, 0, 0, 0))
    return pl.pallas_call(
        _psel_kernel,
        out_shape=[jax.ShapeDtypeStruct((t, d), BF16), sel_bf, sel_bf, sel_shape, sel_shape],
        grid=(t // tq,),
        in_specs=[
            pl.BlockSpec((tq, d), lambda i: (i, 0)),
            pl.BlockSpec((1, d), lambda i: (0, 0)),
            rows.mod_spec(layer, 3, tq),
            rows.mod_spec(layer, 4, tq),
            pl.BlockSpec((nq, d), lambda i: (0, 0)),
            pl.BlockSpec((2, N_KEYS, D_KEY // 2), lambda i: (0, 0, 0)),
        ],
        out_specs=[pl.BlockSpec((tq, d), lambda i: (i, 0)), sel_spec, sel_spec, sel_spec, sel_spec],
        scratch_shapes=[pltpu.VMEM((2, D_KEY, tq), F32), pltpu.VMEM((tq, d), BF16)],
        compiler_params=_cparams(1),
        name="peer_select",
    )(x, nw.reshape(1, d), mod4, mod4, wqt_bf, sub_keys)


def _gelu_sigmoid_form(x):
    k = 2.0 * math.sqrt(2.0 / math.pi) * math.log2(math.e)
    u = x * (-k - (k * 0.044715) * (x * x))
    return x * (1.0 / (1.0 + jnp.exp2(u)))


def _pdense_kernel(h_ref, u_ref, v_ref, r2_ref, e2_ref, cut_ref, e1_ref, x_ref, g_ref, o_ref,
                   hid0, hid1, w0, w1, acc_scr, *, nj, n_real):
    s = pl.program_id(0)
    s3 = jnp.clip(s - 2, 0, n_real - 1)
    j3 = s3 % nj
    te, tq = hid0.shape
    bzero = jnp.zeros((), BF16)

    @pl.when(s == 0)
    def _():
        for r in (hid0, hid1, w0, w1):
            r[...] = jnp.zeros(r.shape, r.dtype)

    @pl.when(j3 == 0)
    def _():
        acc_scr[...] = jnp.zeros(acc_scr.shape, F32)

    def step(hid_new, hid_old, w_new, w_old):
        wide = 2 * LANE
        n_lc = tq // wide
        dcols = acc_scr.shape[1] // n_lc
        for a in range(te // N_KEYS):
            rs = slice(a * N_KEYS, (a + 1) * N_KEYS)
            rs2 = slice((a - a % 2) * N_KEYS, (a - a % 2 + 2) * N_KEYS)
            for lc in range(n_lc):
                ls = slice(lc * wide, (lc + 1) * wide)
                half = N_KEYS // 4
                bcast = [(jnp.broadcast_to(cut_ref[lc, hh, a:a + 1, :], (half, wide)).astype(BF16),
                          jnp.broadcast_to(e1_ref[lc, hh, a:a + 1, :], (half, wide)).astype(BF16))
                         for hh in range(PEER_HEADS)]
                for rb in range(N_KEYS // half):
                    ks_ = slice(rb * half, (rb + 1) * half)
                    rows_ = slice(a * N_KEYS + rb * half, a * N_KEYS + (rb + 1) * half)
                    gate = None
                    for hh in range(PEER_HEADS):
                        cut, e1 = bcast[hh]
                        term = jnp.where(r2_ref[lc, hh, ks_, :] < cut, e2_ref[lc, hh, ks_, :] * e1, bzero)
                        gate = term if gate is None else gate + term
                    w_new[rows_, ls] = _gelu_sigmoid_form(hid_old[rows_, ls].astype(BF16)) * gate
                if a % 2 == 0:
                    hid_new[rs2, ls] = _dot(u_ref[rs2, :], h_ref[ls, :], NT)
                else:
                    ds_ = slice(lc * dcols, (lc + 1) * dcols)
                    acc_scr[:, ds_] += _dot(w_old[rs2, :], v_ref[rs2, ds_], TN)

    @pl.when(s % 2 == 0)
    def _():
        step(hid0, hid1, w0, w1)

    @pl.when(s % 2 == 1)
    def _():
        step(hid1, hid0, w1, w0)

    @pl.when(jnp.logical_and(j3 == nj - 1, s >= 2))
    def _():
        o_ref[...] = x_ref[...] + g_ref[0, 0] * acc_scr[...]


def _pdense_call(h_bf, u_bf, v_bf, r2, e2, cut, e1, x, mod4, layer, rows, tq=512, te=2048):
    t, d = x.shape
    ne = u_bf.shape[1]
    ni, nj = t // tq, ne // te
    n_real = ni * nj
    na = te // N_KEYS
    sel_w = r2.shape[-1]
    assert sel_w == 2 * LANE and tq % sel_w == 0
    n_sel = tq // sel_w

    def tile(shift):
        def f(s):
            ss = jnp.clip(s - shift, 0, n_real - 1)
            return ss // nj, ss % nj
        return f

    t1, t2, t3 = tile(0), tile(1), tile(2)
    return pl.pallas_call(
        functools.partial(_pdense_kernel, nj=nj, n_real=n_real),
        out_shape=jax.ShapeDtypeStruct((t, d), F32),
        grid=(n_real + 2,),
        in_specs=[
            pl.BlockSpec((tq, d), lambda s: (t1(s)[0], 0)),
            pl.BlockSpec((None, te, d), lambda s: (layer, t1(s)[1], 0)),
            pl.BlockSpec((None, te, d), lambda s: (layer, t3(s)[1], 0)),
            pl.BlockSpec((n_sel, PEER_HEADS, N_KEYS, sel_w), lambda s: (t2(s)[0], 0, 0, 0)),
            pl.BlockSpec((n_sel, PEER_HEADS, N_KEYS, sel_w), lambda s: (t2(s)[0], 0, 0, 0)),
            pl.BlockSpec((n_sel, PEER_HEADS, na, sel_w), lambda s: (t2(s)[0], 0, t2(s)[1], 0)),
            pl.BlockSpec((n_sel, PEER_HEADS, na, sel_w), lambda s: (t2(s)[0], 0, t2(s)[1], 0)),
            pl.BlockSpec((tq, d), lambda s: (t3(s)[0], 0)),
            rows.mod_spec(layer, 5, tq, tile_of=lambda s: t3(s)[0]),
        ],
        out_specs=pl.BlockSpec((tq, d), lambda s: (t3(s)[0], 0)),
        scratch_shapes=[pltpu.VMEM((te, tq), F32), pltpu.VMEM((te, tq), F32),
                        pltpu.VMEM((te, tq), BF16), pltpu.VMEM((te, tq), BF16),
                        pltpu.VMEM((tq, d), F32)],
        compiler_params=_cparams(1),
        name="peer_dense",
    )(h_bf, u_bf, v_bf, r2, e2, cut, e1, x, mod4)


def _final_kernel(x_ref, w_ref, o_ref):
    o_ref[...] = _rms(x_ref[...]) * w_ref[...]


def _final_call(x, w, row_block0, n_rows, tm=512):
    d = x.shape[1]
    return pl.pallas_call(
        _final_kernel,
        out_shape=jax.ShapeDtypeStruct((n_rows, d), F32),
        grid=(n_rows // tm,),
        in_specs=[pl.BlockSpec((tm, d), lambda i: (row_block0 + i, 0)),
                  pl.BlockSpec((1, d), lambda i: (0, 0))],
        out_specs=pl.BlockSpec((tm, d), lambda i: (i, 0)),
        compiler_params=_cparams(1),
        name="final_norm",
    )(x, w.reshape(1, d))


def _grid_pos_embed(rows, dtype):
    t = jnp.arange(rows * GRID_W)
    r = (t // GRID_W).astype(F32)
    col = (t % GRID_W).astype(F32)
    nf = D_MODEL // 4
    freqs = 1.0 / (10000.0 ** (jnp.arange(nf, dtype=F32) / nf))

    def emb(p):
        a = p[:, None] * freqs[None, :]
        return jnp.concatenate([jnp.sin(a), jnp.cos(a)], axis=-1)
    return jnp.concatenate([emb(r), emb(col)], axis=-1).astype(dtype)


def _even_w_in_layout(w):
    d = w.shape[0]
    a_end = 4 * W_A
    lr = w[:, a_end:a_end + GATE_RANK]
    uv = w[:, a_end + GATE_RANK:]
    pad = jnp.zeros((d, LANE - GATE_RANK), w.dtype)
    return jnp.concatenate([w[:, :a_end], uv, lr, pad], axis=1)


def _pad_gate_w(w):
    return jnp.concatenate([w, jnp.zeros((LANE - GATE_RANK, w.shape[1]), w.dtype)], axis=0)


def kernel(x_prompt, x_sample, state_gla, c, c_ctx, w_mod, b_mod, norm_w, final_norm_w, even_w_in, even_w_gate_f, even_b_gate_f, even_w_gate_b, even_b_gate_b, gla_norm_w, sgu_norm_w, sgu_w_s, sgu_b_s, even_w_out, odd_w_in, odd_conv_w, odd_w_out, peer_w_q, peer_sub_keys, peer_u, peer_v):
    bp, lp, d = x_prompt.shape
    bs, ls, _ = x_sample.shape
    depth = w_mod.shape[0]
    n_even = even_w_in.shape[0]
    tp = bp * lp
    ts = bs * ls
    rows = _Rows(tp, ls)

    x = _embed_call(x_prompt.reshape(tp, d), x_sample.reshape(ts, d),
                    _grid_pos_embed(ls // GRID_W, x_sample.dtype))

    cc = jnp.concatenate([c_ctx[None], c, jnp.zeros((MOD_ROWS - 1 - bs, d), F32)], axis=0)
    mod4 = _mod_call(cc, w_mod, b_mod).reshape(depth, MOD_ROWS, 1, 6 * d)

    cd, sd = _dft_mats(DD)
    clp, slp = _dft_mats(lp)
    cls, sls = _dft_mats(ls)
    s0_all = state_gla.reshape(bs, n_even, 2 * H_A, DK_A, DV_A)
    u_all = peer_u.astype(BF16)
    v_all = peer_v.astype(BF16)

    new_states = []
    for l in range(depth):
        if l % 2 == 0:
            e = l // 2
            z = _normproj_call(x, norm_w[l, 0], mod4, l, rows, _even_w_in_layout(even_w_in[e]).astype(BF16))
            gate_args = (_pad_gate_w(even_w_gate_f[e]), even_b_gate_f[e][None],
                         _pad_gate_w(even_w_gate_b[e]), even_b_gate_b[e][None], gla_norm_w[e][None])
            o_p, sf, sb = _gla_call(z, lp, bp, 0, *gate_args, want_state=True)
            (o_s,) = _gla_call(z, ls, bs, tp // ls, *gate_args,
                               s0=s0_all[:, e].reshape(bs * 2 * H_A, DK_A, DV_A))
            a1 = (o_p, o_s)
            bs_full = jnp.broadcast_to(sgu_b_s[e][:, :, None], (H_B, CHUNK_B, DB))
            a2 = _sgu_call(z, sgu_norm_w[e][None], sgu_w_s[e].astype(BF16), bs_full)
            x = _outproj_call(a1, a2, even_w_out[e].astype(BF16), x, mod4, l, rows)
            sfb = jnp.stack([sf.reshape(bp, H_A, DK_A, DV_A), sb.reshape(bp, H_A, DK_A, DV_A)], axis=1)
            new_states.append(sfb.astype(x_prompt.dtype))
        else:
            o = l // 2
            z = _normproj_call(x, norm_w[l, 0], mod4, l, rows, odd_w_in[o].astype(BF16))
            yc = (_conv_call(z, lp, bp, 0, odd_conv_w[o]), _conv_call(z, ls, bs, tp // ls, odd_conv_w[o]))
            yd = (_fft_call(z, lp, bp, 0, cd, sd, clp, slp), _fft_call(z, ls, bs, tp // ls, cd, sd, cls, sls))
            x = _outproj_call(yc, yd, odd_w_out[o].astype(BF16), x, mod4, l, rows)
        h_bf, r2, e2, cut, e1 = _psel_call(x, norm_w[l, 1], mod4, l, rows,
                                           peer_w_q[l].T.astype(BF16), peer_sub_keys[l])
        x = _pdense_call(h_bf, u_all, v_all, r2, e2, cut, e1, x, mod4, l, rows)

    y_prompt = _final_call(x, final_norm_w, 0, tp).reshape(bp, lp, d)
    y_sample = _final_call(x, final_norm_w, tp // 512, ts).reshape(bs, ls, d)
    state_new = jnp.stack(new_states, axis=1)
    return (y_prompt, y_sample, state_new)
```

```python
import functools
import math

import jax
import jax.numpy as jnp
import numpy as np
from jax import lax
from jax.experimental import pallas as pl
from jax.experimental.pallas import tpu as pltpu

F32 = jnp.float32
BF16 = jnp.bfloat16

D_MODEL = 1024
DEPTH = 4
GRID_W = 64
RMS_EPS = 1e-6
H_A = 4
DK_A = 128
DV_A = 128
GATE_RANK = 16
GATE_NORM = 16.0
GLA_CHUNK = 64
W_A = H_A * DV_A
H_B = 4
DB = 128
CHUNK_B = 128
W_B = H_B * DB
H_C = 4
DC = 128
W_C = H_C * DC
H_D = 4
DD = 128
W_D = H_D * DD
N_KEYS = 128
N_EXPERTS = N_KEYS * N_KEYS
PEER_HEADS = 8
PEER_TOPK = 16
D_KEY = 256

LANE = 128
MOD_ROWS = 8
EVEN_IN_PAD = 4 * W_A + 2 * W_B + LANE
VMEM_LIMIT = 48 * 1024 * 1024

NN = (((1,), (0,)), ((), ()))
NT = (((1,), (1,)), ((), ()))
TN = (((0,), (0,)), ((), ()))


def _cparams(n_axes, flags=None):
    return pltpu.CompilerParams(dimension_semantics=("arbitrary",) * n_axes,
                                vmem_limit_bytes=VMEM_LIMIT, flags=flags)


def _dot(a, b, dims=NN):
    return lax.dot_general(a, b, dims, preferred_element_type=F32)


def _split2(x):
    hi = x.astype(BF16)
    lo = (x - hi.astype(F32)).astype(BF16)
    return hi, lo


def _split3(x):
    hi = x.astype(BF16)
    r = x - hi.astype(F32)
    mid = r.astype(BF16)
    lo = (r - mid.astype(F32)).astype(BF16)
    return hi, mid, lo


def _dot3(a, b, dims=NN):
    ah, al = _split2(a)
    bh, bl = _split2(b)
    return _dot(ah, bh, dims) + _dot(al, bh, dims) + _dot(ah, bl, dims)


def _silu(x):
    return x * (1.0 / (1.0 + jnp.exp(-x)))


def _gelu(x):
    c = math.sqrt(2.0 / math.pi)
    return x * (0.5 * (1.0 + jnp.tanh(c * (x + 0.044715 * (x * x * x)))))


def _log_sigmoid(x):
    return jnp.minimum(x, 0.0) - jnp.log(1.0 + jnp.exp(-jnp.abs(x)))


def _rms(x):
    return x * lax.rsqrt(jnp.mean(x * x, axis=-1, keepdims=True) + RMS_EPS)


def _modnorm(x, nw, shift, scale):
    return (_rms(x) * nw) * (1.0 + scale) + shift


def _mod_kernel(cc_ref, w_ref, b_ref, o_ref):
    a = _silu(cc_ref[...])
    o_ref[0] = _dot3(a, w_ref[0]) + b_ref[0]


def _mod_call(cc, w_mod, b_mod):
    depth, d, n = w_mod.shape
    tn = 1024
    return pl.pallas_call(
        _mod_kernel,
        out_shape=jax.ShapeDtypeStruct((depth, MOD_ROWS, n), F32),
        grid=(depth, n // tn),
        in_specs=[
            pl.BlockSpec((MOD_ROWS, d), lambda l, j: (0, 0)),
            pl.BlockSpec((1, d, tn), lambda l, j: (l, 0, j)),
            pl.BlockSpec((1, 1, tn), lambda l, j: (l, 0, j)),
        ],
        out_specs=pl.BlockSpec((1, MOD_ROWS, tn), lambda l, j: (l, 0, j)),
        compiler_params=_cparams(2),
        name="mod_rows",
    )(cc, w_mod, b_mod.reshape(depth, 1, n))


class _Rows:
    def __init__(self, n_prompt_rows, dec_seq):
        self.n_prompt_rows = n_prompt_rows
        self.dec_seq = dec_seq

    def mod_row(self, i, tm):
        npt = self.n_prompt_rows // tm
        tps = self.dec_seq // tm
        return jnp.where(i < npt, 0, 1 + (i - npt) // tps)

    def mod_spec(self, layer, section, tm, tile_of=lambda i, *_: i):
        return pl.BlockSpec((1, 1, 1, D_MODEL),
                            lambda *g: (layer, self.mod_row(tile_of(*g), tm), 0, section))


def _normproj_kernel(x_ref, nw_ref, sh_ref, sc_ref, w_ref, o_ref):
    h = _modnorm(x_ref[...], nw_ref[...], sh_ref[0, 0], sc_ref[0, 0])
    o_ref[...] = _dot(h.astype(BF16), w_ref[...])


def _normproj_call(x, nw, mod4, layer, rows, w_bf, tm=256):
    t, d = x.shape
    n = w_bf.shape[1]
    return pl.pallas_call(
        _normproj_kernel,
        out_shape=jax.ShapeDtypeStruct((t, n), F32),
        grid=(t // tm,),
        in_specs=[
            pl.BlockSpec((tm, d), lambda i: (i, 0)),
            pl.BlockSpec((1, d), lambda i: (0, 0)),
            rows.mod_spec(layer, 0, tm),
            rows.mod_spec(layer, 1, tm),
            pl.BlockSpec((d, n), lambda i: (0, 0)),
        ],
        out_specs=pl.BlockSpec((tm, n), lambda i: (i, 0)),
        compiler_params=_cparams(1),
        name="normproj",
    )(x, nw.reshape(1, d), mod4, mod4, w_bf)


def _gla_kernel(*refs, has_s0, want_state):
    (q_ref, k_ref, v_ref, g_ref, lr_ref, wgf_ref, bgf_ref, wgb_ref, bgb_ref, nw_ref) = refs[:10]
    pos = 10
    if has_s0:
        s0f_ref, s0b_ref = refs[pos:pos + 2]
        pos += 2
    o_ref = refs[pos]
    pos += 1
    if want_state:
        sf_ref, sb_ref = refs[pos:pos + 2]
        pos += 2
    laf_scr, lab_scr, of_scr, ob_scr, s_scr = refs[pos:]

    seq = q_ref.shape[0]
    c = GLA_CHUNK
    per_group = 4
    gw = per_group * c
    n_groups = seq // gw
    scale = DK_A ** -0.5

    lr = lr_ref[...]
    laf_scr[...] = _log_sigmoid(_dot3(lr, wgf_ref[...]) + bgf_ref[...]) * (1.0 / GATE_NORM)
    lab_scr[...] = _log_sigmoid(_dot3(lr, wgb_ref[...]) + bgb_ref[...]) * (1.0 / GATE_NORM)
    if has_s0:
        s_scr[0] = s0f_ref[0].T
        s_scr[1] = s0b_ref[0].T
    else:
        s_scr[...] = jnp.zeros(s_scr.shape, F32)

    ri = lax.broadcasted_iota(jnp.int32, (gw, gw), 0)
    ci = lax.broadcasted_iota(jnp.int32, (gw, gw), 1)
    same = (ri // c) == (ci // c)
    lower = jnp.logical_and(same, ci <= ri)
    upper = jnp.logical_and(same, ci >= ri)
    tril = jnp.where(lower, 1.0, 0.0).astype(BF16)
    triu = jnp.where(upper, 1.0, 0.0).astype(BF16)
    ones = jnp.where(same, 1.0, 0.0).astype(BF16)

    la_scrs = (laf_scr, lab_scr)
    tris = (tril, triu)
    masks = (lower, upper)
    out_scrs = (of_scr, ob_scr)
    orders = (tuple(range(per_group)), tuple(reversed(range(per_group))))
    dirs = (0, 1)

    def body(i, carry):
        r0s = (pl.multiple_of(i * gw, gw), pl.multiple_of((n_groups - 1 - i) * gw, gw))
        rss = [pl.ds(r0, gw) for r0 in r0s]
        las = [_split3(la_scrs[d][rss[d], :]) for d in dirs]
        b = [_dot(tris[d], las[d][0]) + _dot(tris[d], las[d][1]) + _dot(tris[d], las[d][2])
             for d in dirs]
        be = [_dot(ones, las[d][0]) + _dot(ones, las[d][1]) + _dot(ones, las[d][2])
              for d in dirs]
        v = [v_ref[rss[d], :].astype(BF16) for d in dirs]
        qd = [(q_ref[rss[d], :] * scale * jnp.exp(b[d])).astype(BF16) for d in dirs]
        ki = [(k_ref[rss[d], :] * jnp.exp(-b[d])).astype(BF16) for d in dirs]
        ks = [(k_ref[rss[d], :] * jnp.exp(be[d] - b[d])).astype(BF16) for d in dirs]
        att = [jnp.where(masks[d], _dot(qd[d], ki[d], NT), 0.0).astype(BF16) for d in dirs]
        o_intra = [_dot(att[d], v[d]) for d in dirs]
        dec = [jnp.exp(be[d]) for d in dirs]
        st = [s_scr[d] for d in dirs]
        inter = [[None] * per_group for _ in dirs]
        for step in range(per_group):
            for d in dirs:
                ch = orders[d][step]
                cs = slice(ch * c, (ch + 1) * c)
                inter[d][ch] = _dot(qd[d][cs, :], st[d].astype(BF16), NT)
                st[d] = st[d] * dec[d][ch * c:ch * c + 1, :] + _dot(v[d][cs, :], ks[d][cs, :], TN)
        for d in dirs:
            s_scr[d] = st[d]
            out_scrs[d][rss[d], :] = o_intra[d] + jnp.concatenate(inter[d], axis=0)
        return carry

    lax.fori_loop(0, n_groups, body, 0)

    o = _rms(of_scr[...] + ob_scr[...]) * nw_ref[...]
    o_ref[...] = (o * _silu(g_ref[...])).astype(o_ref.dtype)
    if want_state:
        sf_ref[0] = s_scr[0].T
        sb_ref[0] = s_scr[1].T


def _gla_call(z, seq, n_seq, row_block0, wgf, bgf, wgb, bgb, nw, s0=None, want_state=False):
    has_s0 = s0 is not None

    def col(cb):
        return pl.BlockSpec((seq, LANE), lambda s, h: (row_block0 + s, cb + h))

    in_specs = [col(0), col(H_A), col(2 * H_A), col(3 * H_A),
                pl.BlockSpec((seq, LANE), lambda s, h: (row_block0 + s, (4 * W_A + 2 * W_B) // LANE)),
                pl.BlockSpec((LANE, DK_A), lambda s, h: (0, h)),
                pl.BlockSpec((1, DK_A), lambda s, h: (0, h)),
                pl.BlockSpec((LANE, DK_A), lambda s, h: (0, h)),
                pl.BlockSpec((1, DK_A), lambda s, h: (0, h)),
                pl.BlockSpec((1, DV_A), lambda s, h: (0, 0))]
    args = [z, z, z, z, z, wgf, bgf, wgb, bgb, nw]
    if has_s0:
        in_specs += [pl.BlockSpec((1, DK_A, DV_A), lambda s, h: ((s * 2 + 0) * H_A + h, 0, 0)),
                     pl.BlockSpec((1, DK_A, DV_A), lambda s, h: ((s * 2 + 1) * H_A + h, 0, 0))]
        args += [s0, s0]
    out_shape = [jax.ShapeDtypeStruct((n_seq * seq, W_A), BF16)]
    out_specs = [pl.BlockSpec((seq, LANE), lambda s, h: (s, h))]
    if want_state:
        out_shape += [jax.ShapeDtypeStruct((n_seq * H_A, DK_A, DV_A), F32)] * 2
        out_specs += [pl.BlockSpec((1, DK_A, DV_A), lambda s, h: (s * H_A + h, 0, 0))] * 2
    return pl.pallas_call(
        functools.partial(_gla_kernel, has_s0=has_s0, want_state=want_state),
        out_shape=out_shape,
        grid=(n_seq, H_A),
        in_specs=in_specs,
        out_specs=out_specs,
        scratch_shapes=[pltpu.VMEM((seq, DK_A), F32), pltpu.VMEM((seq, DK_A), F32),
                        pltpu.VMEM((seq, DV_A), F32), pltpu.VMEM((seq, DV_A), F32),
                        pltpu.VMEM((2, DK_A, DV_A), F32)],
        compiler_params=_cparams(2),
        name="gla_seq%d" % seq,
    )(*args)


def _sgu_kernel(u_ref, v_ref, nw_ref, ws_ref, bs_ref, o_ref):
    tm = u_ref.shape[0]
    u = _gelu(u_ref[...])
    v = (_rms(_gelu(v_ref[...])) * nw_ref[...]).astype(BF16)
    ws = ws_ref[0]
    bs = bs_ref[0]
    for cidx in range(tm // CHUNK_B):
        rs = slice(cidx * CHUNK_B, (cidx + 1) * CHUNK_B)
        sg = _dot(ws, v[rs, :]) + bs
        o_ref[rs, :] = (u[rs, :] * sg).astype(o_ref.dtype)


def _sgu_call(z, nw, ws_bf, bs_full, tm=512):
    t = z.shape[0]
    ub0 = (4 * W_A) // LANE
    vb0 = (4 * W_A + W_B) // LANE
    return pl.pallas_call(
        _sgu_kernel,
        out_shape=jax.ShapeDtypeStruct((t, W_B), BF16),
        grid=(t // tm, H_B),
        in_specs=[
            pl.BlockSpec((tm, LANE), lambda i, h: (i, ub0 + h)),
            pl.BlockSpec((tm, LANE), lambda i, h: (i, vb0 + h)),
            pl.BlockSpec((1, DB), lambda i, h: (0, 0)),
            pl.BlockSpec((1, CHUNK_B, CHUNK_B), lambda i, h: (h, 0, 0)),
            pl.BlockSpec((1, CHUNK_B, DB), lambda i, h: (h, 0, 0)),
        ],
        out_specs=pl.BlockSpec((tm, LANE), lambda i, h: (i, h)),
        compiler_params=_cparams(2),
        name="sgu",
    )(z, z, nw, ws_bf, bs_full)


def _outproj_kernel(a1p_ref, a1s_ref, a2p_ref, a2s_ref, w_ref, x_ref, g_ref, o_ref, *, n_prompt_tiles):
    n1 = a1p_ref.shape[1]

    def run(a1_ref, a2_ref):
        y = _dot(a1_ref[...], w_ref[:n1, :]) + _dot(a2_ref[...], w_ref[n1:, :])
        o_ref[...] = x_ref[...] + g_ref[0, 0] * y

    @pl.when(pl.program_id(0) < n_prompt_tiles)
    def _():
        run(a1p_ref, a2p_ref)

    @pl.when(pl.program_id(0) >= n_prompt_tiles)
    def _():
        run(a1s_ref, a2s_ref)


def _outproj_call(a1, a2, w_bf, x, mod4, layer, rows, tm=512):
    t, d = x.shape
    npt = rows.n_prompt_rows // tm

    def parts(a):
        if isinstance(a, tuple):
            n = a[0].shape[1]
            return (a[0], a[1],
                    pl.BlockSpec((tm, n), lambda i: (jnp.minimum(i, npt - 1), 0)),
                    pl.BlockSpec((tm, n), lambda i: (jnp.maximum(i - npt, 0), 0)))
        spec = pl.BlockSpec((tm, a.shape[1]), lambda i: (i, 0))
        return a, a, spec, spec

    a1p, a1s, s1p, s1s = parts(a1)
    a2p, a2s, s2p, s2s = parts(a2)
    n1, n2 = a1p.shape[1], a2p.shape[1]
    return pl.pallas_call(
        functools.partial(_outproj_kernel, n_prompt_tiles=npt),
        out_shape=jax.ShapeDtypeStruct((t, d), F32),
        grid=(t // tm,),
        in_specs=[
            s1p, s1s, s2p, s2s,
            pl.BlockSpec((n1 + n2, d), lambda i: (0, 0)),
            pl.BlockSpec((tm, d), lambda i: (i, 0)),
            rows.mod_spec(layer, 2, tm),
        ],
        out_specs=pl.BlockSpec((tm, d), lambda i: (i, 0)),
        compiler_params=_cparams(1),
        name="outproj",
    )(a1p, a1s, a2p, a2s, w_bf, x, mod4)


def _embed_kernel(xp_ref, xs_ref, pe_ref, o_ref, *, n_prompt_tiles):
    @pl.when(pl.program_id(0) < n_prompt_tiles)
    def _():
        o_ref[...] = xp_ref[...]

    @pl.when(pl.program_id(0) >= n_prompt_tiles)
    def _():
        o_ref[...] = xs_ref[...] + pe_ref[...]


def _embed_call(xp, xs, pe, tm=512):
    tp, d = xp.shape
    ts = xs.shape[0]
    npt = tp // tm
    pe_tiles = pe.shape[0] // tm
    return pl.pallas_call(
        functools.partial(_embed_kernel, n_prompt_tiles=npt),
        out_shape=jax.ShapeDtypeStruct((tp + ts, d), F32),
        grid=((tp + ts) // tm,),
        in_specs=[
            pl.BlockSpec((tm, d), lambda i: (jnp.minimum(i, npt - 1), 0)),
            pl.BlockSpec((tm, d), lambda i: (jnp.maximum(i - npt, 0), 0)),
            pl.BlockSpec((tm, d), lambda i: (jnp.maximum(i - npt, 0) % pe_tiles, 0)),
        ],
        out_specs=pl.BlockSpec((tm, d), lambda i: (i, 0)),
        compiler_params=_cparams(1),
        name="embed",
    )(xp, xs, pe)


def _conv_kernel(bg_ref, cg_ref, xi_ref, w_ref, o_ref, *, seq):
    n_rows = bg_ref.shape[0]
    t = cg_ref[...] * xi_ref[...]
    pos = lax.broadcasted_iota(jnp.int32, t.shape, 0) % seq
    t_prev = jnp.where(pos == 0, 0.0, pltpu.roll(t, 1, 0))
    t_next = jnp.where(pos == seq - 1, 0.0, pltpu.roll(t, n_rows - 1, 0))
    conv = t_prev * w_ref[0:1, :] + t * w_ref[1:2, :] + t_next * w_ref[2:3, :]
    o_ref[...] = (bg_ref[...] * conv).astype(o_ref.dtype)


def _conv_call(z, seq, n_seq, row_block0, conv_w, block_rows=2048):
    per_block = max(1, min(n_seq, block_rows // seq))
    rows_ = per_block * seq
    first = row_block0 * seq // rows_

    def col(cb):
        return pl.BlockSpec((rows_, LANE), lambda s, j: (first + s, cb + j))
    return pl.pallas_call(
        functools.partial(_conv_kernel, seq=seq),
        out_shape=jax.ShapeDtypeStruct((n_seq * seq, W_C), BF16),
        grid=(n_seq // per_block, W_C // LANE),
        in_specs=[col(0), col(H_C), col(2 * H_C),
                  pl.BlockSpec((3, LANE), lambda s, j: (0, j))],
        out_specs=pl.BlockSpec((rows_, LANE), lambda s, j: (s, j)),
        compiler_params=_cparams(2),
        name="conv_seq%d" % seq,
    )(z, z, z, conv_w)


def _fft_kernel(f_ref, cd_ref, sd_ref, cl_ref, sl_ref, o_ref, p_scr, q_scr, *, scale):
    @pl.when(pl.program_id(1) == 0)
    def _():
        cd = cd_ref[...]
        sd = sd_ref[...]
        for h in range(H_D):
            cs = slice(h * DD, (h + 1) * DD)
            fh = f_ref[:, cs].astype(BF16)
            p_scr[:, cs] = _dot(fh, cd).astype(BF16)
            q_scr[:, cs] = _dot(fh, sd).astype(BF16)

    y = _dot(cl_ref[...], p_scr[...]) - _dot(sl_ref[...], q_scr[...])
    o_ref[...] = (y * scale).astype(o_ref.dtype)


def _dft_mats(n):
    idx = jnp.arange(n, dtype=jnp.int32)
    ang = ((idx[:, None] * idx[None, :]) % n).astype(F32) * (2.0 * math.pi / n)
    return jnp.cos(ang).astype(BF16), jnp.sin(ang).astype(BF16)


def _fft_call(z, seq, n_seq, row_block0, cd, sd, cl, sl):
    tl = min(seq, 512)
    fcol = (3 * W_C) // W_D
    return pl.pallas_call(
        functools.partial(_fft_kernel, scale=1.0 / math.sqrt(seq * DD)),
        out_shape=jax.ShapeDtypeStruct((n_seq * seq, W_D), BF16),
        grid=(n_seq, seq // tl),
        in_specs=[
            pl.BlockSpec((seq, W_D), lambda s, i: (row_block0 + s, fcol)),
            pl.BlockSpec((DD, DD), lambda s, i: (0, 0)),
            pl.BlockSpec((DD, DD), lambda s, i: (0, 0)),
            pl.BlockSpec((tl, seq), lambda s, i: (i, 0)),
            pl.BlockSpec((tl, seq), lambda s, i: (i, 0)),
        ],
        out_specs=pl.BlockSpec((tl, W_D), lambda s, i: (s * (seq // tl) + i, 0)),
        scratch_shapes=[pltpu.VMEM((seq, W_D), BF16), pltpu.VMEM((seq, W_D), BF16)],
        compiler_params=_cparams(2),
        name="fft_seq%d" % seq,
    )(z, cd, sd, cl, sl)


def _topk16(s):
    nrows = s.shape[0]
    iota = lax.broadcasted_iota(jnp.int32, s.shape, 0)
    i16 = lax.broadcasted_iota(jnp.int32, (PEER_TOPK, s.shape[1]), 0)

    def body(k, carry):
        s, rank, top = carry
        m = jnp.max(s, axis=0, keepdims=True)
        idx = jnp.min(jnp.where(s == m, iota, nrows), axis=0, keepdims=True)
        sel = iota == idx
        rank = jnp.where(sel, k, rank)
        s = jnp.where(sel, -jnp.inf, s)
        top = jnp.where(i16 == k, m, top)
        return s, rank, top

    init = (s, jnp.full(s.shape, PEER_TOPK, jnp.int32), jnp.zeros((PEER_TOPK, s.shape[1]), F32))
    _, rank, top = lax.fori_loop(0, PEER_TOPK, body, init)
    return rank, top


_CAND_GROUPS = [(0, 16), (1, 8)] + [(k1, 8) for k1 in range(2, 8)]
_CAND_SINGLE0 = sum(n for _, n in _CAND_GROUPS)


_SCORE_FLOOR = -(2.0 ** 126)
_CODE_UNIT = 2.0 ** 127


def _rank_code(k):
    return -_CODE_UNIT * (1.0 + (k + 1) / 32.0)


def _topk16_untied(scores):
    n = scores[0].shape[1]
    i16 = lax.broadcasted_iota(jnp.int32, (PEER_TOPK, n), 0)
    keys = [jnp.maximum(s, _SCORE_FLOOR) for s in scores]
    tops = [jnp.zeros((PEER_TOPK, n), F32) for _ in scores]
    for k in range(PEER_TOPK):
        for c in range(len(keys)):
            m = jnp.max(keys[c], axis=0, keepdims=True)
            keys[c] = jnp.where(keys[c] == m, _rank_code(k), keys[c])
            tops[c] = jnp.where(i16 == k, m, tops[c])
    out = []
    for key, top in zip(keys, tops):
        taken = key < _SCORE_FLOOR
        rank = jnp.where(taken, key * -(32.0 / _CODE_UNIT) - 33.0, float(PEER_TOPK))
        count = jnp.sum(jnp.where(taken, 1.0, 0.0), axis=0, keepdims=True)
        clamped = jnp.sum(jnp.where(top <= _SCORE_FLOOR, 1.0, 0.0), axis=0, keepdims=True)
        out.append((rank, top, (count - PEER_TOPK) + clamped))
    return out


def _cand_scores(t1, t2):
    n = t1.shape[1]
    neg = jnp.full((8, n), -jnp.inf, F32)
    row8 = lax.broadcasted_iota(jnp.int32, (8, n), 0)
    pieces = []
    for k1, nrow in _CAND_GROUPS:
        blk = t1[k1:k1 + 1, :] + t2[0:nrow, :]
        nvalid = PEER_TOPK // (k1 + 1)
        if nvalid < nrow:
            blk = jnp.where(row8 < nvalid, blk, neg)
        pieces.append(blk)
    pieces.append(t1[8:16, :] + t2[0:1, :])
    return jnp.concatenate(pieces, axis=0)


def _psel_finish(s1, s2, r1, t1, t2, cand, sel):
    n = s1.shape[1]
    cmax = t1[0:1, :] + t2[0:1, :]
    z = jnp.sum(jnp.where(sel, jnp.exp(cand - cmax), 0.0), axis=0, keepdims=True)
    self32 = jnp.where(sel, 1.0, 0.0)
    cut = jnp.zeros((N_KEYS, n), F32)
    r0 = 0
    for k1, nrow in _CAND_GROUPS:
        cnt = jnp.sum(self32[r0:r0 + nrow, :], axis=0, keepdims=True)
        cut = jnp.where(r1 == k1, cnt, cut)
        r0 += nrow
    for j in range(8):
        cut = jnp.where(r1 == 8 + j, self32[r0 + j:r0 + j + 1, :], cut)
    e1 = jnp.exp(s1 - t1[0:1, :]) * (1.0 / z)
    e2 = jnp.exp(s2 - t2[0:1, :])
    return cut, e1, e2


def _psel_kernel(x_ref, nw_ref, sh_ref, sc_ref, wq_ref, sk_ref,
                 h_ref, r2_ref, e2_ref, cut_ref, e1_ref, q_scr, hb_scr):
    tq = x_ref.shape[0]
    n_chunks = tq // LANE
    hb = _modnorm(x_ref[...], nw_ref[...], sh_ref[0, 0], sc_ref[0, 0]).astype(BF16)
    h_ref[...] = hb
    hb_scr[...] = hb
    half = D_KEY // 2

    def project(hh, slot):
        rows_ = pl.ds(pl.multiple_of(hh * D_KEY, D_KEY), D_KEY)
        q_scr[slot] = _dot(wq_ref[rows_, :], hb_scr[...], NT)

    project(0, 0)

    def scores(hh):
        slot = hh % 2
        return (_dot3(sk_ref[0], q_scr[slot, pl.ds(0, half), :]),
                _dot3(sk_ref[1], q_scr[slot, pl.ds(half, half), :]))

    def write(hh, ls, r2, cut, e1, e2):
        r2_ref[hh, :, ls] = r2.astype(BF16)
        e2_ref[hh, :, ls] = e2.astype(BF16)
        cut_ref[hh, :, ls] = cut
        e1_ref[hh, :, ls] = e1

    def exact_chunk(hh, ls, s1c, s2c):
        r1, t1 = _topk16(s1c)
        r2, t2 = _topk16(s2c)
        cand = _cand_scores(t1, t2)
        crank, _ = _topk16(cand)
        cut, e1, e2 = _psel_finish(s1c, s2c, r1.astype(F32), t1, t2, cand, crank < PEER_TOPK)
        write(hh, ls, r2.astype(F32), cut, e1, e2)

    def head(hh, carry):
        s1, s2 = scores(hh)
        halves = []
        for lc in range(n_chunks):
            ls = slice(lc * LANE, (lc + 1) * LANE)
            halves.append(_topk16_untied([s1[:, ls], s2[:, ls]]))
            if lc == 0:
                project(jnp.minimum(hh + 1, PEER_HEADS - 1), 1 - hh % 2)
        cands = [_cand_scores(h1[1], h2[1]) for h1, h2 in halves]
        picked = _topk16_untied(cands)
        for lc in range(n_chunks):
            ls = slice(lc * LANE, (lc + 1) * LANE)
            (r1, t1, c1), (r2, t2, c2) = halves[lc]
            crank, _, cc = picked[lc]
            cut, e1, e2 = _psel_finish(s1[:, ls], s2[:, ls], r1, t1, t2, cands[lc], crank < PEER_TOPK)
            write(hh, ls, r2, cut, e1, e2)
            extra = c1 + c2 + cc

            @pl.when(jnp.max(extra) > 0)
            def _():
                exact_chunk(hh, ls, s1[:, ls], s2[:, ls])
        return carry

    lax.fori_loop(0, PEER_HEADS, head, 0)


def _psel_call(x, nw, mod4, layer, rows, wqt_bf, sub_keys, tq=256):
    t, d = x.shape
    nq = wqt_bf.shape[0]
    sel_shape = jax.ShapeDtypeStruct((t // tq, PEER_HEADS, N_KEYS, tq), F32)
    sel_bf = jax.ShapeDtypeStruct((t // tq, PEER_HEADS, N_KEYS, tq), BF16)
    sel_spec = pl.BlockSpec((None, PEER_HEADS, N_KEYS, tq), lambda i: (i, 0, 0, 0))
    return pl.pallas_call(
        _psel_kernel,
        out_shape=[jax.ShapeDtypeStruct((t, d), BF16), sel_bf, sel_bf, sel_shape, sel_shape],
        grid=(t // tq,),
        in_specs=[
            pl.BlockSpec((tq, d), lambda i: (i, 0)),
            pl.BlockSpec((1, d), lambda i: (0, 0)),
            rows.mod_spec(layer, 3, tq),
            rows.mod_spec(layer, 4, tq),
            pl.BlockSpec((nq, d), lambda i: (0, 0)),
            pl.BlockSpec((2, N_KEYS, D_KEY // 2), lambda i: (0, 0, 0)),
        ],
        out_specs=[pl.BlockSpec((tq, d), lambda i: (i, 0)), sel_spec, sel_spec, sel_spec, sel_spec],
        scratch_shapes=[pltpu.VMEM((2, D_KEY, tq), F32), pltpu.VMEM((tq, d), BF16)],
        compiler_params=_cparams(1),
        name="peer_select",
    )(x, nw.reshape(1, d), mod4, mod4, wqt_bf, sub_keys)


def _gelu_sigmoid_form(x):
    k = 2.0 * math.sqrt(2.0 / math.pi) * math.log2(math.e)
    u = x * (-k - (k * 0.044715) * (x * x))
    return x * (1.0 / (1.0 + jnp.exp2(u)))


def _pdense_kernel(h_ref, u_ref, v_ref, r2_ref, e2_ref, cut_ref, e1_ref, x_ref, g_ref, o_ref,
                   hid0, hid1, w0, w1, acc_scr, *, nj, n_real):
    s = pl.program_id(0)
    s3 = jnp.clip(s - 2, 0, n_real - 1)
    j3 = s3 % nj
    te, tq = hid0.shape
    bzero = jnp.zeros((), BF16)

    @pl.when(s == 0)
    def _():
        for r in (hid0, hid1, w0, w1):
            r[...] = jnp.zeros(r.shape, r.dtype)

    @pl.when(j3 == 0)
    def _():
        acc_scr[...] = jnp.zeros(acc_scr.shape, F32)

    def step(hid_new, hid_old, w_new, w_old):
        wide = 2 * LANE
        n_lc = tq // wide
        dcols = acc_scr.shape[1] // n_lc
        for a in range(te // N_KEYS):
            rs = slice(a * N_KEYS, (a + 1) * N_KEYS)
            rs2 = slice((a - a % 2) * N_KEYS, (a - a % 2 + 2) * N_KEYS)
            for lc in range(n_lc):
                ls = slice(lc * wide, (lc + 1) * wide)
                half = N_KEYS // 4
                bcast = [(jnp.broadcast_to(cut_ref[lc, hh, a:a + 1, :], (half, wide)).astype(BF16),
                          jnp.broadcast_to(e1_ref[lc, hh, a:a + 1, :], (half, wide)).astype(BF16))
                         for hh in range(PEER_HEADS)]
                for rb in range(N_KEYS // half):
                    ks_ = slice(rb * half, (rb + 1) * half)
                    rows_ = slice(a * N_KEYS + rb * half, a * N_KEYS + (rb + 1) * half)
                    gate = None
                    for hh in range(PEER_HEADS):
                        cut, e1 = bcast[hh]
                        term = jnp.where(r2_ref[lc, hh, ks_, :] < cut, e2_ref[lc, hh, ks_, :] * e1, bzero)
                        gate = term if gate is None else gate + term
                    w_new[rows_, ls] = _gelu_sigmoid_form(hid_old[rows_, ls].astype(BF16)) * gate
                if a % 2 == 0:
                    hid_new[rs2, ls] = _dot(u_ref[rs2, :], h_ref[ls, :], NT)
                else:
                    ds_ = slice(lc * dcols, (lc + 1) * dcols)
                    acc_scr[:, ds_] += _dot(w_old[rs2, :], v_ref[rs2, ds_], TN)

    @pl.when(s % 2 == 0)
    def _():
        step(hid0, hid1, w0, w1)

    @pl.when(s % 2 == 1)
    def _():
        step(hid1, hid0, w1, w0)

    @pl.when(jnp.logical_and(j3 == nj - 1, s >= 2))
    def _():
        o_ref[...] = x_ref[...] + g_ref[0, 0] * acc_scr[...]


def _pdense_call(h_bf, u_bf, v_bf, r2, e2, cut, e1, x, mod4, layer, rows, tq=512, te=2048):
    t, d = x.shape
    ne = u_bf.shape[1]
    ni, nj = t // tq, ne // te
    n_real = ni * nj
    na = te // N_KEYS
    sel_w = r2.shape[-1]
    assert sel_w == 2 * LANE and tq % sel_w == 0
    n_sel = tq // sel_w

    def tile(shift):
        def f(s):
            ss = jnp.clip(s - shift, 0, n_real - 1)
            return ss // nj, ss % nj
        return f

    t1, t2, t3 = tile(0), tile(1), tile(2)
    return pl.pallas_call(
        functools.partial(_pdense_kernel, nj=nj, n_real=n_real),
        out_shape=jax.ShapeDtypeStruct((t, d), F32),
        grid=(n_real + 2,),
        in_specs=[
            pl.BlockSpec((tq, d), lambda s: (t1(s)[0], 0)),
            pl.BlockSpec((None, te, d), lambda s: (layer, t1(s)[1], 0)),
            pl.BlockSpec((None, te, d), lambda s: (layer, t3(s)[1], 0)),
            pl.BlockSpec((n_sel, PEER_HEADS, N_KEYS, sel_w), lambda s: (t2(s)[0], 0, 0, 0)),
            pl.BlockSpec((n_sel, PEER_HEADS, N_KEYS, sel_w), lambda s: (t2(s)[0], 0, 0, 0)),
            pl.BlockSpec((n_sel, PEER_HEADS, na, sel_w), lambda s: (t2(s)[0], 0, t2(s)[1], 0)),
            pl.BlockSpec((n_sel, PEER_HEADS, na, sel_w), lambda s: (t2(s)[0], 0, t2(s)[1], 0)),
            pl.BlockSpec((tq, d), lambda s: (t3(s)[0], 0)),
            rows.mod_spec(layer, 5, tq, tile_of=lambda s: t3(s)[0]),
        ],
        out_specs=pl.BlockSpec((tq, d), lambda s: (t3(s)[0], 0)),
        scratch_shapes=[pltpu.VMEM((te, tq), F32), pltpu.VMEM((te, tq), F32),
                        pltpu.VMEM((te, tq), BF16), pltpu.VMEM((te, tq), BF16),
                        pltpu.VMEM((tq, d), F32)],
        compiler_params=_cparams(1),
        name="peer_dense",
    )(h_bf, u_bf, v_bf, r2, e2, cut, e1, x, mod4)


def _final_kernel(x_ref, w_ref, o_ref):
    o_ref[...] = _rms(x_ref[...]) * w_ref[...]


def _final_call(x, w, row_block0, n_rows, tm=512):
    d = x.shape[1]
    return pl.pallas_call(
        _final_kernel,
        out_shape=jax.ShapeDtypeStruct((n_rows, d), F32),
        grid=(n_rows // tm,),
        in_specs=[pl.BlockSpec((tm, d), lambda i: (row_block0 + i, 0)),
                  pl.BlockSpec((1, d), lambda i: (0, 0))],
        out_specs=pl.BlockSpec((tm, d), lambda i: (i, 0)),
        compiler_params=_cparams(1),
        name="final_norm",
    )(x, w.reshape(1, d))


def _grid_pos_embed(rows, dtype):
    t = jnp.arange(rows * GRID_W)
    r = (t // GRID_W).astype(F32)
    col = (t % GRID_W).astype(F32)
    nf = D_MODEL // 4
    freqs = 1.0 / (10000.0 ** (jnp.arange(nf, dtype=F32) / nf))

    def emb(p):
        a = p[:, None] * freqs[None, :]
        return jnp.concatenate([jnp.sin(a), jnp.cos(a)], axis=-1)
    return jnp.concatenate([emb(r), emb(col)], axis=-1).astype(dtype)


def _even_w_in_layout(w):
    d = w.shape[0]
    a_end = 4 * W_A
    lr = w[:, a_end:a_end + GATE_RANK]
    uv = w[:, a_end + GATE_RANK:]
    pad = jnp.zeros((d, LANE - GATE_RANK), w.dtype)
    return jnp.concatenate([w[:, :a_end], uv, lr, pad], axis=1)


def _pad_gate_w(w):
    return jnp.concatenate([w, jnp.zeros((LANE - GATE_RANK, w.shape[1]), w.dtype)], axis=0)


def kernel(x_prompt, x_sample, state_gla, c, c_ctx, w_mod, b_mod, norm_w, final_norm_w, even_w_in, even_w_gate_f, even_b_gate_f, even_w_gate_b, even_b_gate_b, gla_norm_w, sgu_norm_w, sgu_w_s, sgu_b_s, even_w_out, odd_w_in, odd_conv_w, odd_w_out, peer_w_q, peer_sub_keys, peer_u, peer_v):
    bp, lp, d = x_prompt.shape
    bs, ls, _ = x_sample.shape
    depth = w_mod.shape[0]
    n_even = even_w_in.shape[0]
    tp = bp * lp
    ts = bs * ls
    rows = _Rows(tp, ls)

    x = _embed_call(x_prompt.reshape(tp, d), x_sample.reshape(ts, d),
                    _grid_pos_embed(ls // GRID_W, x_sample.dtype))

    cc = jnp.concatenate([c_ctx[None], c, jnp.zeros((MOD_ROWS - 1 - bs, d), F32)], axis=0)
    mod4 = _mod_call(cc, w_mod, b_mod).reshape(depth, MOD_ROWS, 1, 6 * d)

    cd, sd = _dft_mats(DD)
    clp, slp = _dft_mats(lp)
    cls, sls = _dft_mats(ls)
    s0_all = state_gla.reshape(bs, n_even, 2 * H_A, DK_A, DV_A)
    u_all = peer_u.astype(BF16)
    v_all = peer_v.astype(BF16)

    new_states = []
    for l in range(depth):
        if l % 2 == 0:
            e = l // 2
            z = _normproj_call(x, norm_w[l, 0], mod4, l, rows, _even_w_in_layout(even_w_in[e]).astype(BF16))
            gate_args = (_pad_gate_w(even_w_gate_f[e]), even_b_gate_f[e][None],
                         _pad_gate_w(even_w_gate_b[e]), even_b_gate_b[e][None], gla_norm_w[e][None])
            o_p, sf, sb = _gla_call(z, lp, bp, 0, *gate_args, want_state=True)
            (o_s,) = _gla_call(z, ls, bs, tp // ls, *gate_args,
                               s0=s0_all[:, e].reshape(bs * 2 * H_A, DK_A, DV_A))
            a1 = (o_p, o_s)
            bs_full = jnp.broadcast_to(sgu_b_s[e][:, :, None], (H_B, CHUNK_B, DB))
            a2 = _sgu_call(z, sgu_norm_w[e][None], sgu_w_s[e].astype(BF16), bs_full)
            x = _outproj_call(a1, a2, even_w_out[e].astype(BF16), x, mod4, l, rows)
            sfb = jnp.stack([sf.reshape(bp, H_A, DK_A, DV_A), sb.reshape(bp, H_A, DK_A, DV_A)], axis=1)
            new_states.append(sfb.astype(x_prompt.dtype))
        else:
            o = l // 2
            z = _normproj_call(x, norm_w[l, 0], mod4, l, rows, odd_w_in[o].astype(BF16))
            yc = (_conv_call(z, lp, bp, 0, odd_conv_w[o]), _conv_call(z, ls, bs, tp // ls, odd_conv_w[o]))
            yd = (_fft_call(z, lp, bp, 0, cd, sd, clp, slp), _fft_call(z, ls, bs, tp // ls, cd, sd, cls, sls))
            x = _outproj_call(yc, yd, odd_w_out[o].astype(BF16), x, mod4, l, rows)
        h_bf, r2, e2, cut, e1 = _psel_call(x, norm_w[l, 1], mod4, l, rows,
                                           peer_w_q[l].T.astype(BF16), peer_sub_keys[l])
        x = _pdense_call(h_bf, u_all, v_all, r2, e2, cut, e1, x, mod4, l, rows)

    y_prompt = _final_call(x, final_norm_w, 0, tp).reshape(bp, lp, d)
    y_sample = _final_call(x, final_norm_w, tp // 512, ts).reshape(bs, ls, d)
    state_new = jnp.stack(new_states, axis=1)
    return (y_prompt, y_sample, state_new)
```
